```python
import jax, jax.numpy as jnp
from jax import lax
import numpy as np

D_MODEL = 2048
BATCH = 4
SEQ = 2048
DEPTH = 1
DEC_BATCH = 128
DEC_SEQ = 8
PAST_LEN = 2048
PAGE_SIZE = 128

NSA_HEADS = 8
NSA_GROUPS = 2
NSA_HPG = NSA_HEADS // NSA_GROUPS
NSA_HD = 128
NSA_BLOCK = 64
NSA_TOPN = 8
NSA_WINDOW = 512
Q_BLOCK = 128
NSA_SCALE = NSA_HD ** -0.5
FORCED_SCORE = NSA_HPG + 1.0
HG_HEADS = 8
HG_DK = 128
HG_DV = 128
HG_CHUNK = 16
D_FF = 5632
CONV_W = 3
EPS = 1e-6
IN_SIZES = (NSA_HEADS * NSA_HD, 2 * NSA_GROUPS * NSA_HD, 2 * NSA_GROUPS * NSA_HD, 2 * NSA_GROUPS * NSA_HD, 3 * NSA_HEADS, HG_HEADS * HG_DK, HG_HEADS * HG_DK, HG_HEADS * HG_DV, HG_HEADS * HG_DV, 2 * D_MODEL)
IN_COLS = sum(IN_SIZES)

kernel_name = 'nsa_hgrn2_gated_hybrid_decode_step'


def rmsnorm(x, w):
    xf = x.astype(jnp.float32)
    y = xf * lax.rsqrt(jnp.mean(xf * xf, axis=-1, keepdims=True) + EPS)
    return (y * w.astype(jnp.float32)).astype(x.dtype)


def masked_softmax(s, mask):
    s = jnp.where(mask, s.astype(jnp.float32), -jnp.inf)
    m = jnp.max(s, axis=-1, keepdims=True)
    m = jnp.where(jnp.isfinite(m), m, 0.0)
    e = jnp.where(mask, jnp.exp(s - m), 0.0)
    d = jnp.sum(e, axis=-1, keepdims=True)
    return e / jnp.where(d > 0, d, 1.0)


def alibi_slopes():
    h = jnp.arange(NSA_HEADS, dtype=jnp.float32)
    return jnp.exp2(-8.0 * (h + 1.0) / NSA_HEADS).reshape(NSA_GROUPS, NSA_HPG)


def compress_blocks(kv, pe, w1, w2):
    b, length = kv.shape[:2]
    n_cmp = length // NSA_BLOCK
    kc = kv[:, :n_cmp * NSA_BLOCK].reshape(b, n_cmp, NSA_BLOCK, 2, NSA_GROUPS, NSA_HD)
    hid = jax.nn.silu(jnp.einsum('bnlcgd,lcde->bncge', kc + pe[:, :, None, :], w1))
    out = jnp.einsum('bncge,cef->bncgf', hid, w2)
    return out[:, :, 0], out[:, :, 1]


def nsa_local(qg, q_pos, idx, kb, kv_win, k_pos_win, slopes):
    b, t = qg.shape[:2]
    n_sel = idx.shape[-1]
    idx_t = idx.transpose(0, 2, 1, 3)
    gath = jax.vmap(jax.vmap(lambda kbg, ig: kbg[ig]))(kb, idx_t)
    kpos = idx_t[..., None] * NSA_BLOCK + jnp.arange(NSA_BLOCK)
    dist = (q_pos[None, None, :, None, None] - kpos).astype(jnp.float32)
    s = jnp.einsum('btghd,bgtnsd->bgthns', qg, gath[..., 0, :]).astype(jnp.float32) * NSA_SCALE
    s = s - slopes[None, :, None, :, None, None] * dist[:, :, :, None]
    n_keys = n_sel * NSA_BLOCK
    p = masked_softmax(s.reshape(b, NSA_GROUPS, t, NSA_HPG, n_keys), (dist >= 0).reshape(b, NSA_GROUPS, t, 1, n_keys))
    o_sel = jnp.einsum('bgthk,bgtkd->btghd', p.astype(qg.dtype), gath[..., 1, :].reshape(b, NSA_GROUPS, t, n_keys, NSA_HD))
    dist_w = q_pos[:, None] - k_pos_win[None, :]
    mask_w = (k_pos_win[None, :] >= 0) & (dist_w >= 0) & (dist_w < NSA_WINDOW)
    s_w = jnp.einsum('btghd,bsgd->btghs', qg, kv_win[:, :, 0]).astype(jnp.float32) * NSA_SCALE
    s_w = s_w - slopes[None, None, :, :, None] * dist_w.astype(jnp.float32)[:, None, None, :]
    p_w = masked_softmax(s_w, mask_w[:, None, None, :])
    o_win = jnp.einsum('btghs,bsgd->btghd', p_w.astype(qg.dtype), kv_win[:, :, 1])
    return o_sel, o_win


def nsa_mixer(qg, kvc_full, kvs_full, win_src, win_pos, q_pos, gate_logits, pe, w1, w2, blocked):
    b, t = qg.shape[:2]
    length = kvc_full.shape[1]
    slopes = alibi_slopes()
    ck, cv = compress_blocks(kvc_full, pe, w1, w2)
    n_cmp = ck.shape[1]
    ends = (jnp.arange(n_cmp) + 1) * NSA_BLOCK - 1
    dist_c = q_pos[:, None] - ends[None, :]
    s_c = jnp.einsum('btghd,bngd->btghn', qg, ck).astype(jnp.float32) * NSA_SCALE
    s_c = s_c - slopes[None, None, :, :, None] * dist_c.astype(jnp.float32)[:, None, None, :]
    p_c = masked_softmax(s_c, (dist_c >= 0)[:, None, None, :])
    o_cmp = jnp.einsum('btghn,bngd->btghd', p_c.astype(cv.dtype), cv)
    n_blk = -(-length // NSA_BLOCK)
    imp = jnp.pad(p_c.sum(axis=3), ((0, 0), (0, 0), (0, 0), (0, n_blk - n_cmp)))
    cur = q_pos // NSA_BLOCK
    blk = jnp.arange(n_blk)
    forced = (blk[None, :] == 0) | (blk[None, :] == cur[:, None]) | (blk[None, :] == cur[:, None] - 1)
    valid = blk[None, :] <= cur[:, None]
    score = jnp.where(valid[None, :, None, :], jnp.where(forced[None, :, None, :], FORCED_SCORE, imp), -1.0)
    n_sel = min(NSA_TOPN, n_blk)
    _, idx = lax.top_k(score, n_sel)
    kvs_pad = jnp.pad(kvs_full, ((0, 0), (0, n_blk * NSA_BLOCK - length), (0, 0), (0, 0), (0, 0)))
    kb = kvs_pad.reshape(b, n_blk, NSA_BLOCK, 2, NSA_GROUPS, NSA_HD).transpose(0, 4, 1, 2, 3, 5)
    if blocked:
        nq = t // Q_BLOCK
        n_span = NSA_WINDOW // Q_BLOCK + 1
        padded = jnp.pad(win_src, ((0, 0), (NSA_WINDOW, 0), (0, 0), (0, 0), (0, 0)))
        pb = padded.reshape(b, nq + n_span - 1, Q_BLOCK, 2, NSA_GROUPS, NSA_HD)
        spans = jnp.stack([pb[:, i:i + nq] for i in range(n_span)], axis=2).reshape(b, nq, n_span * Q_BLOCK, 2, NSA_GROUPS, NSA_HD)
        span_pos = jnp.arange(nq)[:, None] * Q_BLOCK - NSA_WINDOW + jnp.arange(n_span * Q_BLOCK)[None, :]
        xs = (qg.reshape(b, nq, Q_BLOCK, NSA_GROUPS, NSA_HPG, NSA_HD).transpose(1, 0, 2, 3, 4, 5),
              q_pos.reshape(nq, Q_BLOCK),
              idx.reshape(b, nq, Q_BLOCK, NSA_GROUPS, n_sel).transpose(1, 0, 2, 3, 4),
              spans.transpose(1, 0, 2, 3, 4, 5),
              span_pos)
        o_sel, o_win = lax.map(lambda a: nsa_local(a[0], a[1], a[2], kb, a[3], a[4], slopes), xs)
        o_sel = o_sel.transpose(1, 0, 2, 3, 4, 5).reshape(b, t, NSA_GROUPS, NSA_HPG, NSA_HD)
        o_win = o_win.transpose(1, 0, 2, 3, 4, 5).reshape(b, t, NSA_GROUPS, NSA_HPG, NSA_HD)
    else:
        o_sel, o_win = nsa_local(qg, q_pos, idx, kb, win_src, win_pos, slopes)
    g = jax.nn.sigmoid(gate_logits.astype(jnp.float32)).reshape(b, t, NSA_GROUPS, NSA_HPG, 3).astype(qg.dtype)
    o = g[..., 0:1] * o_cmp + g[..., 1:2] * o_sel + g[..., 2:3] * o_win
    return o.reshape(b, t, NSA_HEADS * NSA_HD)


def hgrn2_chunked(q, k, v, g, s0):
    b, t, h, dk = q.shape
    dv = v.shape[-1]
    pad = (-t) % HG_CHUNK
    n = (t + pad) // HG_CHUNK

    def blocks(a):
        a = jnp.pad(a.astype(jnp.float32), ((0, 0), (0, pad), (0, 0), (0, 0)))
        return a.reshape(b, n, HG_CHUNK, h, a.shape[-1]).transpose(1, 0, 2, 3, 4)

    qc, kc, vc, gc = blocks(q), blocks(k), blocks(v), blocks(g)
    cum = jnp.cumsum(gc, axis=2)
    last = cum[:, :, -1]
    q_dec = qc * jnp.exp(cum)
    k_inv = kc * jnp.exp(-cum)
    k_end = kc * jnp.exp(last[:, :, None] - cum)
    causal = jnp.tril(jnp.ones((HG_CHUNK, HG_CHUNK), dtype=bool))
    att = jnp.where(causal, jnp.einsum('nbthk,nbshk->nbhts', q_dec, k_inv), 0.0)
    o_intra = jnp.einsum('nbhts,nbshv->nbthv', att, vc)

    def step(s, xs):
        qd, ke, vv, la = xs
        o = jnp.einsum('bthk,bhkv->bthv', qd, s)
        s = jnp.exp(la)[..., None] * s + jnp.einsum('bshk,bshv->bhkv', ke, vv)
        return s, o

    s_fin, o_inter = lax.scan(step, s0.astype(jnp.float32), (q_dec, k_end, vc, last))
    o = (o_intra + o_inter).transpose(1, 0, 2, 3, 4).reshape(b, n * HG_CHUNK, h, dv)[:, :t]
    return o, s_fin


def conv_ffn(h, buf, w_up, cw, cb, w_dn):
    t = h.shape[1]
    u = h @ w_up
    full = jnp.concatenate([buf.astype(u.dtype), u], axis=1)
    c = cb
    for j in range(CONV_W):
        c = c + full[:, j:j + t] * cw[j]
    a, gt = jnp.split(c, 2, axis=-1)
    out = (jax.nn.gelu(gt, approximate=True) * a) @ w_dn
    return out, full[:, t:]


def decoder_layer(x, past, q_pos0, lb, params):
    (w_in, w_a, w_b, w_o, pe, w1, w2, hg_norm, ln1, ln2, ln3, ln4, w_up, cw, cb, w_dn) = params
    b, t, _ = x.shape
    h = rmsnorm(x, ln1)
    splits = [int(v) for v in np.cumsum(IN_SIZES)[:-1]]
    (q_a, kvc, kvs, kvw, g_nsa, q_b, f_b, i_b, og_b, g_mrg) = jnp.split(h @ w_in, splits, axis=-1)
    kv_shape = (b, t, 2, NSA_GROUPS, NSA_HD)
    kvc, kvs, kvw = kvc.reshape(kv_shape), kvs.reshape(kv_shape), kvw.reshape(kv_shape)
    qg = q_a.reshape(b, t, NSA_GROUPS, NSA_HPG, NSA_HD)
    q_pos = q_pos0 + jnp.arange(t)
    if past is None:
        kvc_full, kvs_full, win_src, win_pos = kvc, kvs, kvw, None
        new_win = kvw[:, t - min(NSA_WINDOW, t):]
        s0 = jnp.zeros((b, HG_HEADS, HG_DK, HG_DV), jnp.float32)
        conv_buf = jnp.zeros((b, CONV_W - 1, 2 * D_FF), x.dtype)
    else:
        past_c, past_s, win_buf, s0, conv_buf = past
        kvc_full = jnp.concatenate([past_c.astype(kvc.dtype), kvc], axis=1)
        kvs_full = jnp.concatenate([past_s.astype(kvs.dtype), kvs], axis=1)
        n_buf = win_buf.shape[1]
        win_src = jnp.concatenate([win_buf.astype(kvw.dtype), kvw], axis=1)
        win_pos = q_pos0 - n_buf + jnp.arange(n_buf + t)
        new_win = win_src[:, t:]
    o_a = nsa_mixer(qg, kvc_full, kvs_full, win_src, win_pos, q_pos, g_nsa, pe, w1, w2, past is None)
    f_raw = f_b.reshape(b, t, HG_HEADS, HG_DK).astype(jnp.float32)
    lbh = lb.reshape(HG_HEADS, HG_DK)
    forget = lbh + (1.0 - lbh) * jax.nn.sigmoid(f_raw)
    in_gate = (1.0 - lbh) * jax.nn.sigmoid(-f_raw)
    o_hg, s_fin = hgrn2_chunked(q_b.reshape(b, t, HG_HEADS, HG_DK), in_gate, i_b.reshape(b, t, HG_HEADS, HG_DV), jnp.log(forget), s0)
    o_hg = rmsnorm(o_hg, hg_norm).reshape(b, t, HG_HEADS * HG_DV).astype(x.dtype) * jax.nn.silu(og_b)
    ga, gb = jnp.split(g_mrg, 2, axis=-1)
    mixed = (jax.nn.sigmoid(ga) * (o_a @ w_a) + jax.nn.sigmoid(gb) * (o_hg @ w_b)) @ w_o
    x1 = x + rmsnorm(mixed, ln2)
    ffn, new_conv = conv_ffn(rmsnorm(x1, ln3), conv_buf, w_up, cw, cb, w_dn)
    y = x1 + rmsnorm(ffn, ln4)
    return y, (kvc, kvs, new_win, s_fin.astype(x.dtype), new_conv)


def setup_inputs(seed: int = 0) -> dict:
    key = jax.random.key(seed)
    ks = jax.random.split(key, 32)
    n_pages = PAST_LEN // PAGE_SIZE
    n_used = DEC_BATCH * n_pages
    n_phys = n_used + n_used // 4
    n_buf = min(NSA_WINDOW, PAST_LEN)

    def nrm(k, shape, scale):
        return jax.random.normal(k, shape, jnp.float32) * scale

    page_table = jax.random.permutation(ks[7], n_phys)[:n_used].reshape(DEC_BATCH, n_pages).astype(jnp.int32)
    kv_tail = (2, NSA_GROUPS, NSA_HD)
    return {
        'x_prompt': nrm(ks[0], (BATCH, SEQ, D_MODEL), 1.0),
        'x_sample': nrm(ks[1], (DEC_BATCH, DEC_SEQ, D_MODEL), 1.0),
        'cache_cmp_kv': nrm(ks[2], (DEPTH, n_phys, PAGE_SIZE) + kv_tail, 1.0),
        'cache_sel_kv': nrm(ks[3], (DEPTH, n_phys, PAGE_SIZE) + kv_tail, 1.0),
        'state_win_kv': nrm(ks[4], (DEPTH, DEC_BATCH, n_buf) + kv_tail, 1.0),
        'state_hgrn': nrm(ks[5], (DEPTH, DEC_BATCH, HG_HEADS, HG_DK, HG_DV), 0.5),
        'state_conv': nrm(ks[6], (DEPTH, DEC_BATCH, CONV_W - 1, 2 * D_FF), 1.0),
        'page_table': page_table,
        'w_in': nrm(ks[8], (DEPTH, D_MODEL, IN_COLS), D_MODEL ** -0.5),
        'w_branch_a': nrm(ks[9], (DEPTH, NSA_HEADS * NSA_HD, D_MODEL), (NSA_HEADS * NSA_HD) ** -0.5),
        'w_branch_b': nrm(ks[10], (DEPTH, HG_HEADS * HG_DV, D_MODEL), (HG_HEADS * HG_DV) ** -0.5),
        'w_out': nrm(ks[11], (DEPTH, D_MODEL, D_MODEL), D_MODEL ** -0.5),
        'cmp_pe': nrm(ks[12], (DEPTH, NSA_BLOCK, 2, NSA_HD), 0.1),
        'cmp_w1': nrm(ks[13], (DEPTH, NSA_BLOCK, 2, NSA_HD, NSA_HD), (NSA_BLOCK * NSA_HD) ** -0.5),
        'cmp_w2': nrm(ks[14], (DEPTH, 2, NSA_HD, NSA_HD), NSA_HD ** -0.5),
        'hg_lb_raw': nrm(ks[15], (DEPTH + 1, HG_HEADS * HG_DK), 0.5),
        'hg_norm_w': 1.0 + nrm(ks[16], (DEPTH, HG_DV), 0.02),
        'ln_mix_pre': 1.0 + nrm(ks[17], (DEPTH, D_MODEL), 0.02),
        'ln_mix_post': 1.0 + nrm(ks[18], (DEPTH, D_MODEL), 0.02),
        'ln_ffn_pre': 1.0 + nrm(ks[19], (DEPTH, D_MODEL), 0.02),
        'ln_ffn_post': 1.0 + nrm(ks[20], (DEPTH, D_MODEL), 0.02),
        'w_up': nrm(ks[21], (DEPTH, D_MODEL, 2 * D_FF), D_MODEL ** -0.5),
        'conv_w': nrm(ks[22], (DEPTH, CONV_W, 2 * D_FF), CONV_W ** -0.5),
        'conv_b': nrm(ks[23], (DEPTH, 2 * D_FF), 0.01),
        'w_down': nrm(ks[24], (DEPTH, D_FF, D_MODEL), D_FF ** -0.5),
    }


def reference(x_prompt, x_sample, cache_cmp_kv, cache_sel_kv, state_win_kv, state_hgrn, state_conv, page_table, w_in, w_branch_a, w_branch_b, w_out, cmp_pe, cmp_w1, cmp_w2, hg_lb_raw, hg_norm_w, ln_mix_pre, ln_mix_post, ln_ffn_pre, ln_ffn_post, w_up, conv_w, conv_b, w_down):
    dec_b, n_pages = page_table.shape
    past_len = n_pages * cache_cmp_kv.shape[2]
    lb_all = jnp.cumsum(jax.nn.softmax(hg_lb_raw.astype(jnp.float32), axis=0), axis=0)
    y_p, y_s = x_prompt, x_sample
    new_p, new_s = [], []
    for l in range(DEPTH):
        params = (w_in[l], w_branch_a[l], w_branch_b[l], w_out[l], cmp_pe[l], cmp_w1[l], cmp_w2[l], hg_norm_w[l],
                  ln_mix_pre[l], ln_mix_post[l], ln_ffn_pre[l], ln_ffn_post[l], w_up[l], conv_w[l], conv_b[l], w_down[l])
        y_p, st_p = decoder_layer(y_p, None, 0, lb_all[l], params)
        past = (cache_cmp_kv[l][page_table].reshape(dec_b, past_len, 2, NSA_GROUPS, NSA_HD),
                cache_sel_kv[l][page_table].reshape(dec_b, past_len, 2, NSA_GROUPS, NSA_HD),
                state_win_kv[l], state_hgrn[l], state_conv[l])
        y_s, st_s = decoder_layer(y_s, past, past_len, lb_all[l], params)
        new_p.append(st_p)
        new_s.append(st_s)

    def stack(group, i):
        return jnp.stack([st[i] for st in group], axis=0)

    return (y_p, y_s, stack(new_p, 0), stack(new_s, 0), stack(new_p, 1), stack(new_s, 1), stack(new_p, 2), stack(new_s, 2), stack(new_p, 3), stack(new_s, 3), stack(new_p, 4), stack(new_s, 4))
```

```python
import functools

import jax
import jax.numpy as jnp
from jax import lax
from jax.experimental import pallas as pl
from jax.experimental.pallas import tpu as pltpu

F32 = jnp.float32
BF16 = jnp.bfloat16

D_MODEL = 2048
NSA_HEADS = 8
NSA_GROUPS = 2
NSA_HPG = NSA_HEADS // NSA_GROUPS
NSA_HD = 128
NSA_BLOCK = 64
NSA_TOPN = 8
NSA_WINDOW = 512
NSA_SCALE = NSA_HD ** -0.5
FORCED_SCORE = NSA_HPG + 1.0
HG_HEADS = 8
HG_DK = 128
HG_DV = 128
HG_CHUNK = 16
D_FF = 5632
CONV_W = 3
EPS = 1e-6
KV_COLS = 2 * NSA_GROUPS * NSA_HD

C_QA = 0
C_QB = 1024
C_FB = 2048
C_IB = 3072
C_OG = 4096
C_GA = 5120
C_GB = 7168
C_KVC = 9216
C_KVS = 9728
C_KVW = 10240
C_GN = 10752
N_PROJ = 10880

LANES = 128
NEG_BIG = -1e30
VMEM_LIMIT = 56 * 1024 * 1024


def _cparams(*sem):
    return pltpu.CompilerParams(dimension_semantics=sem, vmem_limit_bytes=VMEM_LIMIT)


def _mm(a, b):
    return jnp.dot(a.astype(BF16), b.astype(BF16), preferred_element_type=F32)


def _mm_nt(a, b):
    return lax.dot_general(a.astype(BF16), b.astype(BF16), (((1,), (1,)), ((), ())),
                           preferred_element_type=F32)


def _norm_mm_kernel(x_ref, ln_ref, w_ref, o_ref, h_ref):
    @pl.when(pl.program_id(1) == 0)
    def _():
        x = x_ref[...]
        ms = jnp.mean(x * x, axis=-1, keepdims=True)
        h_ref[...] = (x * lax.rsqrt(ms + EPS) * ln_ref[...]).astype(BF16)

    o_ref[...] = _mm(h_ref[...], w_ref[...])


def _norm_matmul(x, ln, w, tm, tn):
    m, k = x.shape
    n = w.shape[1]
    return pl.pallas_call(
        _norm_mm_kernel,
        grid=(m // tm, n // tn),
        in_specs=[pl.BlockSpec((tm, k), lambda i, j: (i, 0)),
                  pl.BlockSpec((1, k), lambda i, j: (0, 0)),
                  pl.BlockSpec((k, tn), lambda i, j: (0, j))],
        out_specs=pl.BlockSpec((tm, tn), lambda i, j: (i, j)),
        out_shape=jax.ShapeDtypeStruct((m, n), F32),
        scratch_shapes=[pltpu.VMEM((tm, k), BF16)],
        compiler_params=_cparams("parallel", "arbitrary"),
        name="norm_matmul",
    )(x, ln.reshape(1, k), w)


def _mm_norm_res_kernel(a_ref, w_ref, res_ref, ln_ref, o_ref, acc_ref, *, nk):
    k = pl.program_id(1)

    @pl.when(k == 0)
    def _():
        acc_ref[...] = jnp.zeros_like(acc_ref)

    acc_ref[...] += _mm(a_ref[...], w_ref[...])

    @pl.when(k == nk - 1)
    def _():
        y = acc_ref[...]
        ms = jnp.mean(y * y, axis=-1, keepdims=True)
        o_ref[...] = res_ref[...] + y * lax.rsqrt(ms + EPS) * ln_ref[...]


def _matmul_norm_res(a, w, res, ln, tm, tk):
    m, kk = a.shape
    n = w.shape[1]
    nk = kk // tk
    return pl.pallas_call(
        functools.partial(_mm_norm_res_kernel, nk=nk),
        grid=(m // tm, nk),
        in_specs=[pl.BlockSpec((tm, tk), lambda i, k: (i, k)),
                  pl.BlockSpec((tk, n), lambda i, k: (k, 0)),
                  pl.BlockSpec((tm, n), lambda i, k: (i, 0)),
                  pl.BlockSpec((1, n), lambda i, k: (0, 0))],
        out_specs=pl.BlockSpec((tm, n), lambda i, k: (i, 0)),
        out_shape=jax.ShapeDtypeStruct((m, n), F32),
        scratch_shapes=[pltpu.VMEM((tm, n), F32)],
        compiler_params=_cparams("parallel", "arbitrary"),
        name="matmul_norm_res",
    )(a, w, res, ln.reshape(1, n))


def _merge_kernel(oa_ref, ohg_ref, wa_ref, wb_ref, ga_ref, gb_ref, o_ref):
    a = _mm(oa_ref[...], wa_ref[...])
    b = _mm(ohg_ref[...], wb_ref[...])
    o_ref[...] = (jax.nn.sigmoid(ga_ref[...]) * a + jax.nn.sigmoid(gb_ref[...]) * b).astype(BF16)


def _gated_merge(o_a, o_hg, w_a, w_b, proj, tm, tn):
    m, k = o_a.shape
    n = w_a.shape[1]
    return pl.pallas_call(
        _merge_kernel,
        grid=(m // tm, n // tn),
        in_specs=[pl.BlockSpec((tm, k), lambda i, j: (i, 0)),
                  pl.BlockSpec((tm, k), lambda i, j: (i, 0)),
                  pl.BlockSpec((k, tn), lambda i, j: (0, j)),
                  pl.BlockSpec((k, tn), lambda i, j: (0, j)),
                  pl.BlockSpec((tm, tn), lambda i, j: (i, C_GA // tn + j)),
                  pl.BlockSpec((tm, tn), lambda i, j: (i, C_GB // tn + j))],
        out_specs=pl.BlockSpec((tm, tn), lambda i, j: (i, j)),
        out_shape=jax.ShapeDtypeStruct((m, n), BF16),
        compiler_params=_cparams("parallel", "arbitrary"),
        name="gated_merge",
    )(o_a, o_hg, w_a, w_b, proj, proj)


def _conv_taps(fa, fg, cwa_ref, cwg_ref, cba_ref, cbg_ref, tap):
    ca = cba_ref[...]
    cg = cbg_ref[...]
    for j in range(CONV_W):
        ca = ca + tap(fa, j) * cwa_ref[j:j + 1, :]
        cg = cg + tap(fg, j) * cwg_ref[j:j + 1, :]
    return jax.nn.gelu(cg, approximate=True) * ca


def _conv_glu_seq_kernel(ua_ref, ug_ref, ha_ref, hg_ref, cwa_ref, cwg_ref, cba_ref, cbg_ref,
                         o_ref, fa_ref, fg_ref, *, tm, tiles_per_seq):
    first = (pl.program_id(0) % tiles_per_seq) == 0
    fa_ref[0:8, :] = jnp.where(first, 0.0, ha_ref[...])
    fg_ref[0:8, :] = jnp.where(first, 0.0, hg_ref[...])
    fa_ref[8:, :] = ua_ref[...]
    fg_ref[8:, :] = ug_ref[...]
    tap = lambda f, j: f[8 - (CONV_W - 1) + j:8 - (CONV_W - 1) + j + tm, :]
    o_ref[...] = _conv_taps(fa_ref, fg_ref, cwa_ref, cwg_ref, cba_ref, cbg_ref, tap).astype(o_ref.dtype)


def _conv_glu_seq(u, conv_w, conv_b, seq_len, tm, tn):
    m = u.shape[0]
    nj = D_FF // tn
    hb = tm // 8
    halo = lambda i, j, off: (jnp.maximum(i * hb - 1, 0), j + off)
    return pl.pallas_call(
        functools.partial(_conv_glu_seq_kernel, tm=tm, tiles_per_seq=seq_len // tm),
        grid=(m // tm, nj),
        in_specs=[pl.BlockSpec((tm, tn), lambda i, j: (i, j)),
                  pl.BlockSpec((tm, tn), lambda i, j: (i, j + nj)),
                  pl.BlockSpec((8, tn), lambda i, j: halo(i, j, 0)),
                  pl.BlockSpec((8, tn), lambda i, j: halo(i, j, nj)),
                  pl.BlockSpec((CONV_W, tn), lambda i, j: (0, j)),
                  pl.BlockSpec((CONV_W, tn), lambda i, j: (0, j + nj)),
                  pl.BlockSpec((1, tn), lambda i, j: (0, j)),
                  pl.BlockSpec((1, tn), lambda i, j: (0, j + nj))],
        out_specs=pl.BlockSpec((tm, tn), lambda i, j: (i, j)),
        out_shape=jax.ShapeDtypeStruct((m, D_FF), BF16),
        scratch_shapes=[pltpu.VMEM((tm + 8, tn), F32), pltpu.VMEM((tm + 8, tn), F32)],
        compiler_params=_cparams("parallel", "arbitrary"),
        name="conv_glu_seq",
    )(u, u, u, u, conv_w, conv_w, conv_b.reshape(1, -1), conv_b.reshape(1, -1))


def _conv_glu_step_kernel(ua_ref, ug_ref, ba_ref, bg_ref, cwa_ref, cwg_ref, cba_ref, cbg_ref,
                          o_ref, fa_ref, fg_ref, *, t):
    fa_ref[:, 8 - (CONV_W - 1):8, :] = ba_ref[...]
    fg_ref[:, 8 - (CONV_W - 1):8, :] = bg_ref[...]
    fa_ref[:, 8:, :] = ua_ref[...]
    fg_ref[:, 8:, :] = ug_ref[...]
    tap = lambda f, j: f[:, 8 - (CONV_W - 1) + j:8 - (CONV_W - 1) + j + t, :]
    o_ref[...] = _conv_taps(fa_ref, fg_ref, cwa_ref, cwg_ref, cba_ref, cbg_ref, tap)


def _conv_glu_step(u3, buf, conv_w, conv_b, nb, tn):
    b, t, _ = u3.shape
    nj = D_FF // tn
    return pl.pallas_call(
        functools.partial(_conv_glu_step_kernel, t=t),
        grid=(b // nb, nj),
        in_specs=[pl.BlockSpec((nb, t, tn), lambda i, j: (i, 0, j)),
                  pl.BlockSpec((nb, t, tn), lambda i, j: (i, 0, j + nj)),
                  pl.BlockSpec((nb, CONV_W - 1, tn), lambda i, j: (i, 0, j)),
                  pl.BlockSpec((nb, CONV_W - 1, tn), lambda i, j: (i, 0, j + nj)),
                  pl.BlockSpec((CONV_W, tn), lambda i, j: (0, j)),
                  pl.BlockSpec((CONV_W, tn), lambda i, j: (0, j + nj)),
                  pl.BlockSpec((1, tn), lambda i, j: (0, j)),
                  pl.BlockSpec((1, tn), lambda i, j: (0, j + nj))],
        out_specs=pl.BlockSpec((nb, t, tn), lambda i, j: (i, 0, j)),
        out_shape=jax.ShapeDtypeStruct((b, t, D_FF), F32),
        scratch_shapes=[pltpu.VMEM((nb, 8 + t, tn), F32), pltpu.VMEM((nb, 8 + t, tn), F32)],
        compiler_params=_cparams("parallel", "arbitrary"),
        name="conv_glu_step",
    )(u3, u3, buf, buf, conv_w, conv_w, conv_b.reshape(1, -1), conv_b.reshape(1, -1))


def _compress_body(xs, pe_ref, w1_ref, w2_ref, o_ref, n_cmp):
    for c in range(2):

        def step(l, acc, c=c):
            x = jnp.concatenate([xs[c * NSA_GROUPS + g][pl.ds(l, n_cmp, stride=NSA_BLOCK), :]
                                 for g in range(NSA_GROUPS)], axis=0)
            x = x + pe_ref[c, l]
            return acc + _mm(x, w1_ref[l, c])

        acc = lax.fori_loop(0, NSA_BLOCK, step, jnp.zeros((NSA_GROUPS * n_cmp, NSA_HD), F32))
        out = _mm(jax.nn.silu(acc), w2_ref[c])
        for g in range(NSA_GROUPS):
            o_ref[0, c, g] = out[g * n_cmp:(g + 1) * n_cmp]


def _compress_seq_kernel(x0_ref, x1_ref, x2_ref, x3_ref, pe_ref, w1_ref, w2_ref, o_ref, *, n_cmp):
    _compress_body((x0_ref, x1_ref, x2_ref, x3_ref), pe_ref, w1_ref, w2_ref, o_ref, n_cmp)


def _compress_seq(proj, b, t, pe, w1, w2):
    n_cmp = t // NSA_BLOCK
    rows = n_cmp * NSA_BLOCK
    assert rows == t
    return pl.pallas_call(
        functools.partial(_compress_seq_kernel, n_cmp=n_cmp),
        grid=(b,),
        in_specs=[pl.BlockSpec((rows, NSA_HD), lambda i, cg=cg: (i, C_KVC // NSA_HD + cg))
                  for cg in range(2 * NSA_GROUPS)] + [
                  pl.BlockSpec(pe.shape, lambda i: (0, 0, 0, 0)),
                  pl.BlockSpec(w1.shape, lambda i: (0, 0, 0, 0)),
                  pl.BlockSpec(w2.shape, lambda i: (0, 0, 0))],
        out_specs=pl.BlockSpec((1, 2, NSA_GROUPS, n_cmp, NSA_HD), lambda i: (i, 0, 0, 0, 0)),
        out_shape=jax.ShapeDtypeStruct((b, 2, NSA_GROUPS, n_cmp, NSA_HD), F32),
        compiler_params=_cparams("parallel"),
        name="compress_seq",
    )(proj, proj, proj, proj, pe, w1, w2)


def _compress_paged_kernel(pt_ref, *refs, n_pages, page):
    del pt_ref
    pages = refs[:n_pages]
    pe_ref, w1_ref, w2_ref, o_ref, x_ref = refs[n_pages:]
    n_cg = 2 * NSA_GROUPS
    for p in range(n_pages):
        for cg in range(n_cg):
            x_ref[cg, p * page:(p + 1) * page, :] = pages[p][0, :, cg * NSA_HD:(cg + 1) * NSA_HD]
    _compress_body([x_ref.at[cg] for cg in range(n_cg)], pe_ref, w1_ref, w2_ref, o_ref,
                   n_pages * page // NSA_BLOCK)


def _compress_paged(cache, page_table_flat, b, n_pages, pe, w1, w2):
    page = cache.shape[1]
    n_cmp = n_pages * page // NSA_BLOCK
    page_spec = lambda p: pl.BlockSpec((1, page, KV_COLS), lambda i, pt: (pt[i * n_pages + p], 0, 0))
    grid_spec = pltpu.PrefetchScalarGridSpec(
        num_scalar_prefetch=1,
        grid=(b,),
        in_specs=[page_spec(p) for p in range(n_pages)] + [
            pl.BlockSpec(pe.shape, lambda i, pt: (0, 0, 0, 0)),
            pl.BlockSpec(w1.shape, lambda i, pt: (0, 0, 0, 0)),
            pl.BlockSpec(w2.shape, lambda i, pt: (0, 0, 0))],
        out_specs=pl.BlockSpec((1, 2, NSA_GROUPS, n_cmp, NSA_HD), lambda i, pt: (i, 0, 0, 0, 0)),
        scratch_shapes=[pltpu.VMEM((2 * NSA_GROUPS, n_pages * page, NSA_HD), F32)],
    )
    return pl.pallas_call(
        functools.partial(_compress_paged_kernel, n_pages=n_pages, page=page),
        grid_spec=grid_spec,
        out_shape=jax.ShapeDtypeStruct((b, 2, NSA_GROUPS, n_cmp, NSA_HD), F32),
        compiler_params=_cparams("parallel"),
        name="compress_paged",
    )(page_table_flat, *([cache] * n_pages), pe, w1, w2)


def _row_consts(tq, g, q_pos0):
    rows = NSA_HPG * tq
    r = lax.broadcasted_iota(jnp.int32, (rows, 1), 0)
    head = r // tq
    qpos = q_pos0 + (r - head * tq)
    slope = jnp.zeros((rows, 1), F32)
    for h in range(NSA_HPG):
        slope = jnp.where(head == h, 2.0 ** (-8.0 * (g * NSA_HPG + h + 1.0) / NSA_HEADS), slope)
    return qpos, slope


def _stack_heads(q_ref, g):
    return jnp.concatenate(
        [q_ref[:, (g * NSA_HPG + h) * NSA_HD:(g * NSA_HPG + h + 1) * NSA_HD] for h in range(NSA_HPG)], axis=0)


def _compressed_branch(qs, ck, cv, qpos, slope, n_cmp):
    pad = jnp.zeros((LANES - n_cmp, NSA_HD), F32)
    ckp = jnp.concatenate([ck, pad], axis=0)
    cvp = jnp.concatenate([cv, pad], axis=0)
    n = lax.broadcasted_iota(jnp.int32, (1, LANES), 1)
    dist = qpos - ((n + 1) * NSA_BLOCK - 1)
    s = _mm_nt(qs, ckp) * NSA_SCALE - slope * dist.astype(F32)
    mask = (dist >= 0) & (n < n_cmp)
    s = jnp.where(mask, s, NEG_BIG)
    m = jnp.max(s, axis=-1, keepdims=True)
    e = jnp.where(mask, jnp.exp(s - m), 0.0)
    d = jnp.sum(e, axis=-1, keepdims=True)
    p = e / jnp.where(d > 0, d, 1.0)
    return _mm(p, cvp), p


def _select_blocks(imp, qpos_t, n_blk):
    blk = lax.broadcasted_iota(jnp.int32, imp.shape, 1)
    cur = qpos_t // NSA_BLOCK
    forced = (blk == 0) | (blk == cur) | (blk == cur - 1)
    valid = blk <= cur
    score = jnp.where(valid, jnp.where(forced, FORCED_SCORE, imp), -1.0)
    score = jnp.where(blk < n_blk, score, -2.0)
    blk_f = blk.astype(F32)
    sel = jnp.zeros(imp.shape, F32)
    for _ in range(min(NSA_TOPN, n_blk)):
        mx = jnp.max(score, axis=-1, keepdims=True)
        first = jnp.min(jnp.where(score == mx, blk_f, 1e9), axis=-1, keepdims=True)
        hit = blk_f == first
        sel = jnp.where(hit, 1.0, sel)
        score = jnp.where(hit, -3.0, score)
    return sel


def _expand_sel(sel, key0, nkeys):
    bi = lax.broadcasted_iota(jnp.int32, (LANES, nkeys), 0)
    ki = lax.broadcasted_iota(jnp.int32, (LANES, nkeys), 1)
    expand = (bi == (key0 + ki) // NSA_BLOCK).astype(BF16)
    m = _mm(sel, expand)
    return jnp.concatenate([m] * NSA_HPG, axis=0)


def _online_update(carry, qs, k, v, valid, bias):
    m_i, l_i, acc = carry
    s = _mm_nt(qs, k) * NSA_SCALE - bias
    s = jnp.where(valid, s, NEG_BIG)
    m_new = jnp.maximum(m_i, jnp.max(s, axis=-1, keepdims=True))
    alpha = jnp.exp(m_i - m_new)
    p = jnp.where(valid, jnp.exp(s - m_new), 0.0)
    l_new = alpha * l_i + jnp.sum(p, axis=-1, keepdims=True)
    acc_new = alpha * acc + _mm(p, v)
    return m_new, l_new, acc_new


def _init_carry(rows):
    return (jnp.full((rows, 1), NEG_BIG, F32), jnp.zeros((rows, 1), F32), jnp.zeros((rows, NSA_HD), F32))


def _finish(carry):
    _, l_i, acc = carry
    return acc / jnp.where(l_i > 0, l_i, 1.0)


def _write_gated(o_ref, gate_ref, g, tq, o_cmp, o_sel, o_win):
    sig = jax.nn.sigmoid(gate_ref[...])
    for h in range(NSA_HPG):
        c0 = (g * NSA_HPG + h) * 3
        rs = slice(h * tq, (h + 1) * tq)
        o = sig[:, c0:c0 + 1] * o_cmp[rs] + sig[:, c0 + 1:c0 + 2] * o_sel[rs] + sig[:, c0 + 2:c0 + 3] * o_win[rs]
        o_ref[:, (g * NSA_HPG + h) * NSA_HD:(g * NSA_HPG + h + 1) * NSA_HD] = o.astype(o_ref.dtype)


def _nsa_seq_kernel(q_ref, gate_ref, ckv_ref, ks_ref, kw_ref, o_ref, *, tq, n_cmp, n_blk):
    j = pl.program_id(1)
    q0 = j * tq
    rows = NSA_HPG * tq
    tk = tq
    lane = lax.broadcasted_iota(jnp.int32, (1, tk), 1)
    qpos_t = q0 + lax.broadcasted_iota(jnp.int32, (tq, 1), 0)
    for g in range(NSA_GROUPS):
        qs = _stack_heads(q_ref, g)
        qpos, slope = _row_consts(tq, g, q0)
        o_cmp, p_c = _compressed_branch(qs, ckv_ref[0, 0, g], ckv_ref[0, 1, g], qpos, slope, n_cmp)
        imp = p_c[0:tq]
        for h in range(1, NSA_HPG):
            imp = imp + p_c[h * tq:(h + 1) * tq]
        sel = _select_blocks(imp, qpos_t, n_blk)
        kcol = slice(g * NSA_HD, (g + 1) * NSA_HD)
        vcol = slice((NSA_GROUPS + g) * NSA_HD, (NSA_GROUPS + g + 1) * NSA_HD)

        def sel_step(kt, carry):
            key0 = kt * tk
            rs = pl.ds(pl.multiple_of(key0, tk), tk)
            dist = qpos - (key0 + lane)
            valid = (dist >= 0) & (_expand_sel(sel, key0, tk) > 0.5)
            return _online_update(carry, qs, ks_ref[rs, kcol], ks_ref[rs, vcol], valid, slope * dist.astype(F32))

        o_sel = _finish(lax.fori_loop(0, j + 1, sel_step, _init_carry(rows)))

        def win_step(kt, carry):
            key0 = kt * tk
            rs = pl.ds(pl.multiple_of(key0, tk), tk)
            dist = qpos - (key0 + lane)
            valid = (dist >= 0) & (dist < NSA_WINDOW)
            return _online_update(carry, qs, kw_ref[rs, kcol], kw_ref[rs, vcol], valid, slope * dist.astype(F32))

        lo = jnp.maximum(j - NSA_WINDOW // tk, 0)
        o_win = _finish(lax.fori_loop(lo, j + 1, win_step, _init_carry(rows)))
        _write_gated(o_ref, gate_ref, g, tq, o_cmp, o_sel, o_win)


def _nsa_seq(proj, ckv, b, t):
    tq = 128
    n_cmp = t // NSA_BLOCK
    n_blk = -(-t // NSA_BLOCK)
    nq = t // tq
    return pl.pallas_call(
        functools.partial(_nsa_seq_kernel, tq=tq, n_cmp=n_cmp, n_blk=n_blk),
        grid=(b, nq),
        in_specs=[pl.BlockSpec((tq, NSA_HEADS * NSA_HD), lambda i, j: (i * nq + j, C_QA // 1024)),
                  pl.BlockSpec((tq, LANES), lambda i, j: (i * nq + j, C_GN // LANES)),
                  pl.BlockSpec((1, 2, NSA_GROUPS, n_cmp, NSA_HD), lambda i, j: (i, 0, 0, 0, 0)),
                  pl.BlockSpec((t, KV_COLS), lambda i, j: (i, C_KVS // KV_COLS)),
                  pl.BlockSpec((t, KV_COLS), lambda i, j: (i, C_KVW // KV_COLS))],
        out_specs=pl.BlockSpec((tq, NSA_HEADS * NSA_HD), lambda i, j: (i * nq + j, 0)),
        out_shape=jax.ShapeDtypeStruct((b * t, NSA_HEADS * NSA_HD), BF16),
        compiler_params=_cparams("parallel", "arbitrary"),
        name="nsa_seq",
    )(proj, proj, ckv, proj, proj)


def _pad_rows(x, rows):
    return jnp.concatenate([x, jnp.zeros((rows - x.shape[0], x.shape[1]), x.dtype)], axis=0)


def _nsa_step_kernel(pt_ref, *refs, t, n_pages, page, past_len, n_win, n_cmp, n_blk):
    del pt_ref
    pages = refs[:n_pages]
    q_ref, gate_ref, ckv_ref, ksn_ref, kwn_ref, win_ref, o_ref = refs[n_pages:]
    rows = NSA_HPG * t
    lane = lax.broadcasted_iota(jnp.int32, (1, page), 1)
    qpos_t = past_len + lax.broadcasted_iota(jnp.int32, (t, 1), 0)
    for g in range(NSA_GROUPS):
        qs = _stack_heads(q_ref, g)
        qpos, slope = _row_consts(t, g, past_len)
        o_cmp, p_c = _compressed_branch(qs, ckv_ref[0, 0, g], ckv_ref[0, 1, g], qpos, slope, n_cmp)
        imp = p_c[0:t]
        for h in range(1, NSA_HPG):
            imp = imp + p_c[h * t:(h + 1) * t]
        sel = _select_blocks(imp, qpos_t, n_blk)
        kcol = slice(g * NSA_HD, (g + 1) * NSA_HD)
        vcol = slice((NSA_GROUPS + g) * NSA_HD, (NSA_GROUPS + g + 1) * NSA_HD)

        carry = _init_carry(rows)
        for p in range(n_pages + 1):
            key0 = p * page
            if p < n_pages:
                k, v = pages[p][0, :, kcol], pages[p][0, :, vcol]
            else:
                k, v = _pad_rows(ksn_ref[:, kcol], page), _pad_rows(ksn_ref[:, vcol], page)
            dist = qpos - (key0 + lane)
            valid = (dist >= 0) & (_expand_sel(sel, key0, page) > 0.5)
            carry = _online_update(carry, qs, k, v, valid, slope * dist.astype(F32))
        o_sel = _finish(carry)

        carry = _init_carry(rows)
        for p in range(n_win // page + 1):
            key0 = past_len - n_win + p * page
            if p < n_win // page:
                k, v = win_ref[0, p * page:(p + 1) * page, kcol], win_ref[0, p * page:(p + 1) * page, vcol]
            else:
                k, v = _pad_rows(kwn_ref[:, kcol], page), _pad_rows(kwn_ref[:, vcol], page)
            dist = qpos - (key0 + lane)
            valid = (dist >= 0) & (dist < NSA_WINDOW) & (key0 + lane >= 0)
            carry = _online_update(carry, qs, k, v, valid, slope * dist.astype(F32))
        o_win = _finish(carry)
        _write_gated(o_ref, gate_ref, g, t, o_cmp, o_sel, o_win)


def _nsa_step(proj, ckv, cache_sel, page_table_flat, state_win, b, t, n_pages):
    page = cache_sel.shape[1]
    past_len = n_pages * page
    n_win = state_win.shape[1]
    assert n_win % page == 0 and t <= page
    n_cmp = (past_len + t) // NSA_BLOCK
    n_blk = -(-(past_len + t) // NSA_BLOCK)
    assert n_cmp * NSA_BLOCK == past_len
    page_spec = lambda p: pl.BlockSpec((1, page, KV_COLS), lambda i, pt: (pt[i * n_pages + p], 0, 0))
    grid_spec = pltpu.PrefetchScalarGridSpec(
        num_scalar_prefetch=1,
        grid=(b,),
        in_specs=[page_spec(p) for p in range(n_pages)] + [
            pl.BlockSpec((t, NSA_HEADS * NSA_HD), lambda i, pt: (i, C_QA // 1024)),
            pl.BlockSpec((t, LANES), lambda i, pt: (i, C_GN // LANES)),
            pl.BlockSpec((1, 2, NSA_GROUPS, n_cmp, NSA_HD), lambda i, pt: (i, 0, 0, 0, 0)),
            pl.BlockSpec((t, KV_COLS), lambda i, pt: (i, C_KVS // KV_COLS)),
            pl.BlockSpec((t, KV_COLS), lambda i, pt: (i, C_KVW // KV_COLS)),
            pl.BlockSpec((1, n_win, KV_COLS), lambda i, pt: (i, 0, 0))],
        out_specs=pl.BlockSpec((t, NSA_HEADS * NSA_HD), lambda i, pt: (i, 0)),
    )
    return pl.pallas_call(
        functools.partial(_nsa_step_kernel, t=t, n_pages=n_pages, page=page, past_len=past_len,
                          n_win=n_win, n_cmp=n_cmp, n_blk=n_blk),
        grid_spec=grid_spec,
        out_shape=jax.ShapeDtypeStruct((b * t, NSA_HEADS * NSA_HD), F32),
        compiler_params=_cparams("parallel"),
        name="nsa_step",
    )(page_table_flat, *([cache_sel] * n_pages), proj, proj, ckv, proj, proj, state_win)


HG_ROWS = 128


def _hgrn_kernel(q_ref, f_ref, i_ref, og_ref, lb_ref, nw_ref, s0_ref, o_ref, sfin_ref,
                 st_ref, oacc_ref, *, rows_in, n_tblk, has_state):
    tb = pl.program_id(1)
    n_chunks = -(-rows_in // HG_CHUNK)

    @pl.when(tb == 0)
    def _():
        for h in range(HG_HEADS):
            if has_state:
                st_ref[h] = s0_ref[0, h].T
            else:
                st_ref[h] = jnp.zeros((HG_DV, HG_DK), F32)

    def padded(ref):
        x = ref[...]
        return x if rows_in == HG_ROWS else _pad_rows(x, HG_ROWS)

    q, f, v = padded(q_ref), padded(f_ref), padded(i_ref)
    lb = lb_ref[...]
    row = lax.broadcasted_iota(jnp.int32, (HG_ROWS, 1), 0)
    live = row < rows_in
    forget = lb + (1.0 - lb) * jax.nn.sigmoid(f)
    k = jnp.where(live, (1.0 - lb) * jax.nn.sigmoid(-f), 0.0)
    gl = jnp.where(live, jnp.log(forget), 0.0)
    rc = row % HG_CHUNK
    cum, suf = gl, gl
    s = 1
    while s < HG_CHUNK:
        cum = cum + jnp.where(rc >= s, pltpu.roll(cum, s, axis=0), 0.0)
        suf = suf + jnp.where(rc < HG_CHUNK - s, pltpu.roll(suf, HG_ROWS - s, axis=0), 0.0)
        s *= 2
    qd = q * jnp.exp(cum)
    ki = k * jnp.exp(-cum)
    ke = k * jnp.exp(suf - gl)
    ci = lax.broadcasted_iota(jnp.int32, (HG_ROWS, HG_ROWS), 0)
    cj = lax.broadcasted_iota(jnp.int32, (HG_ROWS, HG_ROWS), 1)
    intra_mask = (ci // HG_CHUNK == cj // HG_CHUNK) & (ci >= cj)
    tok = lax.broadcasted_iota(jnp.int32, (1, HG_ROWS), 1)
    for h in range(HG_HEADS):
        hs = slice(h * HG_DK, (h + 1) * HG_DK)
        qd_h, v_h, ke_h = qd[:, hs], v[:, hs], ke[:, hs]
        att = jnp.where(intra_mask, _mm_nt(qd_h, ki[:, hs]), 0.0)
        oacc_ref[:, hs] = _mm(att, v_h)
        vt = v_h.T
        for c in range(n_chunks):
            cs = slice(c * HG_CHUNK, (c + 1) * HG_CHUNK)
            st = st_ref[h]
            oacc_ref[cs, hs] += _mm_nt(qd_h[cs], st)
            decay = jnp.exp(cum[(c + 1) * HG_CHUNK - 1:(c + 1) * HG_CHUNK, hs])
            in_chunk = (tok >= c * HG_CHUNK) & (tok < (c + 1) * HG_CHUNK)
            st_ref[h] = decay * st + _mm(jnp.where(in_chunk, vt, 0.0), ke_h)
    og = og_ref[...]
    nw = nw_ref[...]
    for h in range(HG_HEADS):
        hs = slice(h * HG_DV, (h + 1) * HG_DV)
        x = oacc_ref[0:rows_in, hs]
        ms = jnp.mean(x * x, axis=-1, keepdims=True)
        y = x * lax.rsqrt(ms + EPS) * nw
        o_ref[:, hs] = (y * jax.nn.silu(og[:, hs])).astype(o_ref.dtype)

    @pl.when(tb == n_tblk - 1)
    def _():
        for h in range(HG_HEADS):
            sfin_ref[0, h] = st_ref[h].T


def _hgrn(proj, lb, norm_w, s0, b, t):
    rows_in = min(t, HG_ROWS)
    n_tblk = t // rows_in
    assert rows_in * n_tblk == t and rows_in % 8 == 0
    has_state = s0 is not None
    if s0 is None:
        s0 = jnp.zeros((1, HG_HEADS, HG_DK, HG_DV), F32)
    width = HG_HEADS * HG_DK
    col = lambda c: pl.BlockSpec((rows_in, width), lambda i, j: (i * n_tblk + j, c // width))
    return pl.pallas_call(
        functools.partial(_hgrn_kernel, rows_in=rows_in, n_tblk=n_tblk, has_state=has_state),
        grid=(b, n_tblk),
        in_specs=[col(C_QB), col(C_FB), col(C_IB), col(C_OG),
                  pl.BlockSpec((1, width), lambda i, j: (0, 0)),
                  pl.BlockSpec((1, HG_DV), lambda i, j: (0, 0)),
                  pl.BlockSpec((1, HG_HEADS, HG_DK, HG_DV),
                               (lambda i, j: (i, 0, 0, 0)) if has_state else (lambda i, j: (0, 0, 0, 0)))],
        out_specs=[pl.BlockSpec((rows_in, width), lambda i, j: (i * n_tblk + j, 0)),
                   pl.BlockSpec((1, HG_HEADS, HG_DK, HG_DV), lambda i, j: (i, 0, 0, 0))],
        out_shape=[jax.ShapeDtypeStruct((b * t, width), BF16),
                   jax.ShapeDtypeStruct((b, HG_HEADS, HG_DK, HG_DV), F32)],
        scratch_shapes=[pltpu.VMEM((HG_HEADS, HG_DV, HG_DK), F32), pltpu.VMEM((HG_ROWS, width), F32)],
        compiler_params=_cparams("parallel", "arbitrary"),
        name="hgrn2",
    )(proj, proj, proj, proj, lb.reshape(1, width), norm_w.reshape(1, HG_DV), s0)


def _decoder_layer(x, past, lb, params):
    (w_in, w_a, w_b, w_o, pe, w1, w2, hg_norm, ln1, ln2, ln3, ln4, w_up, cw, cb, w_dn) = params
    b, t, d = x.shape
    m = b * t
    x2 = x.reshape(m, d)
    tm = min(512, m)
    proj = _norm_matmul(x2, ln1, w_in, tm, 640)
    kv_shape = (b, t, 2, NSA_GROUPS, NSA_HD)
    kvc = proj[:, C_KVC:C_KVC + KV_COLS].reshape(kv_shape)
    kvs = proj[:, C_KVS:C_KVS + KV_COLS].reshape(kv_shape)
    kvw = proj[:, C_KVW:C_KVW + KV_COLS].reshape(kv_shape)
    if past is None:
        ckv = _compress_seq(proj, b, t, pe, w1, w2)
        o_a = _nsa_seq(proj, ckv, b, t)
        new_win = kvw[:, t - min(NSA_WINDOW, t):]
        s0, conv_buf = None, None
    else:
        cache_c, cache_s, page_table_flat, n_pages, win_buf, s0, conv_buf = past
        ckv = _compress_paged(cache_c, page_table_flat, b, n_pages, pe, w1, w2)
        o_a = _nsa_step(proj, ckv, cache_s, page_table_flat, win_buf.reshape(b, -1, KV_COLS), b, t, n_pages)
        new_win = jnp.concatenate([win_buf, kvw], axis=1)[:, t:]
    o_hg, s_fin = _hgrn(proj, lb, hg_norm, s0, b, t)
    mixed = _gated_merge(o_a.astype(BF16), o_hg, w_a, w_b, proj, tm, 512)
    x1 = _matmul_norm_res(mixed, w_o, x2, ln2, tm, 512)
    u = _norm_matmul(x1, ln3, w_up, tm, 512)
    if past is None:
        act = _conv_glu_seq(u, cw, cb, t, tm, 512)
        new_conv = u.reshape(b, t, 2 * D_FF)[:, t - (CONV_W - 1):]
    else:
        act = _conv_glu_step(u.reshape(b, t, 2 * D_FF), conv_buf, cw, cb, min(64, b), 512).reshape(m, D_FF)
        new_conv = jnp.concatenate([conv_buf, u.reshape(b, t, 2 * D_FF)], axis=1)[:, t:]
    y = _matmul_norm_res(act, w_dn, x1, ln4, tm, 512)
    return y.reshape(b, t, d), (kvc, kvs, new_win, s_fin, new_conv)


def _reorder_w_in(w):
    sizes = (NSA_HEADS * NSA_HD, KV_COLS, KV_COLS, KV_COLS, 3 * NSA_HEADS)
    o_kvc = sizes[0]
    o_gn = o_kvc + 3 * KV_COLS
    o_rest = o_gn + sizes[4]
    pad = jnp.zeros((w.shape[0], N_PROJ - C_GN - sizes[4]), w.dtype)
    return jnp.concatenate([w[:, :o_kvc], w[:, o_rest:], w[:, o_kvc:o_gn], w[:, o_gn:o_rest], pad], axis=1)


def kernel(x_prompt, x_sample, cache_cmp_kv, cache_sel_kv, state_win_kv, state_hgrn, state_conv, page_table,
           w_in, w_branch_a, w_branch_b, w_out, cmp_pe, cmp_w1, cmp_w2, hg_lb_raw, hg_norm_w, ln_mix_pre,
           ln_mix_post, ln_ffn_pre, ln_ffn_post, w_up, conv_w, conv_b, w_down):
    depth = w_in.shape[0]
    dec_b, n_pages = page_table.shape
    page = cache_cmp_kv.shape[2]
    lb_all = jnp.cumsum(jax.nn.softmax(hg_lb_raw.astype(F32), axis=0), axis=0)
    pt_flat = page_table.reshape(-1).astype(jnp.int32)
    y_p, y_s = x_prompt, x_sample
    new_p, new_s = [], []
    for l in range(depth):
        params = (_reorder_w_in(w_in[l]).astype(BF16), w_branch_a[l].astype(BF16), w_branch_b[l].astype(BF16),
                  w_out[l].astype(BF16), cmp_pe[l].transpose(1, 0, 2)[:, :, None, :], cmp_w1[l].astype(BF16),
                  cmp_w2[l].astype(BF16), hg_norm_w[l], ln_mix_pre[l], ln_mix_post[l], ln_ffn_pre[l],
                  ln_ffn_post[l], w_up[l].astype(BF16), conv_w[l], conv_b[l], w_down[l].astype(BF16))
        y_p, st_p = _decoder_layer(y_p, None, lb_all[l], params)
        past = (cache_cmp_kv[l].reshape(-1, page, KV_COLS), cache_sel_kv[l].reshape(-1, page, KV_COLS),
                pt_flat, n_pages, state_win_kv[l], state_hgrn[l], state_conv[l])
        y_s, st_s = _decoder_layer(y_s, past, lb_all[l], params)
        new_p.append(st_p)
        new_s.append(st_s)

    def stack(group, i):
        return jnp.stack([st[i] for st in group], axis=0)

    return (y_p, y_s, stack(new_p, 0), stack(new_s, 0), stack(new_p, 1), stack(new_s, 1), stack(new_p, 2),
            stack(new_s, 2), stack(new_p, 3), stack(new_s, 3), stack(new_p, 4), stack(new_s, 4))
```

```python
import functools

import jax
import jax.numpy as jnp
from jax import lax
from jax.experimental import pallas as pl
from jax.experimental.pallas import tpu as pltpu

F32 = jnp.float32
BF16 = jnp.bfloat16

D_MODEL = 2048
NSA_HEADS = 8
NSA_GROUPS = 2
NSA_HPG = NSA_HEADS // NSA_GROUPS
NSA_HD = 128
NSA_BLOCK = 64
NSA_TOPN = 8
NSA_WINDOW = 512
NSA_SCALE = NSA_HD ** -0.5
FORCED_SCORE = NSA_HPG + 1.0
HG_HEADS = 8
HG_DK = 128
HG_DV = 128
HG_CHUNK = 16
D_FF = 5632
CONV_W = 3
EPS = 1e-6
KV_COLS = 2 * NSA_GROUPS * NSA_HD

C_QA = 0
C_QB = 1024
C_FB = 2048
C_IB = 3072
C_OG = 4096
C_GA = 5120
C_GB = 7168
C_KVC = 9216
C_KVS = 9728
C_KVW = 10240
C_GN = 10752
N_PROJ = 10880

LANES = 128
MM_ROWS = 1024
CONV_ROWS = 512
NEG_BIG = -1e30
VMEM_LIMIT = 56 * 1024 * 1024


def _cparams(*sem):
    return pltpu.CompilerParams(dimension_semantics=sem, vmem_limit_bytes=VMEM_LIMIT)


def _mm(a, b):
    return jnp.dot(a.astype(BF16), b.astype(BF16), preferred_element_type=F32)


def _mm_nt(a, b):
    return lax.dot_general(a.astype(BF16), b.astype(BF16), (((1,), (1,)), ((), ())),
                           preferred_element_type=F32)


def _norm_mm_kernel(x_ref, ln_ref, w_ref, o_ref, h_ref):
    @pl.when(pl.program_id(1) == 0)
    def _():
        x = x_ref[...]
        ms = jnp.mean(x * x, axis=-1, keepdims=True)
        h_ref[...] = (x * lax.rsqrt(ms + EPS) * ln_ref[...]).astype(BF16)

    o_ref[...] = _mm(h_ref[...], w_ref[...])


def _norm_matmul(x, ln, w, tm, tn):
    m, k = x.shape
    n = w.shape[1]
    return pl.pallas_call(
        _norm_mm_kernel,
        grid=(m // tm, n // tn),
        in_specs=[pl.BlockSpec((tm, k), lambda i, j: (i, 0)),
                  pl.BlockSpec((1, k), lambda i, j: (0, 0)),
                  pl.BlockSpec((k, tn), lambda i, j: (0, j))],
        out_specs=pl.BlockSpec((tm, tn), lambda i, j: (i, j)),
        out_shape=jax.ShapeDtypeStruct((m, n), F32),
        scratch_shapes=[pltpu.VMEM((tm, k), BF16)],
        compiler_params=_cparams("parallel", "arbitrary"),
        name="norm_matmul",
    )(x, ln.reshape(1, k), w)


def _mm_norm_res_kernel(a_ref, w_ref, res_ref, ln_ref, o_ref, acc_ref, *, nk):
    k = pl.program_id(1)

    @pl.when(k == 0)
    def _():
        acc_ref[...] = jnp.zeros_like(acc_ref)

    acc_ref[...] += _mm(a_ref[...], w_ref[...])

    @pl.when(k == nk - 1)
    def _():
        y = acc_ref[...]
        ms = jnp.mean(y * y, axis=-1, keepdims=True)
        o_ref[...] = res_ref[...] + y * lax.rsqrt(ms + EPS) * ln_ref[...]


def _matmul_norm_res(a, w, res, ln, tm, tk):
    m, kk = a.shape
    n = w.shape[1]
    nk = kk // tk
    return pl.pallas_call(
        functools.partial(_mm_norm_res_kernel, nk=nk),
        grid=(m // tm, nk),
        in_specs=[pl.BlockSpec((tm, tk), lambda i, k: (i, k)),
                  pl.BlockSpec((tk, n), lambda i, k: (k, 0)),
                  pl.BlockSpec((tm, n), lambda i, k: (i, 0)),
                  pl.BlockSpec((1, n), lambda i, k: (0, 0))],
        out_specs=pl.BlockSpec((tm, n), lambda i, k: (i, 0)),
        out_shape=jax.ShapeDtypeStruct((m, n), F32),
        scratch_shapes=[pltpu.VMEM((tm, n), F32)],
        compiler_params=_cparams("parallel", "arbitrary"),
        name="matmul_norm_res",
    )(a, w, res, ln.reshape(1, n))


def _merge_kernel(oa_ref, ohg_ref, wa_ref, wb_ref, ga_ref, gb_ref, o_ref):
    a = _mm(oa_ref[...], wa_ref[...])
    b = _mm(ohg_ref[...], wb_ref[...])
    o_ref[...] = (jax.nn.sigmoid(ga_ref[...]) * a + jax.nn.sigmoid(gb_ref[...]) * b).astype(BF16)


def _gated_merge(o_a, o_hg, w_a, w_b, proj, tm, tn):
    m, k = o_a.shape
    n = w_a.shape[1]
    return pl.pallas_call(
        _merge_kernel,
        grid=(m // tm, n // tn),
        in_specs=[pl.BlockSpec((tm, k), lambda i, j: (i, 0)),
                  pl.BlockSpec((tm, k), lambda i, j: (i, 0)),
                  pl.BlockSpec((k, tn), lambda i, j: (0, j)),
                  pl.BlockSpec((k, tn), lambda i, j: (0, j)),
                  pl.BlockSpec((tm, tn), lambda i, j: (i, C_GA // tn + j)),
                  pl.BlockSpec((tm, tn), lambda i, j: (i, C_GB // tn + j))],
        out_specs=pl.BlockSpec((tm, tn), lambda i, j: (i, j)),
        out_shape=jax.ShapeDtypeStruct((m, n), BF16),
        compiler_params=_cparams("parallel", "arbitrary"),
        name="gated_merge",
    )(o_a, o_hg, w_a, w_b, proj, proj)


def _conv_taps(fa, fg, cwa_ref, cwg_ref, cba_ref, cbg_ref, tap):
    ca = cba_ref[...]
    cg = cbg_ref[...]
    for j in range(CONV_W):
        ca = ca + tap(fa, j) * cwa_ref[j:j + 1, :]
        cg = cg + tap(fg, j) * cwg_ref[j:j + 1, :]
    return jax.nn.gelu(cg, approximate=True) * ca


def _conv_glu_seq_kernel(ua_ref, ug_ref, ha_ref, hg_ref, cwa_ref, cwg_ref, cba_ref, cbg_ref,
                         o_ref, fa_ref, fg_ref, *, tm, tiles_per_seq):
    first = (pl.program_id(0) % tiles_per_seq) == 0
    fa_ref[0:8, :] = jnp.where(first, 0.0, ha_ref[...])
    fg_ref[0:8, :] = jnp.where(first, 0.0, hg_ref[...])
    fa_ref[8:, :] = ua_ref[...]
    fg_ref[8:, :] = ug_ref[...]
    tap = lambda f, j: f[8 - (CONV_W - 1) + j:8 - (CONV_W - 1) + j + tm, :]
    o_ref[...] = _conv_taps(fa_ref, fg_ref, cwa_ref, cwg_ref, cba_ref, cbg_ref, tap).astype(o_ref.dtype)


def _conv_glu_seq(u, conv_w, conv_b, seq_len, tm, tn):
    m = u.shape[0]
    nj = D_FF // tn
    hb = tm // 8
    halo = lambda i, j, off: (jnp.maximum(i * hb - 1, 0), j + off)
    return pl.pallas_call(
        functools.partial(_conv_glu_seq_kernel, tm=tm, tiles_per_seq=seq_len // tm),
        grid=(m // tm, nj),
        in_specs=[pl.BlockSpec((tm, tn), lambda i, j: (i, j)),
                  pl.BlockSpec((tm, tn), lambda i, j: (i, j + nj)),
                  pl.BlockSpec((8, tn), lambda i, j: halo(i, j, 0)),
                  pl.BlockSpec((8, tn), lambda i, j: halo(i, j, nj)),
                  pl.BlockSpec((CONV_W, tn), lambda i, j: (0, j)),
                  pl.BlockSpec((CONV_W, tn), lambda i, j: (0, j + nj)),
                  pl.BlockSpec((1, tn), lambda i, j: (0, j)),
                  pl.BlockSpec((1, tn), lambda i, j: (0, j + nj))],
        out_specs=pl.BlockSpec((tm, tn), lambda i, j: (i, j)),
        out_shape=jax.ShapeDtypeStruct((m, D_FF), BF16),
        scratch_shapes=[pltpu.VMEM((tm + 8, tn), F32), pltpu.VMEM((tm + 8, tn), F32)],
        compiler_params=_cparams("parallel", "arbitrary"),
        name="conv_glu_seq",
    )(u, u, u, u, conv_w, conv_w, conv_b.reshape(1, -1), conv_b.reshape(1, -1))


def _conv_glu_step_kernel(ua_ref, ug_ref, ba_ref, bg_ref, cwa_ref, cwg_ref, cba_ref, cbg_ref,
                          o_ref, fa_ref, fg_ref, *, t):
    fa_ref[:, 8 - (CONV_W - 1):8, :] = ba_ref[...]
    fg_ref[:, 8 - (CONV_W - 1):8, :] = bg_ref[...]
    fa_ref[:, 8:, :] = ua_ref[...]
    fg_ref[:, 8:, :] = ug_ref[...]
    tap = lambda f, j: f[:, 8 - (CONV_W - 1) + j:8 - (CONV_W - 1) + j + t, :]
    o_ref[...] = _conv_taps(fa_ref, fg_ref, cwa_ref, cwg_ref, cba_ref, cbg_ref, tap)


def _conv_glu_step(u3, buf, conv_w, conv_b, nb, tn):
    b, t, _ = u3.shape
    nj = D_FF // tn
    return pl.pallas_call(
        functools.partial(_conv_glu_step_kernel, t=t),
        grid=(b // nb, nj),
        in_specs=[pl.BlockSpec((nb, t, tn), lambda i, j: (i, 0, j)),
                  pl.BlockSpec((nb, t, tn), lambda i, j: (i, 0, j + nj)),
                  pl.BlockSpec((nb, CONV_W - 1, tn), lambda i, j: (i, 0, j)),
                  pl.BlockSpec((nb, CONV_W - 1, tn), lambda i, j: (i, 0, j + nj)),
                  pl.BlockSpec((CONV_W, tn), lambda i, j: (0, j)),
                  pl.BlockSpec((CONV_W, tn), lambda i, j: (0, j + nj)),
                  pl.BlockSpec((1, tn), lambda i, j: (0, j)),
                  pl.BlockSpec((1, tn), lambda i, j: (0, j + nj))],
        out_specs=pl.BlockSpec((nb, t, tn), lambda i, j: (i, 0, j)),
        out_shape=jax.ShapeDtypeStruct((b, t, D_FF), F32),
        scratch_shapes=[pltpu.VMEM((nb, 8 + t, tn), F32), pltpu.VMEM((nb, 8 + t, tn), F32)],
        compiler_params=_cparams("parallel", "arbitrary"),
        name="conv_glu_step",
    )(u3, u3, buf, buf, conv_w, conv_w, conv_b.reshape(1, -1), conv_b.reshape(1, -1))


def _compress_body(load_x, pe_ref, w1_ref, w2_ref, o_ref, n_seq, n_cmp):
    for c in range(2):
        acc = jnp.zeros((NSA_GROUPS * n_seq * n_cmp, NSA_HD), F32)
        for l in range(0, NSA_BLOCK, 2):
            halves = [jnp.concatenate([load_x(l + dl, c, g) for g in range(NSA_GROUPS)], axis=0) + pe_ref[c, l + dl]
                      for dl in range(2)]
            acc = acc + _mm(jnp.concatenate(halves, axis=1), w1_ref[c, l // 2])
        out = _mm(jax.nn.silu(acc), w2_ref[c])
        for g in range(NSA_GROUPS):
            for s in range(n_seq):
                r0 = (g * n_seq + s) * n_cmp
                o_ref[s, c, g] = out[r0:r0 + n_cmp]


def _compress_seq_kernel(x0_ref, x1_ref, x2_ref, x3_ref, pe_ref, w1_ref, w2_ref, o_ref, *, n_cmp):
    xs = (x0_ref, x1_ref, x2_ref, x3_ref)
    load_x = lambda l, c, g: xs[c * NSA_GROUPS + g][pl.ds(l, n_cmp, stride=NSA_BLOCK), :]
    _compress_body(load_x, pe_ref, w1_ref, w2_ref, o_ref, 1, n_cmp)


def _compress_seq(proj, b, t, pe, w1, w2):
    n_cmp = t // NSA_BLOCK
    rows = n_cmp * NSA_BLOCK
    assert rows == t
    return pl.pallas_call(
        functools.partial(_compress_seq_kernel, n_cmp=n_cmp),
        grid=(b,),
        in_specs=[pl.BlockSpec((rows, NSA_HD), lambda i, cg=cg: (i, C_KVC // NSA_HD + cg))
                  for cg in range(2 * NSA_GROUPS)] + [
                  pl.BlockSpec(pe.shape, lambda i: (0, 0, 0, 0)),
                  pl.BlockSpec(w1.shape, lambda i: (0, 0, 0, 0)),
                  pl.BlockSpec(w2.shape, lambda i: (0, 0, 0))],
        out_specs=pl.BlockSpec((1, 2, NSA_GROUPS, n_cmp, NSA_HD), lambda i: (i, 0, 0, 0, 0)),
        out_shape=jax.ShapeDtypeStruct((b, 2, NSA_GROUPS, n_cmp, NSA_HD), F32),
        compiler_params=_cparams("parallel"),
        name="compress_seq",
    )(proj, proj, proj, proj, pe, w1, w2)


KV_SLOTS = 2 * NSA_GROUPS
CMP_SEQS = 2
CMP_PITCH = NSA_BLOCK * KV_SLOTS + 8


def _compress_paged_kernel(pt_ref, *refs, n_seq, n_pages, page_rows):
    del pt_ref
    pages = refs[:n_seq * n_pages]
    pe_ref, w1_ref, w2_ref, o_ref, x_ref = refs[n_seq * n_pages:]
    blk_rows = NSA_BLOCK * KV_SLOTS
    per_page = page_rows // blk_rows
    for i, pg in enumerate(pages):
        for j in range(per_page):
            n = i * per_page + j
            x_ref[n * CMP_PITCH:n * CMP_PITCH + blk_rows, :] = pg[j * blk_rows:(j + 1) * blk_rows, :]
    n_cmp = n_pages * per_page
    load_x = lambda l, c, g: x_ref[pl.ds(l * KV_SLOTS + c * NSA_GROUPS + g, n_seq * n_cmp, stride=CMP_PITCH), :]
    _compress_body(load_x, pe_ref, w1_ref, w2_ref, o_ref, n_seq, n_cmp)


def _compress_paged(cache2d, page_table_flat, b, n_pages, page, pe, w1, w2):
    page_rows = page * KV_SLOTS
    n_cmp = n_pages * page // NSA_BLOCK
    n_seq = CMP_SEQS if b % CMP_SEQS == 0 else 1
    page_spec = lambda s, p: pl.BlockSpec(
        (page_rows, NSA_HD), lambda i, pt: (pt[(i * n_seq + s) * n_pages + p], 0))
    grid_spec = pltpu.PrefetchScalarGridSpec(
        num_scalar_prefetch=1,
        grid=(b // n_seq,),
        in_specs=[page_spec(s, p) for s in range(n_seq) for p in range(n_pages)] + [
            pl.BlockSpec(pe.shape, lambda i, pt: (0, 0, 0, 0)),
            pl.BlockSpec(w1.shape, lambda i, pt: (0, 0, 0, 0)),
            pl.BlockSpec(w2.shape, lambda i, pt: (0, 0, 0))],
        out_specs=pl.BlockSpec((n_seq, 2, NSA_GROUPS, n_cmp, NSA_HD), lambda i, pt: (i, 0, 0, 0, 0)),
        scratch_shapes=[pltpu.VMEM((n_seq * n_cmp * CMP_PITCH, NSA_HD), F32)],
    )
    return pl.pallas_call(
        functools.partial(_compress_paged_kernel, n_seq=n_seq, n_pages=n_pages, page_rows=page_rows),
        grid_spec=grid_spec,
        out_shape=jax.ShapeDtypeStruct((b, 2, NSA_GROUPS, n_cmp, NSA_HD), F32),
        compiler_params=_cparams("parallel"),
        name="compress_paged",
    )(page_table_flat, *([cache2d] * (n_seq * n_pages)), pe, w1, w2)


def _row_consts(tq, g, q_pos0):
    rows = NSA_HPG * tq
    r = lax.broadcasted_iota(jnp.int32, (rows, 1), 0)
    head = r // tq
    qpos = q_pos0 + (r - head * tq)
    slope = jnp.zeros((rows, 1), F32)
    for h in range(NSA_HPG):
        slope = jnp.where(head == h, 2.0 ** (-8.0 * (g * NSA_HPG + h + 1.0) / NSA_HEADS), slope)
    return qpos, slope


def _stack_heads(q_ref, g):
    return jnp.concatenate(
        [q_ref[:, (g * NSA_HPG + h) * NSA_HD:(g * NSA_HPG + h + 1) * NSA_HD] for h in range(NSA_HPG)], axis=0)


def _compressed_branch(qs, ck, cv, qpos, slope, n_cmp):
    pad = jnp.zeros((LANES - n_cmp, NSA_HD), F32)
    ckp = jnp.concatenate([ck, pad], axis=0)
    cvp = jnp.concatenate([cv, pad], axis=0)
    n = lax.broadcasted_iota(jnp.int32, (1, LANES), 1)
    dist = qpos - ((n + 1) * NSA_BLOCK - 1)
    s = _mm_nt(qs, ckp) * NSA_SCALE - slope * dist.astype(F32)
    mask = (dist >= 0) & (n < n_cmp)
    s = jnp.where(mask, s, NEG_BIG)
    m = jnp.max(s, axis=-1, keepdims=True)
    e = jnp.where(mask, jnp.exp(s - m), 0.0)
    d = jnp.sum(e, axis=-1, keepdims=True)
    p = e / jnp.where(d > 0, d, 1.0)
    return _mm(p, cvp), p


def _select_blocks(imp, qpos_t, n_blk):
    blk = lax.broadcasted_iota(jnp.int32, imp.shape, 1)
    cur = qpos_t // NSA_BLOCK
    forced = (blk == 0) | (blk == cur) | (blk == cur - 1)
    valid = blk <= cur
    score = jnp.where(valid, jnp.where(forced, FORCED_SCORE, imp), -1.0)
    score = jnp.where(blk < n_blk, score, -2.0)
    blk_f = blk.astype(F32)
    sel = jnp.zeros(imp.shape, F32)
    for _ in range(min(NSA_TOPN, n_blk)):
        mx = jnp.max(score, axis=-1, keepdims=True)
        first = jnp.min(jnp.where(score == mx, blk_f, 1e9), axis=-1, keepdims=True)
        hit = blk_f == first
        sel = jnp.where(hit, 1.0, sel)
        score = jnp.where(hit, -3.0, score)
    return sel


def _expand_sel(sel, key0, nkeys):
    bi = lax.broadcasted_iota(jnp.int32, (LANES, nkeys), 0)
    ki = lax.broadcasted_iota(jnp.int32, (LANES, nkeys), 1)
    expand = (bi == (key0 + ki) // NSA_BLOCK).astype(BF16)
    m = _mm(sel, expand)
    return jnp.concatenate([m] * NSA_HPG, axis=0)


def _write_gated(o_ref, gate_ref, g, tq, o_cmp, o_sel, o_win):
    sig = jax.nn.sigmoid(gate_ref[...])
    for h in range(NSA_HPG):
        c0 = (g * NSA_HPG + h) * 3
        rs = slice(h * tq, (h + 1) * tq)
        o = sig[:, c0:c0 + 1] * o_cmp[rs] + sig[:, c0 + 1:c0 + 2] * o_sel[rs] + sig[:, c0 + 2:c0 + 3] * o_win[rs]
        o_ref[:, (g * NSA_HPG + h) * NSA_HD:(g * NSA_HPG + h + 1) * NSA_HD] = o.astype(o_ref.dtype)


SEQ_TK = 256


def _lane_groups(x):
    return [x[:, i:i + LANES] for i in range(0, x.shape[1], LANES)]


def _tiled_attention(qs, kv_ref, kcol, vcol, lo, hi, s_ref, score_mask):
    rows = qs.shape[0]

    def scores(kt, mrun):
        key0 = kt * SEQ_TK
        valid, bias = score_mask(key0)
        k = kv_ref[pl.ds(pl.multiple_of(key0, SEQ_TK), SEQ_TK), kcol]
        s = jnp.where(valid, _mm_nt(qs, k) * NSA_SCALE - bias, NEG_BIG)
        s_ref[kt] = s
        return functools.reduce(jnp.maximum, [mrun] + _lane_groups(s))

    mrun = lax.fori_loop(lo, hi, scores, jnp.full((rows, LANES), NEG_BIG, F32))
    m = jnp.maximum(jnp.max(mrun, axis=-1, keepdims=True), 0.1 * NEG_BIG)

    def values(kt, carry):
        lsum, acc = carry
        p = jnp.exp(s_ref[kt] - m)
        v = kv_ref[pl.ds(pl.multiple_of(kt * SEQ_TK, SEQ_TK), SEQ_TK), vcol]
        return sum(_lane_groups(p), lsum), acc + _mm(p, v)

    lsum, acc = lax.fori_loop(lo, hi, values, (jnp.zeros((rows, LANES), F32), jnp.zeros((rows, NSA_HD), F32)))
    l = jnp.sum(lsum, axis=-1, keepdims=True)
    return acc / jnp.where(l > 0, l, 1.0)


def _nsa_seq_kernel(q_ref, gate_ref, ckv_ref, ks_ref, kw_ref, o_ref, s_ref, *, tq, n_cmp, n_blk):
    j = pl.program_id(1)
    q0 = j * tq
    lane = lax.broadcasted_iota(jnp.int32, (1, SEQ_TK), 1)
    qpos_t = q0 + lax.broadcasted_iota(jnp.int32, (tq, 1), 0)
    hi = (q0 + tq - 1) // SEQ_TK + 1
    for g in range(NSA_GROUPS):
        qs = _stack_heads(q_ref, g)
        qpos, slope = _row_consts(tq, g, q0)
        o_cmp, p_c = _compressed_branch(qs, ckv_ref[0, 0, g], ckv_ref[0, 1, g], qpos, slope, n_cmp)
        imp = p_c[0:tq]
        for h in range(1, NSA_HPG):
            imp = imp + p_c[h * tq:(h + 1) * tq]
        sel = _select_blocks(imp, qpos_t, n_blk)
        kcol = slice(g * NSA_HD, (g + 1) * NSA_HD)
        vcol = slice((NSA_GROUPS + g) * NSA_HD, (NSA_GROUPS + g + 1) * NSA_HD)

        def sel_mask(key0):
            dist = qpos - (key0 + lane)
            return (dist >= 0) & (_expand_sel(sel, key0, SEQ_TK) > 0.5), slope * dist.astype(F32)

        o_sel = _tiled_attention(qs, ks_ref, kcol, vcol, 0, hi, s_ref, sel_mask)

        def win_mask(key0):
            dist = qpos - (key0 + lane)
            return (dist >= 0) & (dist < NSA_WINDOW), slope * dist.astype(F32)

        lo = jnp.maximum(q0 - (NSA_WINDOW - 1), 0) // SEQ_TK
        o_win = _tiled_attention(qs, kw_ref, kcol, vcol, lo, hi, s_ref, win_mask)
        _write_gated(o_ref, gate_ref, g, tq, o_cmp, o_sel, o_win)


def _nsa_seq(proj, ckv, b, t):
    tq = 128
    n_cmp = t // NSA_BLOCK
    n_blk = -(-t // NSA_BLOCK)
    nq = t // tq
    assert t % SEQ_TK == 0 and n_blk <= LANES
    return pl.pallas_call(
        functools.partial(_nsa_seq_kernel, tq=tq, n_cmp=n_cmp, n_blk=n_blk),
        scratch_shapes=[pltpu.VMEM((t // SEQ_TK, NSA_HPG * tq, SEQ_TK), F32)],
        grid=(b, nq),
        in_specs=[pl.BlockSpec((tq, NSA_HEADS * NSA_HD), lambda i, j: (i * nq + j, C_QA // 1024)),
                  pl.BlockSpec((tq, LANES), lambda i, j: (i * nq + j, C_GN // LANES)),
                  pl.BlockSpec((1, 2, NSA_GROUPS, n_cmp, NSA_HD), lambda i, j: (i, 0, 0, 0, 0)),
                  pl.BlockSpec((t, KV_COLS), lambda i, j: (i, C_KVS // KV_COLS)),
                  pl.BlockSpec((t, KV_COLS), lambda i, j: (i, C_KVW // KV_COLS))],
        out_specs=pl.BlockSpec((tq, NSA_HEADS * NSA_HD), lambda i, j: (i * nq + j, 0)),
        out_shape=jax.ShapeDtypeStruct((b * t, NSA_HEADS * NSA_HD), BF16),
        compiler_params=_cparams("parallel", "arbitrary"),
        name="nsa_seq",
    )(proj, proj, ckv, proj, proj)


def _pad_rows(x, rows):
    return jnp.concatenate([x, jnp.zeros((rows - x.shape[0], x.shape[1]), x.dtype)], axis=0)


def _two_pass_attention(qs, tiles):
    scores = []
    for k, _, valid, bias in tiles:
        scores.append(jnp.where(valid, _mm_nt(qs, k) * NSA_SCALE - bias, NEG_BIG))
    m = jnp.max(functools.reduce(jnp.maximum, scores), axis=-1, keepdims=True)
    acc = jnp.zeros((qs.shape[0], NSA_HD), F32)
    lsum = jnp.zeros(scores[0].shape, F32)
    for s, (_, v, _, _) in zip(scores, tiles):
        p = jnp.where(s > 0.5 * NEG_BIG, jnp.exp(s - m), 0.0)
        lsum = lsum + p
        acc = acc + _mm(p, v)
    l = jnp.sum(lsum, axis=-1, keepdims=True)
    return acc / jnp.where(l > 0, l, 1.0)


def _nsa_step_kernel(pt_ref, *refs, t, n_pages, page, past_len, n_win, n_cmp, n_blk):
    del pt_ref
    pages = refs[:n_pages]
    q_ref, gate_ref, ckv_ref, ksn_ref, kwn_ref, win_ref, o_ref = refs[n_pages:]
    lane = lax.broadcasted_iota(jnp.int32, (1, page), 1)
    qpos_t = past_len + lax.broadcasted_iota(jnp.int32, (t, 1), 0)
    blocks_per_page = page // NSA_BLOCK
    for g in range(NSA_GROUPS):
        qs = _stack_heads(q_ref, g)
        qpos, slope = _row_consts(t, g, past_len)
        o_cmp, p_c = _compressed_branch(qs, ckv_ref[0, 0, g], ckv_ref[0, 1, g], qpos, slope, n_cmp)
        imp = p_c[0:t]
        for h in range(1, NSA_HPG):
            imp = imp + p_c[h * t:(h + 1) * t]
        sel = _select_blocks(imp, qpos_t, n_blk)
        kcol = slice(g * NSA_HD, (g + 1) * NSA_HD)
        vcol = slice((NSA_GROUPS + g) * NSA_HD, (NSA_GROUPS + g + 1) * NSA_HD)
        kslot, vslot = g, NSA_GROUPS + g

        tiles = []
        for p in range(n_pages + 1):
            key0 = p * page
            if p < n_pages:
                k = pages[p][pl.ds(kslot, page, stride=KV_SLOTS), :]
                v = pages[p][pl.ds(vslot, page, stride=KV_SLOTS), :]
            else:
                k, v = _pad_rows(ksn_ref[:, kcol], page), _pad_rows(ksn_ref[:, vcol], page)
            chosen = jnp.zeros((t, page), F32)
            for bi in range(blocks_per_page):
                blk = p * blocks_per_page + bi
                chosen = jnp.where(lane // NSA_BLOCK == bi, sel[:, blk:blk + 1], chosen)
            chosen = jnp.concatenate([chosen] * NSA_HPG, axis=0)
            dist = qpos - (key0 + lane)
            tiles.append((k, v, (dist >= 0) & (chosen > 0.5), slope * dist.astype(F32)))
        o_sel = _two_pass_attention(qs, tiles)

        tiles = []
        for p in range(n_win // page + 1):
            key0 = past_len - n_win + p * page
            if p < n_win // page:
                k = win_ref[pl.ds(p * page * KV_SLOTS + kslot, page, stride=KV_SLOTS), :]
                v = win_ref[pl.ds(p * page * KV_SLOTS + vslot, page, stride=KV_SLOTS), :]
            else:
                k, v = _pad_rows(kwn_ref[:, kcol], page), _pad_rows(kwn_ref[:, vcol], page)
            dist = qpos - (key0 + lane)
            valid = (dist >= 0) & (dist < NSA_WINDOW) & (key0 + lane >= 0)
            tiles.append((k, v, valid, slope * dist.astype(F32)))
        o_win = _two_pass_attention(qs, tiles)
        _write_gated(o_ref, gate_ref, g, t, o_cmp, o_sel, o_win)


def _nsa_step(proj, ckv, cache2d, page_table_flat, win2d, b, t, n_pages, page, n_win):
    past_len = n_pages * page
    assert n_win % page == 0 and t <= page
    n_cmp = (past_len + t) // NSA_BLOCK
    n_blk = -(-(past_len + t) // NSA_BLOCK)
    assert n_cmp * NSA_BLOCK == past_len and n_blk <= LANES
    page_spec = lambda p: pl.BlockSpec((page * KV_SLOTS, NSA_HD), lambda i, pt: (pt[i * n_pages + p], 0))
    grid_spec = pltpu.PrefetchScalarGridSpec(
        num_scalar_prefetch=1,
        grid=(b,),
        in_specs=[page_spec(p) for p in range(n_pages)] + [
            pl.BlockSpec((t, NSA_HEADS * NSA_HD), lambda i, pt: (i, C_QA // 1024)),
            pl.BlockSpec((t, LANES), lambda i, pt: (i, C_GN // LANES)),
            pl.BlockSpec((1, 2, NSA_GROUPS, n_cmp, NSA_HD), lambda i, pt: (i, 0, 0, 0, 0)),
            pl.BlockSpec((t, KV_COLS), lambda i, pt: (i, C_KVS // KV_COLS)),
            pl.BlockSpec((t, KV_COLS), lambda i, pt: (i, C_KVW // KV_COLS)),
            pl.BlockSpec((n_win * KV_SLOTS, NSA_HD), lambda i, pt: (i, 0))],
        out_specs=pl.BlockSpec((t, NSA_HEADS * NSA_HD), lambda i, pt: (i, 0)),
    )
    return pl.pallas_call(
        functools.partial(_nsa_step_kernel, t=t, n_pages=n_pages, page=page, past_len=past_len,
                          n_win=n_win, n_cmp=n_cmp, n_blk=n_blk),
        grid_spec=grid_spec,
        out_shape=jax.ShapeDtypeStruct((b * t, NSA_HEADS * NSA_HD), F32),
        compiler_params=_cparams("parallel"),
        name="nsa_step",
    )(page_table_flat, *([cache2d] * n_pages), proj, proj, ckv, proj, proj, win2d)


HG_ROWS = 128


def _hgrn_kernel(q_ref, f_ref, i_ref, og_ref, lb_ref, nw_ref, s0_ref, o_ref, sfin_ref,
                 st_ref, oacc_ref, *, rows_in, n_tblk, has_state):
    tb = pl.program_id(1)
    n_chunks = -(-rows_in // HG_CHUNK)

    @pl.when(tb == 0)
    def _():
        for h in range(HG_HEADS):
            if has_state:
                st_ref[h] = s0_ref[0, h].T
            else:
                st_ref[h] = jnp.zeros((HG_DV, HG_DK), F32)

    def padded(ref):
        x = ref[...]
        return x if rows_in == HG_ROWS else _pad_rows(x, HG_ROWS)

    q, f, v = padded(q_ref), padded(f_ref), padded(i_ref)
    lb = lb_ref[...]
    row = lax.broadcasted_iota(jnp.int32, (HG_ROWS, 1), 0)
    live = row < rows_in
    forget = lb + (1.0 - lb) * jax.nn.sigmoid(f)
    k = jnp.where(live, (1.0 - lb) * jax.nn.sigmoid(-f), 0.0)
    gl = jnp.where(live, jnp.log(forget), 0.0)
    rc = row % HG_CHUNK
    cum, suf = gl, gl
    s = 1
    while s < HG_CHUNK:
        cum = cum + jnp.where(rc >= s, pltpu.roll(cum, s, axis=0), 0.0)
        suf = suf + jnp.where(rc < HG_CHUNK - s, pltpu.roll(suf, HG_ROWS - s, axis=0), 0.0)
        s *= 2
    qd = q * jnp.exp(cum)
    ki = k * jnp.exp(-cum)
    ke = k * jnp.exp(suf - gl)
    ci = lax.broadcasted_iota(jnp.int32, (HG_ROWS, HG_ROWS), 0)
    cj = lax.broadcasted_iota(jnp.int32, (HG_ROWS, HG_ROWS), 1)
    intra_mask = (ci // HG_CHUNK == cj // HG_CHUNK) & (ci >= cj)
    tok = lax.broadcasted_iota(jnp.int32, (1, HG_ROWS), 1)
    for h in range(HG_HEADS):
        hs = slice(h * HG_DK, (h + 1) * HG_DK)
        qd_h, v_h, ke_h = qd[:, hs], v[:, hs], ke[:, hs]
        att = jnp.where(intra_mask, _mm_nt(qd_h, ki[:, hs]), 0.0)
        oacc_ref[:, hs] = _mm(att, v_h)
        vt = v_h.T
        for c in range(n_chunks):
            cs = slice(c * HG_CHUNK, (c + 1) * HG_CHUNK)
            st = st_ref[h]
            oacc_ref[cs, hs] += _mm_nt(qd_h[cs], st)
            decay = jnp.exp(cum[(c + 1) * HG_CHUNK - 1:(c + 1) * HG_CHUNK, hs])
            in_chunk = (tok >= c * HG_CHUNK) & (tok < (c + 1) * HG_CHUNK)
            st_ref[h] = decay * st + _mm(jnp.where(in_chunk, vt, 0.0), ke_h)
    og = og_ref[...]
    nw = nw_ref[...]
    for h in range(HG_HEADS):
        hs = slice(h * HG_DV, (h + 1) * HG_DV)
        x = oacc_ref[0:rows_in, hs]
        ms = jnp.mean(x * x, axis=-1, keepdims=True)
        y = x * lax.rsqrt(ms + EPS) * nw
        o_ref[:, hs] = (y * jax.nn.silu(og[:, hs])).astype(o_ref.dtype)

    @pl.when(tb == n_tblk - 1)
    def _():
        for h in range(HG_HEADS):
            sfin_ref[0, h] = st_ref[h].T


def _hgrn(proj, lb, norm_w, s0, b, t):
    rows_in = min(t, HG_ROWS)
    n_tblk = t // rows_in
    assert rows_in * n_tblk == t and rows_in % 8 == 0
    has_state = s0 is not None
    if s0 is None:
        s0 = jnp.zeros((1, HG_HEADS, HG_DK, HG_DV), F32)
    width = HG_HEADS * HG_DK
    col = lambda c: pl.BlockSpec((rows_in, width), lambda i, j: (i * n_tblk + j, c // width))
    return pl.pallas_call(
        functools.partial(_hgrn_kernel, rows_in=rows_in, n_tblk=n_tblk, has_state=has_state),
        grid=(b, n_tblk),
        in_specs=[col(C_QB), col(C_FB), col(C_IB), col(C_OG),
                  pl.BlockSpec((1, width), lambda i, j: (0, 0)),
                  pl.BlockSpec((1, HG_DV), lambda i, j: (0, 0)),
                  pl.BlockSpec((1, HG_HEADS, HG_DK, HG_DV),
                               (lambda i, j: (i, 0, 0, 0)) if has_state else (lambda i, j: (0, 0, 0, 0)))],
        out_specs=[pl.BlockSpec((rows_in, width), lambda i, j: (i * n_tblk + j, 0)),
                   pl.BlockSpec((1, HG_HEADS, HG_DK, HG_DV), lambda i, j: (i, 0, 0, 0))],
        out_shape=[jax.ShapeDtypeStruct((b * t, width), BF16),
                   jax.ShapeDtypeStruct((b, HG_HEADS, HG_DK, HG_DV), F32)],
        scratch_shapes=[pltpu.VMEM((HG_HEADS, HG_DV, HG_DK), F32), pltpu.VMEM((HG_ROWS, width), F32)],
        compiler_params=_cparams("parallel", "arbitrary"),
        name="hgrn2",
    )(proj, proj, proj, proj, lb.reshape(1, width), norm_w.reshape(1, HG_DV), s0)


def _decoder_layer(x, past, lb, params):
    (w_in, w_a, w_b, w_o, pe, w1, w2, hg_norm, ln1, ln2, ln3, ln4, w_up, cw, cb, w_dn) = params
    b, t, d = x.shape
    m = b * t
    x2 = x.reshape(m, d)
    tm = min(MM_ROWS, m)
    proj = _norm_matmul(x2, ln1, w_in, tm, 640)
    kv_shape = (b, t, 2, NSA_GROUPS, NSA_HD)
    kvc = proj[:, C_KVC:C_KVC + KV_COLS].reshape(kv_shape)
    kvs = proj[:, C_KVS:C_KVS + KV_COLS].reshape(kv_shape)
    kvw = proj[:, C_KVW:C_KVW + KV_COLS].reshape(kv_shape)
    if past is None:
        ckv = _compress_seq(proj, b, t, pe, w1, w2)
        o_a = _nsa_seq(proj, ckv, b, t)
        new_win = kvw[:, t - min(NSA_WINDOW, t):]
        s0, conv_buf = None, None
    else:
        cache_c, cache_s, page_table_flat, n_pages, page, win_buf, s0, conv_buf = past
        n_win = win_buf.shape[1]
        ckv = _compress_paged(cache_c, page_table_flat, b, n_pages, page, pe, w1, w2)
        o_a = _nsa_step(proj, ckv, cache_s, page_table_flat, win_buf.reshape(-1, NSA_HD), b, t, n_pages, page, n_win)
        new_win = jnp.concatenate([win_buf, kvw], axis=1)[:, t:]
    o_hg, s_fin = _hgrn(proj, lb, hg_norm, s0, b, t)
    mixed = _gated_merge(o_a.astype(BF16), o_hg, w_a, w_b, proj, tm, 512)
    x1 = _matmul_norm_res(mixed, w_o, x2, ln2, tm, 512)
    u = _norm_matmul(x1, ln3, w_up, tm, 512)
    if past is None:
        act = _conv_glu_seq(u, cw, cb, t, min(CONV_ROWS, t), 512)
        new_conv = u.reshape(b, t, 2 * D_FF)[:, t - (CONV_W - 1):]
    else:
        act = _conv_glu_step(u.reshape(b, t, 2 * D_FF), conv_buf, cw, cb, min(64, b), 512).reshape(m, D_FF)
        new_conv = jnp.concatenate([conv_buf, u.reshape(b, t, 2 * D_FF)], axis=1)[:, t:]
    y = _matmul_norm_res(act, w_dn, x1, ln4, tm, 512)
    return y.reshape(b, t, d), (kvc, kvs, new_win, s_fin, new_conv)


def _reorder_w_in(w):
    sizes = (NSA_HEADS * NSA_HD, KV_COLS, KV_COLS, KV_COLS, 3 * NSA_HEADS)
    o_kvc = sizes[0]
    o_gn = o_kvc + 3 * KV_COLS
    o_rest = o_gn + sizes[4]
    pad = jnp.zeros((w.shape[0], N_PROJ - C_GN - sizes[4]), w.dtype)
    return jnp.concatenate([w[:, :o_kvc], w[:, o_rest:], w[:, o_kvc:o_gn], w[:, o_gn:o_rest], pad], axis=1)


def kernel(x_prompt, x_sample, cache_cmp_kv, cache_sel_kv, state_win_kv, state_hgrn, state_conv, page_table,
           w_in, w_branch_a, w_branch_b, w_out, cmp_pe, cmp_w1, cmp_w2, hg_lb_raw, hg_norm_w, ln_mix_pre,
           ln_mix_post, ln_ffn_pre, ln_ffn_post, w_up, conv_w, conv_b, w_down):
    depth = w_in.shape[0]
    dec_b, n_pages = page_table.shape
    page = cache_cmp_kv.shape[2]
    lb_all = jnp.cumsum(jax.nn.softmax(hg_lb_raw.astype(F32), axis=0), axis=0)
    pt_flat = page_table.reshape(-1).astype(jnp.int32)
    y_p, y_s = x_prompt, x_sample
    new_p, new_s = [], []
    for l in range(depth):
        params = (_reorder_w_in(w_in[l]).astype(BF16), w_branch_a[l].astype(BF16), w_branch_b[l].astype(BF16),
                  w_out[l].astype(BF16), cmp_pe[l].transpose(1, 0, 2)[:, :, None, :],
                  cmp_w1[l].astype(BF16).transpose(1, 0, 2, 3).reshape(2, NSA_BLOCK // 2, 2 * NSA_HD, NSA_HD),
                  cmp_w2[l].astype(BF16), hg_norm_w[l], ln_mix_pre[l], ln_mix_post[l], ln_ffn_pre[l],
                  ln_ffn_post[l], w_up[l].astype(BF16), conv_w[l], conv_b[l], w_down[l].astype(BF16))
        y_p, st_p = _decoder_layer(y_p, None, lb_all[l], params)
        past = (cache_cmp_kv[l].reshape(-1, NSA_HD), cache_sel_kv[l].reshape(-1, NSA_HD),
                pt_flat, n_pages, page, state_win_kv[l], state_hgrn[l], state_conv[l])
        y_s, st_s = _decoder_layer(y_s, past, lb_all[l], params)
        new_p.append(st_p)
        new_s.append(st_s)

    def stack(group, i):
        return jnp.stack([st[i] for st in group], axis=0)

    return (y_p, y_s, stack(new_p, 0), stack(new_s, 0), stack(new_p, 1), stack(new_s, 1), stack(new_p, 2),
            stack(new_s, 2), stack(new_p, 3), stack(new_s, 3), stack(new_p, 4), stack(new_s, 4))
```

```python
import functools

import jax
import jax.numpy as jnp
from jax import lax
from jax.experimental import pallas as pl
from jax.experimental.pallas import tpu as pltpu

F32 = jnp.float32
BF16 = jnp.bfloat16

D_MODEL = 2048
NSA_HEADS = 8
NSA_GROUPS = 2
NSA_HPG = NSA_HEADS // NSA_GROUPS
NSA_HD = 128
NSA_BLOCK = 64
NSA_TOPN = 8
NSA_WINDOW = 512
NSA_SCALE = NSA_HD ** -0.5
FORCED_SCORE = NSA_HPG + 1.0
HG_HEADS = 8
HG_DK = 128
HG_DV = 128
HG_CHUNK = 16
D_FF = 5632
CONV_W = 3
EPS = 1e-6
KV_COLS = 2 * NSA_GROUPS * NSA_HD

C_QA = 0
C_QB = 1024
C_FB = 2048
C_IB = 3072
C_OG = 4096
C_GA = 5120
C_GB = 7168
C_KVC = 9216
C_KVS = 9728
C_KVW = 10240
C_GN = 10752
N_PROJ = 10880

LANES = 128
MM_ROWS = 1024
CONV_ROWS = 512
NEG_BIG = -1e30
VMEM_LIMIT = 56 * 1024 * 1024


def _cparams(*sem):
    return pltpu.CompilerParams(dimension_semantics=sem, vmem_limit_bytes=VMEM_LIMIT)


def _mm(a, b):
    return jnp.dot(a.astype(BF16), b.astype(BF16), preferred_element_type=F32)


def _mm_nt(a, b):
    return lax.dot_general(a.astype(BF16), b.astype(BF16), (((1,), (1,)), ((), ())),
                           preferred_element_type=F32)


def _norm_mm_kernel(x_ref, ln_ref, w_ref, o_ref, h_ref):
    @pl.when(pl.program_id(1) == 0)
    def _():
        x = x_ref[...]
        ms = jnp.mean(x * x, axis=-1, keepdims=True)
        h_ref[...] = (x * lax.rsqrt(ms + EPS) * ln_ref[...]).astype(BF16)

    o_ref[...] = _mm(h_ref[...], w_ref[...])


def _norm_matmul(x, ln, w, tm, tn):
    m, k = x.shape
    n = w.shape[1]
    return pl.pallas_call(
        _norm_mm_kernel,
        grid=(m // tm, n // tn),
        in_specs=[pl.BlockSpec((tm, k), lambda i, j: (i, 0)),
                  pl.BlockSpec((1, k), lambda i, j: (0, 0)),
                  pl.BlockSpec((k, tn), lambda i, j: (0, j))],
        out_specs=pl.BlockSpec((tm, tn), lambda i, j: (i, j)),
        out_shape=jax.ShapeDtypeStruct((m, n), F32),
        scratch_shapes=[pltpu.VMEM((tm, k), BF16)],
        compiler_params=_cparams("parallel", "arbitrary"),
        name="norm_matmul",
    )(x, ln.reshape(1, k), w)


def _mm_norm_res_kernel(a_ref, w_ref, res_ref, ln_ref, o_ref, acc_ref, *, nk):
    k = pl.program_id(1)

    @pl.when(k == 0)
    def _():
        acc_ref[...] = jnp.zeros_like(acc_ref)

    acc_ref[...] += _mm(a_ref[...], w_ref[...])

    @pl.when(k == nk - 1)
    def _():
        y = acc_ref[...]
        ms = jnp.mean(y * y, axis=-1, keepdims=True)
        o_ref[...] = res_ref[...] + y * lax.rsqrt(ms + EPS) * ln_ref[...]


def _matmul_norm_res(a, w, res, ln, tm, tk):
    m, kk = a.shape
    n = w.shape[1]
    nk = kk // tk
    return pl.pallas_call(
        functools.partial(_mm_norm_res_kernel, nk=nk),
        grid=(m // tm, nk),
        in_specs=[pl.BlockSpec((tm, tk), lambda i, k: (i, k)),
                  pl.BlockSpec((tk, n), lambda i, k: (k, 0)),
                  pl.BlockSpec((tm, n), lambda i, k: (i, 0)),
                  pl.BlockSpec((1, n), lambda i, k: (0, 0))],
        out_specs=pl.BlockSpec((tm, n), lambda i, k: (i, 0)),
        out_shape=jax.ShapeDtypeStruct((m, n), F32),
        scratch_shapes=[pltpu.VMEM((tm, n), F32)],
        compiler_params=_cparams("parallel", "arbitrary"),
        name="matmul_norm_res",
    )(a, w, res, ln.reshape(1, n))


def _merge_kernel(oa_ref, ohg_ref, wa_ref, wb_ref, ga_ref, gb_ref, o_ref):
    a = _mm(oa_ref[...], wa_ref[...])
    b = _mm(ohg_ref[...], wb_ref[...])
    o_ref[...] = (jax.nn.sigmoid(ga_ref[...]) * a + jax.nn.sigmoid(gb_ref[...]) * b).astype(BF16)


def _gated_merge(o_a, o_hg, w_a, w_b, proj, tm, tn):
    m, k = o_a.shape
    n = w_a.shape[1]
    return pl.pallas_call(
        _merge_kernel,
        grid=(m // tm, n // tn),
        in_specs=[pl.BlockSpec((tm, k), lambda i, j: (i, 0)),
                  pl.BlockSpec((tm, k), lambda i, j: (i, 0)),
                  pl.BlockSpec((k, tn), lambda i, j: (0, j)),
                  pl.BlockSpec((k, tn), lambda i, j: (0, j)),
                  pl.BlockSpec((tm, tn), lambda i, j: (i, C_GA // tn + j)),
                  pl.BlockSpec((tm, tn), lambda i, j: (i, C_GB // tn + j))],
        out_specs=pl.BlockSpec((tm, tn), lambda i, j: (i, j)),
        out_shape=jax.ShapeDtypeStruct((m, n), BF16),
        compiler_params=_cparams("parallel", "arbitrary"),
        name="gated_merge",
    )(o_a, o_hg, w_a, w_b, proj, proj)


def _conv_taps(fa, fg, cwa_ref, cwg_ref, cba_ref, cbg_ref, tap):
    ca = cba_ref[...]
    cg = cbg_ref[...]
    for j in range(CONV_W):
        ca = ca + tap(fa, j) * cwa_ref[j:j + 1, :]
        cg = cg + tap(fg, j) * cwg_ref[j:j + 1, :]
    return jax.nn.gelu(cg, approximate=True) * ca


def _conv_glu_seq_kernel(ua_ref, ug_ref, ha_ref, hg_ref, cwa_ref, cwg_ref, cba_ref, cbg_ref,
                         o_ref, fa_ref, fg_ref, *, tm, tiles_per_seq):
    first = (pl.program_id(0) % tiles_per_seq) == 0
    fa_ref[0:8, :] = jnp.where(first, 0.0, ha_ref[...])
    fg_ref[0:8, :] = jnp.where(first, 0.0, hg_ref[...])
    fa_ref[8:, :] = ua_ref[...]
    fg_ref[8:, :] = ug_ref[...]
    tap = lambda f, j: f[8 - (CONV_W - 1) + j:8 - (CONV_W - 1) + j + tm, :]
    o_ref[...] = _conv_taps(fa_ref, fg_ref, cwa_ref, cwg_ref, cba_ref, cbg_ref, tap).astype(o_ref.dtype)


def _conv_glu_seq(u, conv_w, conv_b, seq_len, tm, tn):
    m = u.shape[0]
    nj = D_FF // tn
    hb = tm // 8
    halo = lambda i, j, off: (jnp.maximum(i * hb - 1, 0), j + off)
    return pl.pallas_call(
        functools.partial(_conv_glu_seq_kernel, tm=tm, tiles_per_seq=seq_len // tm),
        grid=(m // tm, nj),
        in_specs=[pl.BlockSpec((tm, tn), lambda i, j: (i, j)),
                  pl.BlockSpec((tm, tn), lambda i, j: (i, j + nj)),
                  pl.BlockSpec((8, tn), lambda i, j: halo(i, j, 0)),
                  pl.BlockSpec((8, tn), lambda i, j: halo(i, j, nj)),
                  pl.BlockSpec((CONV_W, tn), lambda i, j: (0, j)),
                  pl.BlockSpec((CONV_W, tn), lambda i, j: (0, j + nj)),
                  pl.BlockSpec((1, tn), lambda i, j: (0, j)),
                  pl.BlockSpec((1, tn), lambda i, j: (0, j + nj))],
        out_specs=pl.BlockSpec((tm, tn), lambda i, j: (i, j)),
        out_shape=jax.ShapeDtypeStruct((m, D_FF), BF16),
        scratch_shapes=[pltpu.VMEM((tm + 8, tn), F32), pltpu.VMEM((tm + 8, tn), F32)],
        compiler_params=_cparams("parallel", "arbitrary"),
        name="conv_glu_seq",
    )(u, u, u, u, conv_w, conv_w, conv_b.reshape(1, -1), conv_b.reshape(1, -1))


def _conv_glu_step_kernel(ua_ref, ug_ref, ba_ref, bg_ref, cwa_ref, cwg_ref, cba_ref, cbg_ref,
                          o_ref, fa_ref, fg_ref, *, t):
    fa_ref[:, 8 - (CONV_W - 1):8, :] = ba_ref[...]
    fg_ref[:, 8 - (CONV_W - 1):8, :] = bg_ref[...]
    fa_ref[:, 8:, :] = ua_ref[...]
    fg_ref[:, 8:, :] = ug_ref[...]
    tap = lambda f, j: f[:, 8 - (CONV_W - 1) + j:8 - (CONV_W - 1) + j + t, :]
    o_ref[...] = _conv_taps(fa_ref, fg_ref, cwa_ref, cwg_ref, cba_ref, cbg_ref, tap)


def _conv_glu_step(u3, buf, conv_w, conv_b, nb, tn):
    b, t, _ = u3.shape
    nj = D_FF // tn
    return pl.pallas_call(
        functools.partial(_conv_glu_step_kernel, t=t),
        grid=(b // nb, nj),
        in_specs=[pl.BlockSpec((nb, t, tn), lambda i, j: (i, 0, j)),
                  pl.BlockSpec((nb, t, tn), lambda i, j: (i, 0, j + nj)),
                  pl.BlockSpec((nb, CONV_W - 1, tn), lambda i, j: (i, 0, j)),
                  pl.BlockSpec((nb, CONV_W - 1, tn), lambda i, j: (i, 0, j + nj)),
                  pl.BlockSpec((CONV_W, tn), lambda i, j: (0, j)),
                  pl.BlockSpec((CONV_W, tn), lambda i, j: (0, j + nj)),
                  pl.BlockSpec((1, tn), lambda i, j: (0, j)),
                  pl.BlockSpec((1, tn), lambda i, j: (0, j + nj))],
        out_specs=pl.BlockSpec((nb, t, tn), lambda i, j: (i, 0, j)),
        out_shape=jax.ShapeDtypeStruct((b, t, D_FF), F32),
        scratch_shapes=[pltpu.VMEM((nb, 8 + t, tn), F32), pltpu.VMEM((nb, 8 + t, tn), F32)],
        compiler_params=_cparams("parallel", "arbitrary"),
        name="conv_glu_step",
    )(u3, u3, buf, buf, conv_w, conv_w, conv_b.reshape(1, -1), conv_b.reshape(1, -1))


def _compress_body(load_x, pe_ref, w1_ref, w2_ref, o_ref, n_seq, n_cmp):
    for c in range(2):
        acc = jnp.zeros((NSA_GROUPS * n_seq * n_cmp, NSA_HD), F32)
        for l in range(0, NSA_BLOCK, 2):
            halves = [jnp.concatenate([load_x(l + dl, c, g) for g in range(NSA_GROUPS)], axis=0) + pe_ref[c, l + dl]
                      for dl in range(2)]
            acc = acc + _mm(jnp.concatenate(halves, axis=1), w1_ref[c, l // 2])
        out = _mm(jax.nn.silu(acc), w2_ref[c])
        for g in range(NSA_GROUPS):
            for s in range(n_seq):
                r0 = (g * n_seq + s) * n_cmp
                o_ref[s, c, g] = out[r0:r0 + n_cmp]


def _compress_seq_kernel(x0_ref, x1_ref, x2_ref, x3_ref, pe_ref, w1_ref, w2_ref, o_ref, *, n_cmp):
    xs = (x0_ref, x1_ref, x2_ref, x3_ref)
    load_x = lambda l, c, g: xs[c * NSA_GROUPS + g][pl.ds(l, n_cmp, stride=NSA_BLOCK), :]
    _compress_body(load_x, pe_ref, w1_ref, w2_ref, o_ref, 1, n_cmp)


def _compress_seq(proj, b, t, pe, w1, w2):
    n_cmp = t // NSA_BLOCK
    rows = n_cmp * NSA_BLOCK
    assert rows == t
    return pl.pallas_call(
        functools.partial(_compress_seq_kernel, n_cmp=n_cmp),
        grid=(b,),
        in_specs=[pl.BlockSpec((rows, NSA_HD), lambda i, cg=cg: (i, C_KVC // NSA_HD + cg))
                  for cg in range(2 * NSA_GROUPS)] + [
                  pl.BlockSpec(pe.shape, lambda i: (0, 0, 0, 0)),
                  pl.BlockSpec(w1.shape, lambda i: (0, 0, 0, 0)),
                  pl.BlockSpec(w2.shape, lambda i: (0, 0, 0))],
        out_specs=pl.BlockSpec((1, 2, NSA_GROUPS, n_cmp, NSA_HD), lambda i: (i, 0, 0, 0, 0)),
        out_shape=jax.ShapeDtypeStruct((b, 2, NSA_GROUPS, n_cmp, NSA_HD), F32),
        compiler_params=_cparams("parallel"),
        name="compress_seq",
    )(proj, proj, proj, proj, pe, w1, w2)


KV_SLOTS = 2 * NSA_GROUPS
CMP_SEQS = 2
CMP_PITCH = NSA_BLOCK * KV_SLOTS + 8


def _compress_paged_kernel(pt_ref, *refs, n_seq, n_pages, page_rows):
    del pt_ref
    pages = refs[:n_seq * n_pages]
    pe_ref, w1_ref, w2_ref, o_ref, x_ref = refs[n_seq * n_pages:]
    blk_rows = NSA_BLOCK * KV_SLOTS
    per_page = page_rows // blk_rows
    for i, pg in enumerate(pages):
        for j in range(per_page):
            n = i * per_page + j
            x_ref[n * CMP_PITCH:n * CMP_PITCH + blk_rows, :] = pg[j * blk_rows:(j + 1) * blk_rows, :]
    n_cmp = n_pages * per_page
    load_x = lambda l, c, g: x_ref[pl.ds(l * KV_SLOTS + c * NSA_GROUPS + g, n_seq * n_cmp, stride=CMP_PITCH), :]
    _compress_body(load_x, pe_ref, w1_ref, w2_ref, o_ref, n_seq, n_cmp)


def _compress_paged(cache2d, page_table_flat, b, n_pages, page, pe, w1, w2):
    page_rows = page * KV_SLOTS
    n_cmp = n_pages * page // NSA_BLOCK
    n_seq = CMP_SEQS if b % CMP_SEQS == 0 else 1
    page_spec = lambda s, p: pl.BlockSpec(
        (page_rows, NSA_HD), lambda i, pt: (pt[(i * n_seq + s) * n_pages + p], 0))
    grid_spec = pltpu.PrefetchScalarGridSpec(
        num_scalar_prefetch=1,
        grid=(b // n_seq,),
        in_specs=[page_spec(s, p) for s in range(n_seq) for p in range(n_pages)] + [
            pl.BlockSpec(pe.shape, lambda i, pt: (0, 0, 0, 0)),
            pl.BlockSpec(w1.shape, lambda i, pt: (0, 0, 0, 0)),
            pl.BlockSpec(w2.shape, lambda i, pt: (0, 0, 0))],
        out_specs=pl.BlockSpec((n_seq, 2, NSA_GROUPS, n_cmp, NSA_HD), lambda i, pt: (i, 0, 0, 0, 0)),
        scratch_shapes=[pltpu.VMEM((n_seq * n_cmp * CMP_PITCH, NSA_HD), F32)],
    )
    return pl.pallas_call(
        functools.partial(_compress_paged_kernel, n_seq=n_seq, n_pages=n_pages, page_rows=page_rows),
        grid_spec=grid_spec,
        out_shape=jax.ShapeDtypeStruct((b, 2, NSA_GROUPS, n_cmp, NSA_HD), F32),
        compiler_params=_cparams("parallel"),
        name="compress_paged",
    )(page_table_flat, *([cache2d] * (n_seq * n_pages)), pe, w1, w2)


def _alibi_slope(g, h):
    return 2.0 ** (-8.0 * (g * NSA_HPG + h + 1.0) / NSA_HEADS)


def _row_consts(tq, g, q_pos0):
    rows = NSA_HPG * tq
    r = lax.broadcasted_iota(jnp.int32, (rows, 1), 0)
    head = r // tq
    qpos = q_pos0 + (r - head * tq)
    slope = jnp.zeros((rows, 1), F32)
    for h in range(NSA_HPG):
        slope = jnp.where(head == h, _alibi_slope(g, h), slope)
    return qpos, slope


def _stack_heads(q_ref, g):
    return jnp.concatenate(
        [q_ref[:, (g * NSA_HPG + h) * NSA_HD:(g * NSA_HPG + h + 1) * NSA_HD] for h in range(NSA_HPG)], axis=0)


def _compressed_branch(qs, ck, cv, qpos, slope, n_cmp):
    pad = jnp.zeros((LANES - n_cmp, NSA_HD), F32)
    ckp = jnp.concatenate([ck, pad], axis=0)
    cvp = jnp.concatenate([cv, pad], axis=0)
    n = lax.broadcasted_iota(jnp.int32, (1, LANES), 1)
    dist = qpos - ((n + 1) * NSA_BLOCK - 1)
    s = _mm_nt(qs, ckp) * NSA_SCALE - slope * dist.astype(F32)
    mask = (dist >= 0) & (n < n_cmp)
    s = jnp.where(mask, s, NEG_BIG)
    m = jnp.max(s, axis=-1, keepdims=True)
    e = jnp.where(mask, jnp.exp(s - m), 0.0)
    d = jnp.sum(e, axis=-1, keepdims=True)
    p = e / jnp.where(d > 0, d, 1.0)
    return _mm(p, cvp), p


def _select_blocks(imp, q_pos0, n_blk):
    tq = imp.shape[0]
    nb = -(-n_blk // 8) * 8
    imp_t = (imp if tq == LANES else _pad_rows(imp, LANES)).T[0:nb]
    blk = lax.broadcasted_iota(jnp.int32, (nb, LANES), 0)
    cur = (q_pos0 + lax.broadcasted_iota(jnp.int32, (nb, LANES), 1)) // NSA_BLOCK
    forced = (blk == 0) | (blk == cur) | (blk == cur - 1)
    valid = blk <= cur
    score = jnp.where(valid, jnp.where(forced, FORCED_SCORE, imp_t), -1.0)
    score = jnp.where(blk < n_blk, score, -2.0)
    blk_f = blk.astype(F32)
    sel = jnp.zeros((nb, LANES), F32)
    for _ in range(min(NSA_TOPN, n_blk)):
        mx = jnp.max(score, axis=0, keepdims=True)
        first = jnp.min(jnp.where(score == mx, blk_f, 1e9), axis=0, keepdims=True)
        hit = blk_f == first
        sel = jnp.where(hit, 1.0, sel)
        score = jnp.where(hit, -3.0, score)
    return _pad_rows(sel, LANES).T[0:tq]


def _expand_sel(sel, key0, nkeys):
    bi = lax.broadcasted_iota(jnp.int32, (LANES, nkeys), 0)
    ki = lax.broadcasted_iota(jnp.int32, (LANES, nkeys), 1)
    expand = (bi == (key0 + ki) // NSA_BLOCK).astype(BF16)
    return _mm(sel, expand)


def _write_gated(o_ref, gate_ref, g, tq, o_cmp, o_sel, o_win):
    sig = jax.nn.sigmoid(gate_ref[...])
    for h in range(NSA_HPG):
        c0 = (g * NSA_HPG + h) * 3
        rs = slice(h * tq, (h + 1) * tq)
        o = sig[:, c0:c0 + 1] * o_cmp[rs] + sig[:, c0 + 1:c0 + 2] * o_sel[rs] + sig[:, c0 + 2:c0 + 3] * o_win[rs]
        o_ref[:, (g * NSA_HPG + h) * NSA_HD:(g * NSA_HPG + h + 1) * NSA_HD] = o.astype(o_ref.dtype)


SEQ_TK = 256


def _lane_groups(x):
    return [x[:, i:i + LANES] for i in range(0, x.shape[1], LANES)]


def _tiled_attention(load_q, slopes, kv_ref, kcol, vcol, lo, hi, tile_mask, s_ref, m_ref, l_ref, acc_ref):
    n_heads = len(slopes)
    tq = m_ref.shape[0] // n_heads
    head_rows = [slice(h * tq, (h + 1) * tq) for h in range(n_heads)]
    m_ref[...] = jnp.full(m_ref.shape, NEG_BIG, F32)

    def scores(kt, carry):
        key0 = kt * SEQ_TK
        valid, dist = tile_mask(key0)
        k = kv_ref[pl.ds(pl.multiple_of(key0, SEQ_TK), SEQ_TK), kcol]
        for h, rs in enumerate(head_rows):
            s = jnp.where(valid, _mm_nt(load_q(h), k) * NSA_SCALE - slopes[h] * dist, NEG_BIG)
            s_ref[kt, rs] = s
            m_ref[rs] = functools.reduce(jnp.maximum, [m_ref[rs]] + _lane_groups(s))
        return carry

    lax.fori_loop(lo, hi, scores, 0)
    for rs in head_rows:
        m = jnp.maximum(jnp.max(m_ref[rs], axis=-1, keepdims=True), 0.1 * NEG_BIG)
        m_ref[rs] = jnp.broadcast_to(m, (tq, LANES))
    l_ref[...] = jnp.zeros(l_ref.shape, F32)
    acc_ref[...] = jnp.zeros(acc_ref.shape, F32)

    def values(kt, carry):
        v = kv_ref[pl.ds(pl.multiple_of(kt * SEQ_TK, SEQ_TK), SEQ_TK), vcol]
        for rs in head_rows:
            m = m_ref[rs]
            p = [jnp.exp(s - m) for s in _lane_groups(s_ref[kt, rs])]
            l_ref[rs] += sum(p[1:], p[0])
            acc_ref[rs] += _mm(jnp.concatenate(p, axis=1), v)
        return carry

    lax.fori_loop(lo, hi, values, 0)
    outs = []
    for rs in head_rows:
        l = jnp.sum(l_ref[rs], axis=-1, keepdims=True)
        outs.append(acc_ref[rs] / jnp.where(l > 0, l, 1.0))
    return jnp.concatenate(outs, axis=0)


def _nsa_seq_kernel(q_ref, gate_ref, ckv_ref, ks_ref, kw_ref, o_ref, s_ref, m_ref, l_ref, acc_ref,
                    *, tq, n_cmp, n_blk):
    j = pl.program_id(1)
    q0 = j * tq
    lane = lax.broadcasted_iota(jnp.int32, (1, SEQ_TK), 1)
    qpos_t = q0 + lax.broadcasted_iota(jnp.int32, (tq, 1), 0)
    hi = (q0 + tq - 1) // SEQ_TK + 1
    prep = []
    for g in range(NSA_GROUPS):
        qs = _stack_heads(q_ref, g)
        qpos, slope = _row_consts(tq, g, q0)
        o_cmp, p_c = _compressed_branch(qs, ckv_ref[0, 0, g], ckv_ref[0, 1, g], qpos, slope, n_cmp)
        imp = p_c[0:tq]
        for h in range(1, NSA_HPG):
            imp = imp + p_c[h * tq:(h + 1) * tq]
        prep.append((o_cmp, _select_blocks(imp, q0, n_blk)))
    for g in range(NSA_GROUPS):
        o_cmp, sel = prep[g]
        kcol = slice(g * NSA_HD, (g + 1) * NSA_HD)
        vcol = slice((NSA_GROUPS + g) * NSA_HD, (NSA_GROUPS + g + 1) * NSA_HD)
        slopes = [_alibi_slope(g, h) for h in range(NSA_HPG)]
        load_q = lambda h, g=g: q_ref[:, (g * NSA_HPG + h) * NSA_HD:(g * NSA_HPG + h + 1) * NSA_HD]
        scratch = (s_ref, m_ref, l_ref, acc_ref)

        def sel_mask(key0, sel=sel):
            dist = qpos_t - (key0 + lane)
            return (dist >= 0) & (_expand_sel(sel, key0, SEQ_TK) > 0.5), dist.astype(F32)

        o_sel = _tiled_attention(load_q, slopes, ks_ref, kcol, vcol, 0, hi, sel_mask, *scratch)

        def win_mask(key0):
            dist = qpos_t - (key0 + lane)
            return (dist >= 0) & (dist < NSA_WINDOW), dist.astype(F32)

        lo = jnp.maximum(q0 - (NSA_WINDOW - 1), 0) // SEQ_TK
        o_win = _tiled_attention(load_q, slopes, kw_ref, kcol, vcol, lo, hi, win_mask, *scratch)
        _write_gated(o_ref, gate_ref, g, tq, o_cmp, o_sel, o_win)


def _nsa_seq(proj, ckv, b, t):
    tq = 128
    n_cmp = t // NSA_BLOCK
    n_blk = -(-t // NSA_BLOCK)
    nq = t // tq
    assert t % SEQ_TK == 0 and n_blk <= LANES
    return pl.pallas_call(
        functools.partial(_nsa_seq_kernel, tq=tq, n_cmp=n_cmp, n_blk=n_blk),
        scratch_shapes=[pltpu.VMEM((t // SEQ_TK, NSA_HPG * tq, SEQ_TK), F32)]
        + [pltpu.VMEM((NSA_HPG * tq, LANES), F32)] * 3,
        grid=(b, nq),
        in_specs=[pl.BlockSpec((tq, NSA_HEADS * NSA_HD), lambda i, j: (i * nq + j, C_QA // 1024)),
                  pl.BlockSpec((tq, LANES), lambda i, j: (i * nq + j, C_GN // LANES)),
                  pl.BlockSpec((1, 2, NSA_GROUPS, n_cmp, NSA_HD), lambda i, j: (i, 0, 0, 0, 0)),
                  pl.BlockSpec((t, KV_COLS), lambda i, j: (i, C_KVS // KV_COLS)),
                  pl.BlockSpec((t, KV_COLS), lambda i, j: (i, C_KVW // KV_COLS))],
        out_specs=pl.BlockSpec((tq, NSA_HEADS * NSA_HD), lambda i, j: (i * nq + j, 0)),
        out_shape=jax.ShapeDtypeStruct((b * t, NSA_HEADS * NSA_HD), BF16),
        compiler_params=_cparams("parallel", "arbitrary"),
        name="nsa_seq",
    )(proj, proj, ckv, proj, proj)


def _pad_rows(x, rows):
    if x.shape[0] == rows:
        return x
    return jnp.concatenate([x, jnp.zeros((rows - x.shape[0], x.shape[1]), x.dtype)], axis=0)


def _two_pass_attention(qs, tiles):
    scores = []
    for k, _, valid, bias in tiles:
        scores.append(jnp.where(valid, _mm_nt(qs, k) * NSA_SCALE - bias, NEG_BIG))
    m = jnp.max(functools.reduce(jnp.maximum, scores), axis=-1, keepdims=True)
    acc = jnp.zeros((qs.shape[0], NSA_HD), F32)
    lsum = jnp.zeros(scores[0].shape, F32)
    for s, (_, v, _, _) in zip(scores, tiles):
        p = jnp.where(s > 0.5 * NEG_BIG, jnp.exp(s - m), 0.0)
        lsum = lsum + p
        acc = acc + _mm(p, v)
    l = jnp.sum(lsum, axis=-1, keepdims=True)
    return acc / jnp.where(l > 0, l, 1.0)


def _nsa_step_kernel(pt_ref, *refs, t, n_pages, page, past_len, n_win, n_cmp, n_blk):
    del pt_ref
    pages = refs[:n_pages]
    q_ref, gate_ref, ckv_ref, ksn_ref, kwn_ref, win_ref, o_ref = refs[n_pages:]
    lane = lax.broadcasted_iota(jnp.int32, (1, page), 1)
    blocks_per_page = page // NSA_BLOCK
    for g in range(NSA_GROUPS):
        qs = _stack_heads(q_ref, g)
        qpos, slope = _row_consts(t, g, past_len)
        o_cmp, p_c = _compressed_branch(qs, ckv_ref[0, 0, g], ckv_ref[0, 1, g], qpos, slope, n_cmp)
        imp = p_c[0:t]
        for h in range(1, NSA_HPG):
            imp = imp + p_c[h * t:(h + 1) * t]
        sel = _select_blocks(imp, past_len, n_blk)
        kcol = slice(g * NSA_HD, (g + 1) * NSA_HD)
        vcol = slice((NSA_GROUPS + g) * NSA_HD, (NSA_GROUPS + g + 1) * NSA_HD)
        kslot, vslot = g, NSA_GROUPS + g

        tiles = []
        for p in range(n_pages + 1):
            key0 = p * page
            if p < n_pages:
                k = pages[p][pl.ds(kslot, page, stride=KV_SLOTS), :]
                v = pages[p][pl.ds(vslot, page, stride=KV_SLOTS), :]
            else:
                k, v = _pad_rows(ksn_ref[:, kcol], page), _pad_rows(ksn_ref[:, vcol], page)
            chosen = jnp.zeros((t, page), F32)
            for bi in range(blocks_per_page):
                blk = p * blocks_per_page + bi
                chosen = jnp.where(lane // NSA_BLOCK == bi, sel[:, blk:blk + 1], chosen)
            chosen = jnp.concatenate([chosen] * NSA_HPG, axis=0)
            dist = qpos - (key0 + lane)
            tiles.append((k, v, (dist >= 0) & (chosen > 0.5), slope * dist.astype(F32)))
        o_sel = _two_pass_attention(qs, tiles)

        tiles = []
        for p in range(n_win // page + 1):
            key0 = past_len - n_win + p * page
            if p < n_win // page:
                k = win_ref[pl.ds(p * page * KV_SLOTS + kslot, page, stride=KV_SLOTS), :]
                v = win_ref[pl.ds(p * page * KV_SLOTS + vslot, page, stride=KV_SLOTS), :]
            else:
                k, v = _pad_rows(kwn_ref[:, kcol], page), _pad_rows(kwn_ref[:, vcol], page)
            dist = qpos - (key0 + lane)
            valid = (dist >= 0) & (dist < NSA_WINDOW) & (key0 + lane >= 0)
            tiles.append((k, v, valid, slope * dist.astype(F32)))
        o_win = _two_pass_attention(qs, tiles)
        _write_gated(o_ref, gate_ref, g, t, o_cmp, o_sel, o_win)


def _nsa_step(proj, ckv, cache2d, page_table_flat, win2d, b, t, n_pages, page, n_win):
    past_len = n_pages * page
    assert n_win % page == 0 and t <= page
    n_cmp = (past_len + t) // NSA_BLOCK
    n_blk = -(-(past_len + t) // NSA_BLOCK)
    assert n_cmp * NSA_BLOCK == past_len and n_blk <= LANES
    page_spec = lambda p: pl.BlockSpec((page * KV_SLOTS, NSA_HD), lambda i, pt: (pt[i * n_pages + p], 0))
    grid_spec = pltpu.PrefetchScalarGridSpec(
        num_scalar_prefetch=1,
        grid=(b,),
        in_specs=[page_spec(p) for p in range(n_pages)] + [
            pl.BlockSpec((t, NSA_HEADS * NSA_HD), lambda i, pt: (i, C_QA // 1024)),
            pl.BlockSpec((t, LANES), lambda i, pt: (i, C_GN // LANES)),
            pl.BlockSpec((1, 2, NSA_GROUPS, n_cmp, NSA_HD), lambda i, pt: (i, 0, 0, 0, 0)),
            pl.BlockSpec((t, KV_COLS), lambda i, pt: (i, C_KVS // KV_COLS)),
            pl.BlockSpec((t, KV_COLS), lambda i, pt: (i, C_KVW // KV_COLS)),
            pl.BlockSpec((n_win * KV_SLOTS, NSA_HD), lambda i, pt: (i, 0))],
        out_specs=pl.BlockSpec((t, NSA_HEADS * NSA_HD), lambda i, pt: (i, 0)),
    )
    return pl.pallas_call(
        functools.partial(_nsa_step_kernel, t=t, n_pages=n_pages, page=page, past_len=past_len,
                          n_win=n_win, n_cmp=n_cmp, n_blk=n_blk),
        grid_spec=grid_spec,
        out_shape=jax.ShapeDtypeStruct((b * t, NSA_HEADS * NSA_HD), F32),
        compiler_params=_cparams("parallel"),
        name="nsa_step",
    )(page_table_flat, *([cache2d] * n_pages), proj, proj, ckv, proj, proj, win2d)


HG_ROWS = 128


def _hgrn_kernel(q_ref, f_ref, i_ref, og_ref, lb_ref, nw_ref, s0_ref, o_ref, sfin_ref,
                 st_ref, oacc_ref, *, rows_in, n_tblk, has_state):
    tb = pl.program_id(1)

    @pl.when(tb == 0)
    def _():
        for h in range(HG_HEADS):
            if has_state:
                st_ref[h] = s0_ref[0, h].T
            else:
                st_ref[h] = jnp.zeros((HG_DV, HG_DK), F32)

    pr = HG_CHUNK if rows_in <= HG_CHUNK else HG_ROWS
    assert rows_in <= pr

    def padded(ref):
        x = ref[...]
        return x if rows_in == pr else _pad_rows(x, pr)

    def key_rows(x):
        return x if pr == HG_ROWS else _pad_rows(x, HG_ROWS)

    q, f, v = padded(q_ref), padded(f_ref), padded(i_ref)
    lb = lb_ref[...]
    row = lax.broadcasted_iota(jnp.int32, (pr, 1), 0)
    live = row < rows_in
    forget = lb + (1.0 - lb) * jax.nn.sigmoid(f)
    k = jnp.where(live, (1.0 - lb) * jax.nn.sigmoid(-f), 0.0)
    gl = jnp.where(live, jnp.log(forget), 0.0)
    rc = row % HG_CHUNK
    cum, suf = gl, gl
    s = 1
    while s < HG_CHUNK:
        cum = cum + jnp.where(rc >= s, pltpu.roll(cum, s, axis=0), 0.0)
        suf = suf + jnp.where(rc < HG_CHUNK - s, pltpu.roll(suf, pr - s, axis=0), 0.0)
        s *= 2
    ki = k * jnp.exp(-cum)
    qd = {HG_CHUNK: q * jnp.exp(cum)}
    ke = {HG_CHUNK: k * jnp.exp(suf - gl)}
    ci = lax.broadcasted_iota(jnp.int32, (pr, HG_ROWS), 0)
    cj = lax.broadcasted_iota(jnp.int32, (pr, HG_ROWS), 1)
    masks = {HG_CHUNK: (ci // HG_CHUNK == cj // HG_CHUNK) & (ci >= cj)}
    w = HG_CHUNK
    while w < pr:
        tot = cum + suf - gl
        odd = (row // w) % 2 == 1
        cum = cum + jnp.where(odd, pltpu.roll(tot, w, axis=0), 0.0)
        suf = suf + jnp.where(odd, 0.0, pltpu.roll(tot, pr - w, axis=0))
        masks[2 * w] = ((ci // w) % 2 == 1) & (cj // w == ci // w - 1)
        w *= 2
        qd[w] = q * jnp.exp(cum)
        ke[w] = k * jnp.exp(suf - gl)
    decay = jnp.exp((cum + suf - gl)[0:1, :])
    for h in range(HG_HEADS):
        hs = slice(h * HG_DK, (h + 1) * HG_DK)
        att = jnp.where(masks[HG_CHUNK], _mm_nt(qd[HG_CHUNK][:, hs], key_rows(ki[:, hs])), 0.0)
        w = HG_CHUNK
        while w < pr:
            att = att + jnp.where(masks[2 * w], _mm_nt(qd[w][:, hs], key_rows(ke[w][:, hs])), 0.0)
            w *= 2
        v_h = key_rows(v[:, hs])
        st = st_ref[h]
        oacc_ref[0:pr, hs] = _mm(att, v_h) + _mm_nt(qd[pr][:, hs], st)
        st_ref[h] = decay[:, hs] * st + _mm(v_h.T, key_rows(ke[pr][:, hs]))
    og = og_ref[...]
    nw = nw_ref[...]
    for h in range(HG_HEADS):
        hs = slice(h * HG_DV, (h + 1) * HG_DV)
        x = oacc_ref[0:rows_in, hs]
        ms = jnp.mean(x * x, axis=-1, keepdims=True)
        y = x * lax.rsqrt(ms + EPS) * nw
        o_ref[:, hs] = (y * jax.nn.silu(og[:, hs])).astype(o_ref.dtype)

    @pl.when(tb == n_tblk - 1)
    def _():
        for h in range(HG_HEADS):
            sfin_ref[0, h] = st_ref[h].T


def _hgrn(proj, lb, norm_w, s0, b, t):
    rows_in = min(t, HG_ROWS)
    n_tblk = t // rows_in
    assert rows_in * n_tblk == t and rows_in % 8 == 0
    has_state = s0 is not None
    if s0 is None:
        s0 = jnp.zeros((1, HG_HEADS, HG_DK, HG_DV), F32)
    width = HG_HEADS * HG_DK
    col = lambda c: pl.BlockSpec((rows_in, width), lambda i, j: (i * n_tblk + j, c // width))
    return pl.pallas_call(
        functools.partial(_hgrn_kernel, rows_in=rows_in, n_tblk=n_tblk, has_state=has_state),
        grid=(b, n_tblk),
        in_specs=[col(C_QB), col(C_FB), col(C_IB), col(C_OG),
                  pl.BlockSpec((1, width), lambda i, j: (0, 0)),
                  pl.BlockSpec((1, HG_DV), lambda i, j: (0, 0)),
                  pl.BlockSpec((1, HG_HEADS, HG_DK, HG_DV),
                               (lambda i, j: (i, 0, 0, 0)) if has_state else (lambda i, j: (0, 0, 0, 0)))],
        out_specs=[pl.BlockSpec((rows_in, width), lambda i, j: (i * n_tblk + j, 0)),
                   pl.BlockSpec((1, HG_HEADS, HG_DK, HG_DV), lambda i, j: (i, 0, 0, 0))],
        out_shape=[jax.ShapeDtypeStruct((b * t, width), BF16),
                   jax.ShapeDtypeStruct((b, HG_HEADS, HG_DK, HG_DV), F32)],
        scratch_shapes=[pltpu.VMEM((HG_HEADS, HG_DV, HG_DK), F32), pltpu.VMEM((HG_ROWS, width), F32)],
        compiler_params=_cparams("parallel", "arbitrary"),
        name="hgrn2",
    )(proj, proj, proj, proj, lb.reshape(1, width), norm_w.reshape(1, HG_DV), s0)


def _decoder_layer(x, past, lb, params):
    (w_in, w_a, w_b, w_o, pe, w1, w2, hg_norm, ln1, ln2, ln3, ln4, w_up, cw, cb, w_dn) = params
    b, t, d = x.shape
    m = b * t
    x2 = x.reshape(m, d)
    tm = min(MM_ROWS, m)
    proj = _norm_matmul(x2, ln1, w_in, tm, 640)
    kv_shape = (b, t, 2, NSA_GROUPS, NSA_HD)
    kvc = proj[:, C_KVC:C_KVC + KV_COLS].reshape(kv_shape)
    kvs = proj[:, C_KVS:C_KVS + KV_COLS].reshape(kv_shape)
    kvw = proj[:, C_KVW:C_KVW + KV_COLS].reshape(kv_shape)
    if past is None:
        ckv = _compress_seq(proj, b, t, pe, w1, w2)
        o_a = _nsa_seq(proj, ckv, b, t)
        new_win = kvw[:, t - min(NSA_WINDOW, t):]
        s0, conv_buf = None, None
    else:
        cache_c, cache_s, page_table_flat, n_pages, page, win_buf, s0, conv_buf = past
        n_win = win_buf.shape[1]
        ckv = _compress_paged(cache_c, page_table_flat, b, n_pages, page, pe, w1, w2)
        o_a = _nsa_step(proj, ckv, cache_s, page_table_flat, win_buf.reshape(-1, NSA_HD), b, t, n_pages, page, n_win)
        new_win = jnp.concatenate([win_buf, kvw], axis=1)[:, t:]
    o_hg, s_fin = _hgrn(proj, lb, hg_norm, s0, b, t)
    mixed = _gated_merge(o_a.astype(BF16), o_hg, w_a, w_b, proj, tm, 512)
    x1 = _matmul_norm_res(mixed, w_o, x2, ln2, tm, 512)
    u = _norm_matmul(x1, ln3, w_up, tm, 512)
    if past is None:
        act = _conv_glu_seq(u, cw, cb, t, min(CONV_ROWS, t), 512)
        new_conv = u.reshape(b, t, 2 * D_FF)[:, t - (CONV_W - 1):]
    else:
        act = _conv_glu_step(u.reshape(b, t, 2 * D_FF), conv_buf, cw, cb, min(64, b), 512).reshape(m, D_FF)
        new_conv = jnp.concatenate([conv_buf, u.reshape(b, t, 2 * D_FF)], axis=1)[:, t:]
    y = _matmul_norm_res(act, w_dn, x1, ln4, tm, 512)
    return y.reshape(b, t, d), (kvc, kvs, new_win, s_fin, new_conv)


def _reorder_w_in(w):
    sizes = (NSA_HEADS * NSA_HD, KV_COLS, KV_COLS, KV_COLS, 3 * NSA_HEADS)
    o_kvc = sizes[0]
    o_gn = o_kvc + 3 * KV_COLS
    o_rest = o_gn + sizes[4]
    pad = jnp.zeros((w.shape[0], N_PROJ - C_GN - sizes[4]), w.dtype)
    return jnp.concatenate([w[:, :o_kvc], w[:, o_rest:], w[:, o_kvc:o_gn], w[:, o_gn:o_rest], pad], axis=1)


def kernel(x_prompt, x_sample, cache_cmp_kv, cache_sel_kv, state_win_kv, state_hgrn, state_conv, page_table,
           w_in, w_branch_a, w_branch_b, w_out, cmp_pe, cmp_w1, cmp_w2, hg_lb_raw, hg_norm_w, ln_mix_pre,
           ln_mix_post, ln_ffn_pre, ln_ffn_post, w_up, conv_w, conv_b, w_down):
    depth = w_in.shape[0]
    dec_b, n_pages = page_table.shape
    page = cache_cmp_kv.shape[2]
    lb_all = jnp.cumsum(jax.nn.softmax(hg_lb_raw.astype(F32), axis=0), axis=0)
    pt_flat = page_table.reshape(-1).astype(jnp.int32)
    y_p, y_s = x_prompt, x_sample
    new_p, new_s = [], []
    for l in range(depth):
        params = (_reorder_w_in(w_in[l]).astype(BF16), w_branch_a[l].astype(BF16), w_branch_b[l].astype(BF16),
                  w_out[l].astype(BF16), cmp_pe[l].transpose(1, 0, 2)[:, :, None, :],
                  cmp_w1[l].astype(BF16).transpose(1, 0, 2, 3).reshape(2, NSA_BLOCK // 2, 2 * NSA_HD, NSA_HD),
                  cmp_w2[l].astype(BF16), hg_norm_w[l], ln_mix_pre[l], ln_mix_post[l], ln_ffn_pre[l],
                  ln_ffn_post[l], w_up[l].astype(BF16), conv_w[l], conv_b[l], w_down[l].astype(BF16))
        y_p, st_p = _decoder_layer(y_p, None, lb_all[l], params)
        past = (cache_cmp_kv[l].reshape(-1, NSA_HD), cache_sel_kv[l].reshape(-1, NSA_HD),
                pt_flat, n_pages, page, state_win_kv[l], state_hgrn[l], state_conv[l])
        y_s, st_s = _decoder_layer(y_s, past, lb_all[l], params)
        new_p.append(st_p)
        new_s.append(st_s)

    def stack(group, i):
        return jnp.stack([st[i] for st in group], axis=0)

    return (y_p, y_s, stack(new_p, 0), stack(new_s, 0), stack(new_p, 1), stack(new_s, 1), stack(new_p, 2),
            stack(new_s, 2), stack(new_p, 3), stack(new_s, 3), stack(new_p, 4), stack(new_s, 4))
```

```python
import functools

import jax
import jax.numpy as jnp
from jax import lax
from jax.experimental import pallas as pl
from jax.experimental.pallas import tpu as pltpu

F32 = jnp.float32
BF16 = jnp.bfloat16

D_MODEL = 2048
NSA_HEADS = 8
NSA_GROUPS = 2
NSA_HPG = NSA_HEADS // NSA_GROUPS
NSA_HD = 128
NSA_BLOCK = 64
NSA_TOPN = 8
NSA_WINDOW = 512
NSA_SCALE = NSA_HD ** -0.5
FORCED_SCORE = NSA_HPG + 1.0
HG_HEADS = 8
HG_DK = 128
HG_DV = 128
HG_CHUNK = 16
D_FF = 5632
CONV_W = 3
EPS = 1e-6
KV_COLS = 2 * NSA_GROUPS * NSA_HD

C_QA = 0
C_QB = 1024
C_FB = 2048
C_IB = 3072
C_OG = 4096
C_GA = 5120
C_GB = 7168
C_KVC = 9216
C_KVS = 9728
C_KVW = 10240
C_GN = 10752

LANES = 128
MM_ROWS = 1024
CONV_ROWS = 512
NEG_BIG = -1e30
VMEM_LIMIT = 56 * 1024 * 1024


def _cparams(*sem):
    return pltpu.CompilerParams(dimension_semantics=sem, vmem_limit_bytes=VMEM_LIMIT)


def _mm(a, b):
    return jnp.dot(a.astype(BF16), b.astype(BF16), preferred_element_type=F32)


def _mm_nt(a, b):
    return lax.dot_general(a.astype(BF16), b.astype(BF16), (((1,), (1,)), ((), ())),
                           preferred_element_type=F32)


def _norm_mm_kernel(x_ref, ln_ref, w_ref, o_ref, h_ref):
    @pl.when(pl.program_id(1) == 0)
    def _():
        x = x_ref[...]
        ms = jnp.mean(x * x, axis=-1, keepdims=True)
        h_ref[...] = (x * lax.rsqrt(ms + EPS) * ln_ref[...]).astype(BF16)

    o_ref[...] = _mm(h_ref[...], w_ref[...])


def _norm_matmul(x, ln, w, tm, tn):
    m, k = x.shape
    n = w.shape[1]
    return pl.pallas_call(
        _norm_mm_kernel,
        grid=(m // tm, n // tn),
        in_specs=[pl.BlockSpec((tm, k), lambda i, j: (i, 0)),
                  pl.BlockSpec((1, k), lambda i, j: (0, 0)),
                  pl.BlockSpec((k, tn), lambda i, j: (0, j))],
        out_specs=pl.BlockSpec((tm, tn), lambda i, j: (i, j)),
        out_shape=jax.ShapeDtypeStruct((m, n), F32),
        scratch_shapes=[pltpu.VMEM((tm, k), BF16)],
        compiler_params=_cparams("parallel", "arbitrary"),
        name="norm_matmul",
    )(x, ln.reshape(1, k), w)


PROJ_TN = 512
TILES_Q = (C_QB - C_QA) // PROJ_TN
TILE_REST0 = C_QB // PROJ_TN
TILE_KV0 = C_KVC // PROJ_TN
TILE_GATE = C_GN // PROJ_TN
N_PROJ = (TILE_GATE + 1) * PROJ_TN


def _in_proj_kernel(x_ref, ln_ref, wh_ref, wr_ref, wg_ref, o_ref, h_ref):
    j = pl.program_id(1)

    @pl.when(j == 0)
    def _():
        x = x_ref[...]
        ms = jnp.mean(x * x, axis=-1, keepdims=True)
        h_ref[...] = (x * lax.rsqrt(ms + EPS) * ln_ref[...]).astype(BF16)

    from_rest = (j >= TILE_REST0) & (j < TILE_KV0)
    from_gate = j == TILE_GATE

    @pl.when(from_rest)
    def _():
        o_ref[...] = _mm(h_ref[...], wr_ref[...])

    @pl.when(from_gate)
    def _():
        o_ref[...] = _mm(h_ref[...], wg_ref[...])

    @pl.when(jnp.logical_not(from_rest | from_gate))
    def _():
        o_ref[...] = _mm(h_ref[...], wh_ref[...])


def _in_projection(x, ln, w_head, w_rest, w_gate, tm):
    m, k = x.shape
    tn = PROJ_TN
    n_head = w_head.shape[1] // tn
    n_rest = w_rest.shape[1] // tn
    assert n_head == TILES_Q + TILE_GATE - TILE_KV0 and n_rest == TILE_KV0 - TILE_REST0 and w_gate.shape[1] == tn
    head_tile = lambda j: jnp.where(j < TILES_Q, j, jnp.clip(j - (TILE_KV0 - TILES_Q), TILES_Q - 1, n_head - 1))
    rest_tile = lambda j: jnp.clip(j - TILE_REST0, 0, n_rest - 1)
    return pl.pallas_call(
        _in_proj_kernel,
        grid=(m // tm, N_PROJ // tn),
        in_specs=[pl.BlockSpec((tm, k), lambda i, j: (i, 0)),
                  pl.BlockSpec((1, k), lambda i, j: (0, 0)),
                  pl.BlockSpec((k, tn), lambda i, j: (0, head_tile(j))),
                  pl.BlockSpec((k, tn), lambda i, j: (0, rest_tile(j))),
                  pl.BlockSpec((k, tn), lambda i, j: (0, 0))],
        out_specs=pl.BlockSpec((tm, tn), lambda i, j: (i, j)),
        out_shape=jax.ShapeDtypeStruct((m, N_PROJ), F32),
        scratch_shapes=[pltpu.VMEM((tm, k), BF16)],
        compiler_params=_cparams("parallel", "arbitrary"),
        name="in_projection",
    )(x, ln.reshape(1, k), w_head, w_rest, w_gate)


def _kv_relayout_kernel(c_ref, s_ref, w_ref, oc_ref, os_ref, ow_ref, *, tm):
    for x_ref, o_ref in ((c_ref, oc_ref), (s_ref, os_ref), (w_ref, ow_ref)):
        for slot in range(2 * NSA_GROUPS):
            o_ref[pl.ds(slot, tm, stride=2 * NSA_GROUPS), :] = x_ref[:, slot * NSA_HD:(slot + 1) * NSA_HD]


def _kv_relayout(proj, tm):
    m = proj.shape[0]
    slots = 2 * NSA_GROUPS
    out = jax.ShapeDtypeStruct((m * slots, NSA_HD), F32)
    return pl.pallas_call(
        functools.partial(_kv_relayout_kernel, tm=tm),
        grid=(m // tm,),
        in_specs=[pl.BlockSpec((tm, KV_COLS), lambda i, c=c: (i, c // KV_COLS)) for c in (C_KVC, C_KVS, C_KVW)],
        out_specs=[pl.BlockSpec((tm * slots, NSA_HD), lambda i: (i, 0))] * 3,
        out_shape=[out] * 3,
        compiler_params=_cparams("parallel"),
        name="kv_relayout",
    )(proj, proj, proj)


def _mm_norm_res_kernel(a_ref, w_ref, res_ref, ln_ref, o_ref, acc_ref, *, nk):
    k = pl.program_id(1)

    @pl.when(k == 0)
    def _():
        acc_ref[...] = jnp.zeros_like(acc_ref)

    acc_ref[...] += _mm(a_ref[...], w_ref[...])

    @pl.when(k == nk - 1)
    def _():
        y = acc_ref[...]
        ms = jnp.mean(y * y, axis=-1, keepdims=True)
        o_ref[...] = res_ref[...] + y * lax.rsqrt(ms + EPS) * ln_ref[...]


def _matmul_norm_res(a, w, res, ln, tm, tk):
    m, kk = a.shape
    n = w.shape[1]
    nk = kk // tk
    return pl.pallas_call(
        functools.partial(_mm_norm_res_kernel, nk=nk),
        grid=(m // tm, nk),
        in_specs=[pl.BlockSpec((tm, tk), lambda i, k: (i, k)),
                  pl.BlockSpec((tk, n), lambda i, k: (k, 0)),
                  pl.BlockSpec((tm, n), lambda i, k: (i, 0)),
                  pl.BlockSpec((1, n), lambda i, k: (0, 0))],
        out_specs=pl.BlockSpec((tm, n), lambda i, k: (i, 0)),
        out_shape=jax.ShapeDtypeStruct((m, n), F32),
        scratch_shapes=[pltpu.VMEM((tm, n), F32)],
        compiler_params=_cparams("parallel", "arbitrary"),
        name="matmul_norm_res",
    )(a, w, res, ln.reshape(1, n))


def _merge_kernel(oa_ref, ohg_ref, wa_ref, wb_ref, ga_ref, gb_ref, o_ref):
    a = _mm(oa_ref[...], wa_ref[...])
    b = _mm(ohg_ref[...], wb_ref[...])
    o_ref[...] = (jax.nn.sigmoid(ga_ref[...]) * a + jax.nn.sigmoid(gb_ref[...]) * b).astype(BF16)


def _gated_merge(o_a, o_hg, w_a, w_b, proj, tm, tn):
    m, k = o_a.shape
    n = w_a.shape[1]
    return pl.pallas_call(
        _merge_kernel,
        grid=(m // tm, n // tn),
        in_specs=[pl.BlockSpec((tm, k), lambda i, j: (i, 0)),
                  pl.BlockSpec((tm, k), lambda i, j: (i, 0)),
                  pl.BlockSpec((k, tn), lambda i, j: (0, j)),
                  pl.BlockSpec((k, tn), lambda i, j: (0, j)),
                  pl.BlockSpec((tm, tn), lambda i, j: (i, C_GA // tn + j)),
                  pl.BlockSpec((tm, tn), lambda i, j: (i, C_GB // tn + j))],
        out_specs=pl.BlockSpec((tm, tn), lambda i, j: (i, j)),
        out_shape=jax.ShapeDtypeStruct((m, n), BF16),
        compiler_params=_cparams("parallel", "arbitrary"),
        name="gated_merge",
    )(o_a, o_hg, w_a, w_b, proj, proj)


def _conv_taps(fa, fg, cwa_ref, cwg_ref, cba_ref, cbg_ref, tap):
    ca = cba_ref[...]
    cg = cbg_ref[...]
    for j in range(CONV_W):
        ca = ca + tap(fa, j) * cwa_ref[j:j + 1, :]
        cg = cg + tap(fg, j) * cwg_ref[j:j + 1, :]
    return jax.nn.gelu(cg, approximate=True) * ca


def _conv_glu_down_kernel(ua_ref, ug_ref, ha_ref, hg_ref, cwa_ref, cwg_ref, cba_ref, cbg_ref, w_ref, res_ref,
                          ln_ref, o_ref, fa_ref, fg_ref, acc_ref, *, tm, tiles_per_seq, nk):
    k = pl.program_id(1)
    first = (pl.program_id(0) % tiles_per_seq) == 0
    fa_ref[0:8, :] = jnp.where(first, 0.0, ha_ref[...])
    fg_ref[0:8, :] = jnp.where(first, 0.0, hg_ref[...])
    fa_ref[8:, :] = ua_ref[...]
    fg_ref[8:, :] = ug_ref[...]
    tap = lambda f, j: f[8 - (CONV_W - 1) + j:8 - (CONV_W - 1) + j + tm, :]
    act = _conv_taps(fa_ref, fg_ref, cwa_ref, cwg_ref, cba_ref, cbg_ref, tap)

    @pl.when(k == 0)
    def _():
        acc_ref[...] = jnp.zeros_like(acc_ref)

    acc_ref[...] += _mm(act, w_ref[...])

    @pl.when(k == nk - 1)
    def _():
        y = acc_ref[...]
        ms = jnp.mean(y * y, axis=-1, keepdims=True)
        o_ref[...] = res_ref[...] + y * lax.rsqrt(ms + EPS) * ln_ref[...]


def _conv_glu_down(u, conv_w, conv_b, w, res, ln, seq_len, tm, tk):
    m = u.shape[0]
    n = w.shape[1]
    nk = D_FF // tk
    hb = tm // 8
    halo = lambda i, k, off: (jnp.maximum(i * hb - 1, 0), k + off)
    return pl.pallas_call(
        functools.partial(_conv_glu_down_kernel, tm=tm, tiles_per_seq=seq_len // tm, nk=nk),
        grid=(m // tm, nk),
        in_specs=[pl.BlockSpec((tm, tk), lambda i, k: (i, k)),
                  pl.BlockSpec((tm, tk), lambda i, k: (i, k + nk)),
                  pl.BlockSpec((8, tk), lambda i, k: halo(i, k, 0)),
                  pl.BlockSpec((8, tk), lambda i, k: halo(i, k, nk)),
                  pl.BlockSpec((CONV_W, tk), lambda i, k: (0, k)),
                  pl.BlockSpec((CONV_W, tk), lambda i, k: (0, k + nk)),
                  pl.BlockSpec((1, tk), lambda i, k: (0, k)),
                  pl.BlockSpec((1, tk), lambda i, k: (0, k + nk)),
                  pl.BlockSpec((tk, n), lambda i, k: (k, 0)),
                  pl.BlockSpec((tm, n), lambda i, k: (i, 0)),
                  pl.BlockSpec((1, n), lambda i, k: (0, 0))],
        out_specs=pl.BlockSpec((tm, n), lambda i, k: (i, 0)),
        out_shape=jax.ShapeDtypeStruct((m, n), F32),
        scratch_shapes=[pltpu.VMEM((tm + 8, tk), F32), pltpu.VMEM((tm + 8, tk), F32), pltpu.VMEM((tm, n), F32)],
        compiler_params=_cparams("parallel", "arbitrary"),
        name="conv_glu_down",
    )(u, u, u, u, conv_w, conv_w, conv_b.reshape(1, -1), conv_b.reshape(1, -1), w, res, ln.reshape(1, n))


def _conv_glu_step_kernel(ua_ref, ug_ref, ba_ref, bg_ref, cwa_ref, cwg_ref, cba_ref, cbg_ref,
                          o_ref, fa_ref, fg_ref, *, t):
    fa_ref[:, 8 - (CONV_W - 1):8, :] = ba_ref[...]
    fg_ref[:, 8 - (CONV_W - 1):8, :] = bg_ref[...]
    fa_ref[:, 8:, :] = ua_ref[...]
    fg_ref[:, 8:, :] = ug_ref[...]
    tap = lambda f, j: f[:, 8 - (CONV_W - 1) + j:8 - (CONV_W - 1) + j + t, :]
    o_ref[...] = _conv_taps(fa_ref, fg_ref, cwa_ref, cwg_ref, cba_ref, cbg_ref, tap)


def _conv_glu_step(u3, buf, conv_w, conv_b, nb, tn):
    b, t, _ = u3.shape
    nj = D_FF // tn
    return pl.pallas_call(
        functools.partial(_conv_glu_step_kernel, t=t),
        grid=(b // nb, nj),
        in_specs=[pl.BlockSpec((nb, t, tn), lambda i, j: (i, 0, j)),
                  pl.BlockSpec((nb, t, tn), lambda i, j: (i, 0, j + nj)),
                  pl.BlockSpec((nb, CONV_W - 1, tn), lambda i, j: (i, 0, j)),
                  pl.BlockSpec((nb, CONV_W - 1, tn), lambda i, j: (i, 0, j + nj)),
                  pl.BlockSpec((CONV_W, tn), lambda i, j: (0, j)),
                  pl.BlockSpec((CONV_W, tn), lambda i, j: (0, j + nj)),
                  pl.BlockSpec((1, tn), lambda i, j: (0, j)),
                  pl.BlockSpec((1, tn), lambda i, j: (0, j + nj))],
        out_specs=pl.BlockSpec((nb, t, tn), lambda i, j: (i, 0, j)),
        out_shape=jax.ShapeDtypeStruct((b, t, D_FF), F32),
        scratch_shapes=[pltpu.VMEM((nb, 8 + t, tn), F32), pltpu.VMEM((nb, 8 + t, tn), F32)],
        compiler_params=_cparams("parallel", "arbitrary"),
        name="conv_glu_step",
    )(u3, u3, buf, buf, conv_w, conv_w, conv_b.reshape(1, -1), conv_b.reshape(1, -1))


def _compress_body(load_x, pe_ref, w1_ref, w2_ref, o_ref, n_seq, n_cmp):
    for c in range(2):
        acc = jnp.zeros((NSA_GROUPS * n_seq * n_cmp, NSA_HD), F32)
        for l in range(0, NSA_BLOCK, 2):
            halves = [jnp.concatenate([load_x(l + dl, c, g) for g in range(NSA_GROUPS)], axis=0) + pe_ref[c, l + dl]
                      for dl in range(2)]
            acc = acc + _mm(jnp.concatenate(halves, axis=1), w1_ref[c, l // 2])
        out = _mm(jax.nn.silu(acc), w2_ref[c])
        for g in range(NSA_GROUPS):
            for s in range(n_seq):
                r0 = (g * n_seq + s) * n_cmp
                o_ref[s, c, g] = out[r0:r0 + n_cmp]


def _compress_seq_kernel(x0_ref, x1_ref, x2_ref, x3_ref, pe_ref, w1_ref, w2_ref, o_ref, *, n_cmp):
    xs = (x0_ref, x1_ref, x2_ref, x3_ref)
    load_x = lambda l, c, g: xs[c * NSA_GROUPS + g][pl.ds(l, n_cmp, stride=NSA_BLOCK), :]
    _compress_body(load_x, pe_ref, w1_ref, w2_ref, o_ref, 1, n_cmp)


def _compress_seq(proj, b, t, pe, w1, w2):
    n_cmp = t // NSA_BLOCK
    rows = n_cmp * NSA_BLOCK
    assert rows == t
    return pl.pallas_call(
        functools.partial(_compress_seq_kernel, n_cmp=n_cmp),
        grid=(b,),
        in_specs=[pl.BlockSpec((rows, NSA_HD), lambda i, cg=cg: (i, C_KVC // NSA_HD + cg))
                  for cg in range(2 * NSA_GROUPS)] + [
                  pl.BlockSpec(pe.shape, lambda i: (0, 0, 0, 0)),
                  pl.BlockSpec(w1.shape, lambda i: (0, 0, 0, 0)),
                  pl.BlockSpec(w2.shape, lambda i: (0, 0, 0))],
        out_specs=pl.BlockSpec((1, 2, NSA_GROUPS, n_cmp, NSA_HD), lambda i: (i, 0, 0, 0, 0)),
        out_shape=jax.ShapeDtypeStruct((b, 2, NSA_GROUPS, n_cmp, NSA_HD), F32),
        compiler_params=_cparams("parallel"),
        name="compress_seq",
    )(proj, proj, proj, proj, pe, w1, w2)


KV_SLOTS = 2 * NSA_GROUPS
CMP_SEQS = 2
CMP_PITCH = NSA_BLOCK * KV_SLOTS + 8


def _compress_paged_kernel(pt_ref, *refs, n_seq, n_pages, page_rows):
    del pt_ref
    pages = refs[:n_seq * n_pages]
    pe_ref, w1_ref, w2_ref, o_ref, x_ref = refs[n_seq * n_pages:]
    blk_rows = NSA_BLOCK * KV_SLOTS
    per_page = page_rows // blk_rows
    for i, pg in enumerate(pages):
        for j in range(per_page):
            n = i * per_page + j
            x_ref[n * CMP_PITCH:n * CMP_PITCH + blk_rows, :] = pg[j * blk_rows:(j + 1) * blk_rows, :]
    n_cmp = n_pages * per_page
    load_x = lambda l, c, g: x_ref[pl.ds(l * KV_SLOTS + c * NSA_GROUPS + g, n_seq * n_cmp, stride=CMP_PITCH), :]
    _compress_body(load_x, pe_ref, w1_ref, w2_ref, o_ref, n_seq, n_cmp)


def _compress_paged(cache2d, page_table_flat, b, n_pages, page, pe, w1, w2):
    page_rows = page * KV_SLOTS
    n_cmp = n_pages * page // NSA_BLOCK
    n_seq = CMP_SEQS if b % CMP_SEQS == 0 else 1
    page_spec = lambda s, p: pl.BlockSpec(
        (page_rows, NSA_HD), lambda i, pt: (pt[(i * n_seq + s) * n_pages + p], 0))
    grid_spec = pltpu.PrefetchScalarGridSpec(
        num_scalar_prefetch=1,
        grid=(b // n_seq,),
        in_specs=[page_spec(s, p) for s in range(n_seq) for p in range(n_pages)] + [
            pl.BlockSpec(pe.shape, lambda i, pt: (0, 0, 0, 0)),
            pl.BlockSpec(w1.shape, lambda i, pt: (0, 0, 0, 0)),
            pl.BlockSpec(w2.shape, lambda i, pt: (0, 0, 0))],
        out_specs=pl.BlockSpec((n_seq, 2, NSA_GROUPS, n_cmp, NSA_HD), lambda i, pt: (i, 0, 0, 0, 0)),
        scratch_shapes=[pltpu.VMEM((n_seq * n_cmp * CMP_PITCH, NSA_HD), F32)],
    )
    return pl.pallas_call(
        functools.partial(_compress_paged_kernel, n_seq=n_seq, n_pages=n_pages, page_rows=page_rows),
        grid_spec=grid_spec,
        out_shape=jax.ShapeDtypeStruct((b, 2, NSA_GROUPS, n_cmp, NSA_HD), F32),
        compiler_params=_cparams("parallel"),
        name="compress_paged",
    )(page_table_flat, *([cache2d] * (n_seq * n_pages)), pe, w1, w2)


def _alibi_slope(g, h):
    return 2.0 ** (-8.0 * (g * NSA_HPG + h + 1.0) / NSA_HEADS)


def _row_consts(tq, g, q_pos0):
    rows = NSA_HPG * tq
    r = lax.broadcasted_iota(jnp.int32, (rows, 1), 0)
    head = r // tq
    qpos = q_pos0 + (r - head * tq)
    slope = jnp.zeros((rows, 1), F32)
    for h in range(NSA_HPG):
        slope = jnp.where(head == h, _alibi_slope(g, h), slope)
    return qpos, slope


def _stack_heads(q_ref, g):
    return jnp.concatenate(
        [q_ref[:, (g * NSA_HPG + h) * NSA_HD:(g * NSA_HPG + h + 1) * NSA_HD] for h in range(NSA_HPG)], axis=0)


def _compressed_branch(qs, ck, cv, qpos, slope, n_cmp):
    pad = jnp.zeros((LANES - n_cmp, NSA_HD), F32)
    ckp = jnp.concatenate([ck, pad], axis=0)
    cvp = jnp.concatenate([cv, pad], axis=0)
    n = lax.broadcasted_iota(jnp.int32, (1, LANES), 1)
    dist = qpos - ((n + 1) * NSA_BLOCK - 1)
    s = _mm_nt(qs, ckp) * NSA_SCALE - slope * dist.astype(F32)
    mask = (dist >= 0) & (n < n_cmp)
    s = jnp.where(mask, s, NEG_BIG)
    m = jnp.max(s, axis=-1, keepdims=True)
    e = jnp.where(mask, jnp.exp(s - m), 0.0)
    d = jnp.sum(e, axis=-1, keepdims=True)
    p = e / jnp.where(d > 0, d, 1.0)
    return _mm(p, cvp), p


def _select_blocks(imp, q_pos0, n_blk):
    tq = imp.shape[0]
    nb = -(-n_blk // 8) * 8
    imp_t = (imp if tq == LANES else _pad_rows(imp, LANES)).T[0:nb]
    blk = lax.broadcasted_iota(jnp.int32, (nb, LANES), 0)
    cur = (q_pos0 + lax.broadcasted_iota(jnp.int32, (nb, LANES), 1)) // NSA_BLOCK
    forced = (blk == 0) | (blk == cur) | (blk == cur - 1)
    valid = blk <= cur
    score = jnp.where(valid, jnp.where(forced, FORCED_SCORE, imp_t), -1.0)
    score = jnp.where(blk < n_blk, score, -2.0)
    blk_f = blk.astype(F32)
    sel = jnp.zeros((nb, LANES), F32)
    for _ in range(min(NSA_TOPN, n_blk)):
        mx = jnp.max(score, axis=0, keepdims=True)
        first = jnp.min(jnp.where(score == mx, blk_f, 1e9), axis=0, keepdims=True)
        hit = blk_f == first
        sel = jnp.where(hit, 1.0, sel)
        score = jnp.where(hit, -3.0, score)
    return _pad_rows(sel, LANES).T[0:tq]


def _expand_sel(sel, key0, nkeys):
    bi = lax.broadcasted_iota(jnp.int32, (LANES, nkeys), 0)
    ki = lax.broadcasted_iota(jnp.int32, (LANES, nkeys), 1)
    expand = (bi == (key0 + ki) // NSA_BLOCK).astype(BF16)
    return _mm(sel, expand)


def _write_gated(o_ref, gate_ref, g, tq, o_cmp, o_sel, o_win):
    sig = jax.nn.sigmoid(gate_ref[...])
    for h in range(NSA_HPG):
        c0 = (g * NSA_HPG + h) * 3
        rs = slice(h * tq, (h + 1) * tq)
        o = sig[:, c0:c0 + 1] * o_cmp[rs] + sig[:, c0 + 1:c0 + 2] * o_sel[rs] + sig[:, c0 + 2:c0 + 3] * o_win[rs]
        o_ref[:, (g * NSA_HPG + h) * NSA_HD:(g * NSA_HPG + h + 1) * NSA_HD] = o.astype(o_ref.dtype)


SEQ_TK = 256


def _lane_groups(x):
    return [x[:, i:i + LANES] for i in range(0, x.shape[1], LANES)]


def _tiled_attention(load_q, slopes, kv_ref, kcol, vcol, lo, hi, tile_mask, s_ref, m_ref, l_ref, acc_ref):
    n_heads = len(slopes)
    tq = m_ref.shape[0] // n_heads
    head_rows = [slice(h * tq, (h + 1) * tq) for h in range(n_heads)]
    m_ref[...] = jnp.full(m_ref.shape, NEG_BIG, F32)

    def scores(kt, carry):
        key0 = kt * SEQ_TK
        valid, dist = tile_mask(key0)
        k = kv_ref[pl.ds(pl.multiple_of(key0, SEQ_TK), SEQ_TK), kcol]
        for h, rs in enumerate(head_rows):
            s = jnp.where(valid, _mm_nt(load_q(h), k) * NSA_SCALE - slopes[h] * dist, NEG_BIG)
            s_ref[kt, rs] = s
            m_ref[rs] = functools.reduce(jnp.maximum, [m_ref[rs]] + _lane_groups(s))
        return carry

    lax.fori_loop(lo, hi, scores, 0)
    for rs in head_rows:
        m = jnp.maximum(jnp.max(m_ref[rs], axis=-1, keepdims=True), 0.1 * NEG_BIG)
        m_ref[rs] = jnp.broadcast_to(m, (tq, LANES))
    l_ref[...] = jnp.zeros(l_ref.shape, F32)
    acc_ref[...] = jnp.zeros(acc_ref.shape, F32)

    def values(kt, carry):
        v = kv_ref[pl.ds(pl.multiple_of(kt * SEQ_TK, SEQ_TK), SEQ_TK), vcol]
        for rs in head_rows:
            m = m_ref[rs]
            p = [jnp.exp(s - m) for s in _lane_groups(s_ref[kt, rs])]
            l_ref[rs] += sum(p[1:], p[0])
            acc_ref[rs] += _mm(jnp.concatenate(p, axis=1), v)
        return carry

    lax.fori_loop(lo, hi, values, 0)
    outs = []
    for rs in head_rows:
        l = jnp.sum(l_ref[rs], axis=-1, keepdims=True)
        outs.append(acc_ref[rs] / jnp.where(l > 0, l, 1.0))
    return jnp.concatenate(outs, axis=0)


def _nsa_seq_kernel(q_ref, gate_ref, ckv_ref, ks_ref, kw_ref, o_ref, s_ref, m_ref, l_ref, acc_ref,
                    *, tq, n_cmp, n_blk):
    j = pl.program_id(1)
    q0 = j * tq
    lane = lax.broadcasted_iota(jnp.int32, (1, SEQ_TK), 1)
    qpos_t = q0 + lax.broadcasted_iota(jnp.int32, (tq, 1), 0)
    hi = (q0 + tq - 1) // SEQ_TK + 1
    prep = []
    for g in range(NSA_GROUPS):
        qs = _stack_heads(q_ref, g)
        qpos, slope = _row_consts(tq, g, q0)
        o_cmp, p_c = _compressed_branch(qs, ckv_ref[0, 0, g], ckv_ref[0, 1, g], qpos, slope, n_cmp)
        imp = p_c[0:tq]
        for h in range(1, NSA_HPG):
            imp = imp + p_c[h * tq:(h + 1) * tq]
        prep.append((o_cmp, _select_blocks(imp, q0, n_blk)))
    for g in range(NSA_GROUPS):
        o_cmp, sel = prep[g]
        kcol = slice(g * NSA_HD, (g + 1) * NSA_HD)
        vcol = slice((NSA_GROUPS + g) * NSA_HD, (NSA_GROUPS + g + 1) * NSA_HD)
        slopes = [_alibi_slope(g, h) for h in range(NSA_HPG)]
        load_q = lambda h, g=g: q_ref[:, (g * NSA_HPG + h) * NSA_HD:(g * NSA_HPG + h + 1) * NSA_HD]
        scratch = (s_ref, m_ref, l_ref, acc_ref)

        def sel_mask(key0, sel=sel):
            dist = qpos_t - (key0 + lane)
            return (dist >= 0) & (_expand_sel(sel, key0, SEQ_TK) > 0.5), dist.astype(F32)

        o_sel = _tiled_attention(load_q, slopes, ks_ref, kcol, vcol, 0, hi, sel_mask, *scratch)

        def win_mask(key0):
            dist = qpos_t - (key0 + lane)
            return (dist >= 0) & (dist < NSA_WINDOW), dist.astype(F32)

        lo = jnp.maximum(q0 - (NSA_WINDOW - 1), 0) // SEQ_TK
        o_win = _tiled_attention(load_q, slopes, kw_ref, kcol, vcol, lo, hi, win_mask, *scratch)
        _write_gated(o_ref, gate_ref, g, tq, o_cmp, o_sel, o_win)


def _nsa_seq(proj, ckv, b, t):
    tq = 128
    n_cmp = t // NSA_BLOCK
    n_blk = -(-t // NSA_BLOCK)
    nq = t // tq
    assert t % SEQ_TK == 0 and n_blk <= LANES
    return pl.pallas_call(
        functools.partial(_nsa_seq_kernel, tq=tq, n_cmp=n_cmp, n_blk=n_blk),
        scratch_shapes=[pltpu.VMEM((t // SEQ_TK, NSA_HPG * tq, SEQ_TK), F32)]
        + [pltpu.VMEM((NSA_HPG * tq, LANES), F32)] * 3,
        grid=(b, nq),
        in_specs=[pl.BlockSpec((tq, NSA_HEADS * NSA_HD), lambda i, j: (i * nq + j, C_QA // 1024)),
                  pl.BlockSpec((tq, LANES), lambda i, j: (i * nq + j, C_GN // LANES)),
                  pl.BlockSpec((1, 2, NSA_GROUPS, n_cmp, NSA_HD), lambda i, j: (i, 0, 0, 0, 0)),
                  pl.BlockSpec((t, KV_COLS), lambda i, j: (i, C_KVS // KV_COLS)),
                  pl.BlockSpec((t, KV_COLS), lambda i, j: (i, C_KVW // KV_COLS))],
        out_specs=pl.BlockSpec((tq, NSA_HEADS * NSA_HD), lambda i, j: (i * nq + j, 0)),
        out_shape=jax.ShapeDtypeStruct((b * t, NSA_HEADS * NSA_HD), BF16),
        compiler_params=_cparams("parallel", "arbitrary"),
        name="nsa_seq",
    )(proj, proj, ckv, proj, proj)


def _pad_rows(x, rows):
    if x.shape[0] == rows:
        return x
    return jnp.concatenate([x, jnp.zeros((rows - x.shape[0], x.shape[1]), x.dtype)], axis=0)


def _two_pass_attention(qs, tiles):
    scores = []
    for k, _, valid, bias in tiles:
        scores.append(jnp.where(valid, _mm_nt(qs, k) * NSA_SCALE - bias, NEG_BIG))
    m = jnp.max(functools.reduce(jnp.maximum, scores), axis=-1, keepdims=True)
    acc = jnp.zeros((qs.shape[0], NSA_HD), F32)
    lsum = jnp.zeros(scores[0].shape, F32)
    for s, (_, v, _, _) in zip(scores, tiles):
        p = jnp.where(s > 0.5 * NEG_BIG, jnp.exp(s - m), 0.0)
        lsum = lsum + p
        acc = acc + _mm(p, v)
    l = jnp.sum(lsum, axis=-1, keepdims=True)
    return acc / jnp.where(l > 0, l, 1.0)


def _nsa_step_kernel(pt_ref, *refs, t, n_pages, page, past_len, n_win, n_cmp, n_blk):
    del pt_ref
    pages = refs[:n_pages]
    q_ref, gate_ref, ckv_ref, ksn_ref, kwn_ref, win_ref, o_ref = refs[n_pages:]
    lane = lax.broadcasted_iota(jnp.int32, (1, page), 1)
    blocks_per_page = page // NSA_BLOCK
    for g in range(NSA_GROUPS):
        qs = _stack_heads(q_ref, g)
        qpos, slope = _row_consts(t, g, past_len)
        o_cmp, p_c = _compressed_branch(qs, ckv_ref[0, 0, g], ckv_ref[0, 1, g], qpos, slope, n_cmp)
        imp = p_c[0:t]
        for h in range(1, NSA_HPG):
            imp = imp + p_c[h * t:(h + 1) * t]
        sel = _select_blocks(imp, past_len, n_blk)
        kcol = slice(g * NSA_HD, (g + 1) * NSA_HD)
        vcol = slice((NSA_GROUPS + g) * NSA_HD, (NSA_GROUPS + g + 1) * NSA_HD)
        kslot, vslot = g, NSA_GROUPS + g

        tiles = []
        for p in range(n_pages + 1):
            key0 = p * page
            if p < n_pages:
                k = pages[p][pl.ds(kslot, page, stride=KV_SLOTS), :]
                v = pages[p][pl.ds(vslot, page, stride=KV_SLOTS), :]
            else:
                k, v = _pad_rows(ksn_ref[:, kcol], page), _pad_rows(ksn_ref[:, vcol], page)
            chosen = jnp.zeros((t, page), F32)
            for bi in range(blocks_per_page):
                blk = p * blocks_per_page + bi
                chosen = jnp.where(lane // NSA_BLOCK == bi, sel[:, blk:blk + 1], chosen)
            chosen = jnp.concatenate([chosen] * NSA_HPG, axis=0)
            dist = qpos - (key0 + lane)
            tiles.append((k, v, (dist >= 0) & (chosen > 0.5), slope * dist.astype(F32)))
        o_sel = _two_pass_attention(qs, tiles)

        tiles = []
        for p in range(n_win // page + 1):
            key0 = past_len - n_win + p * page
            if p < n_win // page:
                k = win_ref[pl.ds(p * page * KV_SLOTS + kslot, page, stride=KV_SLOTS), :]
                v = win_ref[pl.ds(p * page * KV_SLOTS + vslot, page, stride=KV_SLOTS), :]
            else:
                k, v = _pad_rows(kwn_ref[:, kcol], page), _pad_rows(kwn_ref[:, vcol], page)
            dist = qpos - (key0 + lane)
            valid = (dist >= 0) & (dist < NSA_WINDOW) & (key0 + lane >= 0)
            tiles.append((k, v, valid, slope * dist.astype(F32)))
        o_win = _two_pass_attention(qs, tiles)
        _write_gated(o_ref, gate_ref, g, t, o_cmp, o_sel, o_win)


def _nsa_step(proj, ckv, cache2d, page_table_flat, win2d, b, t, n_pages, page, n_win):
    past_len = n_pages * page
    assert n_win % page == 0 and t <= page
    n_cmp = (past_len + t) // NSA_BLOCK
    n_blk = -(-(past_len + t) // NSA_BLOCK)
    assert n_cmp * NSA_BLOCK == past_len and n_blk <= LANES
    page_spec = lambda p: pl.BlockSpec((page * KV_SLOTS, NSA_HD), lambda i, pt: (pt[i * n_pages + p], 0))
    grid_spec = pltpu.PrefetchScalarGridSpec(
        num_scalar_prefetch=1,
        grid=(b,),
        in_specs=[page_spec(p) for p in range(n_pages)] + [
            pl.BlockSpec((t, NSA_HEADS * NSA_HD), lambda i, pt: (i, C_QA // 1024)),
            pl.BlockSpec((t, LANES), lambda i, pt: (i, C_GN // LANES)),
            pl.BlockSpec((1, 2, NSA_GROUPS, n_cmp, NSA_HD), lambda i, pt: (i, 0, 0, 0, 0)),
            pl.BlockSpec((t, KV_COLS), lambda i, pt: (i, C_KVS // KV_COLS)),
            pl.BlockSpec((t, KV_COLS), lambda i, pt: (i, C_KVW // KV_COLS)),
            pl.BlockSpec((n_win * KV_SLOTS, NSA_HD), lambda i, pt: (i, 0))],
        out_specs=pl.BlockSpec((t, NSA_HEADS * NSA_HD), lambda i, pt: (i, 0)),
    )
    return pl.pallas_call(
        functools.partial(_nsa_step_kernel, t=t, n_pages=n_pages, page=page, past_len=past_len,
                          n_win=n_win, n_cmp=n_cmp, n_blk=n_blk),
        grid_spec=grid_spec,
        out_shape=jax.ShapeDtypeStruct((b * t, NSA_HEADS * NSA_HD), F32),
        compiler_params=_cparams("parallel"),
        name="nsa_step",
    )(page_table_flat, *([cache2d] * n_pages), proj, proj, ckv, proj, proj, win2d)


HG_ROWS = 128


def _hgrn_kernel(q_ref, f_ref, i_ref, og_ref, lb_ref, nw_ref, s0_ref, o_ref, sfin_ref,
                 st_ref, oacc_ref, *, rows_in, n_tblk, has_state):
    tb = pl.program_id(1)

    @pl.when(tb == 0)
    def _():
        for h in range(HG_HEADS):
            if has_state:
                st_ref[h] = s0_ref[0, h].T
            else:
                st_ref[h] = jnp.zeros((HG_DV, HG_DK), F32)

    pr = HG_CHUNK if rows_in <= HG_CHUNK else HG_ROWS
    assert rows_in <= pr

    def padded(ref):
        x = ref[...]
        return x if rows_in == pr else _pad_rows(x, pr)

    def key_rows(x):
        return x if pr == HG_ROWS else _pad_rows(x, HG_ROWS)

    q, f, v = padded(q_ref), padded(f_ref), padded(i_ref)
    lb = lb_ref[...]
    row = lax.broadcasted_iota(jnp.int32, (pr, 1), 0)
    live = row < rows_in
    forget = lb + (1.0 - lb) * jax.nn.sigmoid(f)
    k = jnp.where(live, (1.0 - lb) * jax.nn.sigmoid(-f), 0.0)
    gl = jnp.where(live, jnp.log(forget), 0.0)
    rc = row % HG_CHUNK
    cum, suf = gl, gl
    s = 1
    while s < HG_CHUNK:
        cum = cum + jnp.where(rc >= s, pltpu.roll(cum, s, axis=0), 0.0)
        suf = suf + jnp.where(rc < HG_CHUNK - s, pltpu.roll(suf, pr - s, axis=0), 0.0)
        s *= 2
    ki = k * jnp.exp(-cum)
    qd = {HG_CHUNK: q * jnp.exp(cum)}
    ke = {HG_CHUNK: k * jnp.exp(suf - gl)}
    ci = lax.broadcasted_iota(jnp.int32, (pr, HG_ROWS), 0)
    cj = lax.broadcasted_iota(jnp.int32, (pr, HG_ROWS), 1)
    masks = {HG_CHUNK: (ci // HG_CHUNK == cj // HG_CHUNK) & (ci >= cj)}
    w = HG_CHUNK
    while w < pr:
        tot = cum + suf - gl
        odd = (row // w) % 2 == 1
        cum = cum + jnp.where(odd, pltpu.roll(tot, w, axis=0), 0.0)
        suf = suf + jnp.where(odd, 0.0, pltpu.roll(tot, pr - w, axis=0))
        masks[2 * w] = ((ci // w) % 2 == 1) & (cj // w == ci // w - 1)
        w *= 2
        qd[w] = q * jnp.exp(cum)
        ke[w] = k * jnp.exp(suf - gl)
    decay = jnp.exp((cum + suf - gl)[0:1, :])
    for h in range(HG_HEADS):
        hs = slice(h * HG_DK, (h + 1) * HG_DK)
        att = jnp.where(masks[HG_CHUNK], _mm_nt(qd[HG_CHUNK][:, hs], key_rows(ki[:, hs])), 0.0)
        w = HG_CHUNK
        while w < pr:
            att = att + jnp.where(masks[2 * w], _mm_nt(qd[w][:, hs], key_rows(ke[w][:, hs])), 0.0)
            w *= 2
        v_h = key_rows(v[:, hs])
        st = st_ref[h]
        oacc_ref[0:pr, hs] = _mm(att, v_h) + _mm_nt(qd[pr][:, hs], st)
        st_ref[h] = decay[:, hs] * st + _mm(v_h.T, key_rows(ke[pr][:, hs]))
    og = og_ref[...]
    nw = nw_ref[...]
    for h in range(HG_HEADS):
        hs = slice(h * HG_DV, (h + 1) * HG_DV)
        x = oacc_ref[0:rows_in, hs]
        ms = jnp.mean(x * x, axis=-1, keepdims=True)
        y = x * lax.rsqrt(ms + EPS) * nw
        o_ref[:, hs] = (y * jax.nn.silu(og[:, hs])).astype(o_ref.dtype)

    @pl.when(tb == n_tblk - 1)
    def _():
        for h in range(HG_HEADS):
            sfin_ref[0, h] = st_ref[h].T


def _hgrn(proj, lb, norm_w, s0, b, t):
    rows_in = min(t, HG_ROWS)
    n_tblk = t // rows_in
    assert rows_in * n_tblk == t and rows_in % 8 == 0
    has_state = s0 is not None
    if s0 is None:
        s0 = jnp.zeros((1, HG_HEADS, HG_DK, HG_DV), F32)
    width = HG_HEADS * HG_DK
    col = lambda c: pl.BlockSpec((rows_in, width), lambda i, j: (i * n_tblk + j, c // width))
    return pl.pallas_call(
        functools.partial(_hgrn_kernel, rows_in=rows_in, n_tblk=n_tblk, has_state=has_state),
        grid=(b, n_tblk),
        in_specs=[col(C_QB), col(C_FB), col(C_IB), col(C_OG),
                  pl.BlockSpec((1, width), lambda i, j: (0, 0)),
                  pl.BlockSpec((1, HG_DV), lambda i, j: (0, 0)),
                  pl.BlockSpec((1, HG_HEADS, HG_DK, HG_DV),
                               (lambda i, j: (i, 0, 0, 0)) if has_state else (lambda i, j: (0, 0, 0, 0)))],
        out_specs=[pl.BlockSpec((rows_in, width), lambda i, j: (i * n_tblk + j, 0)),
                   pl.BlockSpec((1, HG_HEADS, HG_DK, HG_DV), lambda i, j: (i, 0, 0, 0))],
        out_shape=[jax.ShapeDtypeStruct((b * t, width), BF16),
                   jax.ShapeDtypeStruct((b, HG_HEADS, HG_DK, HG_DV), F32)],
        scratch_shapes=[pltpu.VMEM((HG_HEADS, HG_DV, HG_DK), F32), pltpu.VMEM((HG_ROWS, width), F32)],
        compiler_params=_cparams("parallel", "arbitrary"),
        name="hgrn2",
    )(proj, proj, proj, proj, lb.reshape(1, width), norm_w.reshape(1, HG_DV), s0)


def _decoder_layer(x, past, lb, params):
    (w_in, w_a, w_b, w_o, pe, w1, w2, hg_norm, ln1, ln2, ln3, ln4, w_up, cw, cb, w_dn) = params
    b, t, d = x.shape
    m = b * t
    x2 = x.reshape(m, d)
    tm = min(MM_ROWS, m)
    proj = _in_projection(x2, ln1, *w_in, tm)
    kv_shape = (b, t, 2, NSA_GROUPS, NSA_HD)
    kvc, kvs, kvw = (a.reshape(kv_shape) for a in _kv_relayout(proj, min(CONV_ROWS, m)))
    if past is None:
        ckv = _compress_seq(proj, b, t, pe, w1, w2)
        o_a = _nsa_seq(proj, ckv, b, t)
        new_win = kvw[:, t - min(NSA_WINDOW, t):]
        s0, conv_buf = None, None
    else:
        cache_c, cache_s, page_table_flat, n_pages, page, win_buf, s0, conv_buf = past
        n_win = win_buf.shape[1]
        ckv = _compress_paged(cache_c, page_table_flat, b, n_pages, page, pe, w1, w2)
        o_a = _nsa_step(proj, ckv, cache_s, page_table_flat, win_buf.reshape(-1, NSA_HD), b, t, n_pages, page, n_win)
        new_win = jnp.concatenate([win_buf, kvw], axis=1)[:, t:]
    o_hg, s_fin = _hgrn(proj, lb, hg_norm, s0, b, t)
    mixed = _gated_merge(o_a.astype(BF16), o_hg, w_a, w_b, proj, tm, 512)
    x1 = _matmul_norm_res(mixed, w_o, x2, ln2, tm, 512)
    u = _norm_matmul(x1, ln3, w_up, tm, 512)
    if past is None:
        y = _conv_glu_down(u, cw, cb, w_dn, x1, ln4, t, min(CONV_ROWS, t), 512)
        new_conv = u.reshape(b, t, 2 * D_FF)[:, t - (CONV_W - 1):]
    else:
        act = _conv_glu_step(u.reshape(b, t, 2 * D_FF), conv_buf, cw, cb, min(64, b), 512).reshape(m, D_FF)
        new_conv = jnp.concatenate([conv_buf, u.reshape(b, t, 2 * D_FF)], axis=1)[:, t:]
        y = _matmul_norm_res(act, w_dn, x1, ln4, tm, 512)
    return y.reshape(b, t, d), (kvc, kvs, new_win, s_fin, new_conv)


def _split_w_in(w):
    o_gate = NSA_HEADS * NSA_HD + 3 * KV_COLS
    o_rest = o_gate + 3 * NSA_HEADS
    gate = jnp.pad(w[:, o_gate:o_rest], ((0, 0), (0, PROJ_TN - 3 * NSA_HEADS)))
    return w[:, :o_gate].astype(BF16), w[:, o_rest:].astype(BF16), gate.astype(BF16)


def kernel(x_prompt, x_sample, cache_cmp_kv, cache_sel_kv, state_win_kv, state_hgrn, state_conv, page_table,
           w_in, w_branch_a, w_branch_b, w_out, cmp_pe, cmp_w1, cmp_w2, hg_lb_raw, hg_norm_w, ln_mix_pre,
           ln_mix_post, ln_ffn_pre, ln_ffn_post, w_up, conv_w, conv_b, w_down):
    depth = w_in.shape[0]
    dec_b, n_pages = page_table.shape
    page = cache_cmp_kv.shape[2]
    lb_all = jnp.cumsum(jax.nn.softmax(hg_lb_raw.astype(F32), axis=0), axis=0)
    pt_flat = page_table.reshape(-1).astype(jnp.int32)
    y_p, y_s = x_prompt, x_sample
    new_p, new_s = [], []
    for l in range(depth):
        params = (_split_w_in(w_in[l]), w_branch_a[l].astype(BF16), w_branch_b[l].astype(BF16),
                  w_out[l].astype(BF16), cmp_pe[l].transpose(1, 0, 2)[:, :, None, :],
                  cmp_w1[l].astype(BF16).transpose(1, 0, 2, 3).reshape(2, NSA_BLOCK // 2, 2 * NSA_HD, NSA_HD),
                  cmp_w2[l].astype(BF16), hg_norm_w[l], ln_mix_pre[l], ln_mix_post[l], ln_ffn_pre[l],
                  ln_ffn_post[l], w_up[l].astype(BF16), conv_w[l], conv_b[l], w_down[l].astype(BF16))
        y_p, st_p = _decoder_layer(y_p, None, lb_all[l], params)
        past = (cache_cmp_kv[l].reshape(-1, NSA_HD), cache_sel_kv[l].reshape(-1, NSA_HD),
                pt_flat, n_pages, page, state_win_kv[l], state_hgrn[l], state_conv[l])
        y_s, st_s = _decoder_layer(y_s, past, lb_all[l], params)
        new_p.append(st_p)
        new_s.append(st_s)

    def stack(group, i):
        return jnp.stack([st[i] for st in group], axis=0)

    return (y_p, y_s, stack(new_p, 0), stack(new_s, 0), stack(new_p, 1), stack(new_s, 1), stack(new_p, 2),
            stack(new_s, 2), stack(new_p, 3), stack(new_s, 3), stack(new_p, 4), stack(new_s, 4))
```

```python
import functools

import jax
import jax.numpy as jnp
from jax import lax
from jax.experimental import pallas as pl
from jax.experimental.pallas import tpu as pltpu

F32 = jnp.float32
BF16 = jnp.bfloat16

D_MODEL = 2048
NSA_HEADS = 8
NSA_GROUPS = 2
NSA_HPG = NSA_HEADS // NSA_GROUPS
NSA_HD = 128
NSA_BLOCK = 64
NSA_TOPN = 8
NSA_WINDOW = 512
NSA_SCALE = NSA_HD ** -0.5
FORCED_SCORE = NSA_HPG + 1.0
HG_HEADS = 8
HG_DK = 128
HG_DV = 128
HG_CHUNK = 16
D_FF = 5632
CONV_W = 3
EPS = 1e-6
KV_COLS = 2 * NSA_GROUPS * NSA_HD

C_QA = 0
C_QB = 1024
C_FB = 2048
C_IB = 3072
C_OG = 4096
C_GA = 5120
C_GB = 7168
C_KVC = 9216
C_KVS = 9728
C_KVW = 10240
C_GN = 10752

LANES = 128
MM_ROWS = 1024
CONV_ROWS = 512
DOWN_PARTS = 8
NEG_BIG = -1e30
VMEM_LIMIT = 56 * 1024 * 1024


def _cparams(*sem):
    return pltpu.CompilerParams(dimension_semantics=sem, vmem_limit_bytes=VMEM_LIMIT)


def _mm(a, b):
    return jnp.dot(a.astype(BF16), b.astype(BF16), preferred_element_type=F32)


def _mm_nt(a, b):
    return lax.dot_general(a.astype(BF16), b.astype(BF16), (((1,), (1,)), ((), ())),
                           preferred_element_type=F32)


def _norm_mm_kernel(x_ref, ln_ref, w_ref, o_ref, h_ref):
    @pl.when(pl.program_id(1) == 0)
    def _():
        x = x_ref[...]
        ms = jnp.mean(x * x, axis=-1, keepdims=True)
        h_ref[...] = (x * lax.rsqrt(ms + EPS) * ln_ref[...]).astype(BF16)

    o_ref[...] = _mm(h_ref[...], w_ref[...])


def _norm_matmul(x, ln, w, tm, tn):
    m, k = x.shape
    n = w.shape[1]
    return pl.pallas_call(
        _norm_mm_kernel,
        grid=(m // tm, n // tn),
        in_specs=[pl.BlockSpec((tm, k), lambda i, j: (i, 0)),
                  pl.BlockSpec((1, k), lambda i, j: (0, 0)),
                  pl.BlockSpec((k, tn), lambda i, j: (0, j))],
        out_specs=pl.BlockSpec((tm, tn), lambda i, j: (i, j)),
        out_shape=jax.ShapeDtypeStruct((m, n), F32),
        scratch_shapes=[pltpu.VMEM((tm, k), BF16)],
        compiler_params=_cparams("parallel", "arbitrary"),
        name="norm_matmul",
    )(x, ln.reshape(1, k), w)


PROJ_TN = 512
TILES_Q = (C_QB - C_QA) // PROJ_TN
TILE_REST0 = C_QB // PROJ_TN
TILE_KV0 = C_KVC // PROJ_TN
TILE_GATE = C_GN // PROJ_TN
N_PROJ = (TILE_GATE + 1) * PROJ_TN


def _in_proj_kernel(x_ref, ln_ref, wh_ref, wr_ref, wg_ref, o_ref, h_ref):
    j = pl.program_id(1)

    @pl.when(j == 0)
    def _():
        x = x_ref[...]
        ms = jnp.mean(x * x, axis=-1, keepdims=True)
        h_ref[...] = (x * lax.rsqrt(ms + EPS) * ln_ref[...]).astype(BF16)

    from_rest = (j >= TILE_REST0) & (j < TILE_KV0)
    from_gate = j == TILE_GATE

    @pl.when(from_rest)
    def _():
        o_ref[...] = _mm(h_ref[...], wr_ref[...])

    @pl.when(from_gate)
    def _():
        o_ref[...] = _mm(h_ref[...], wg_ref[...])

    @pl.when(jnp.logical_not(from_rest | from_gate))
    def _():
        o_ref[...] = _mm(h_ref[...], wh_ref[...])


def _in_projection(x, ln, w_head, w_rest, w_gate, tm):
    m, k = x.shape
    tn = PROJ_TN
    n_head = w_head.shape[1] // tn
    n_rest = w_rest.shape[1] // tn
    assert n_head == TILES_Q + TILE_GATE - TILE_KV0 and n_rest == TILE_KV0 - TILE_REST0 and w_gate.shape[1] == tn
    head_tile = lambda j: jnp.where(j < TILES_Q, j, jnp.clip(j - (TILE_KV0 - TILES_Q), TILES_Q - 1, n_head - 1))
    rest_tile = lambda j: jnp.clip(j - TILE_REST0, 0, n_rest - 1)
    return pl.pallas_call(
        _in_proj_kernel,
        grid=(m // tm, N_PROJ // tn),
        in_specs=[pl.BlockSpec((tm, k), lambda i, j: (i, 0)),
                  pl.BlockSpec((1, k), lambda i, j: (0, 0)),
                  pl.BlockSpec((k, tn), lambda i, j: (0, head_tile(j))),
                  pl.BlockSpec((k, tn), lambda i, j: (0, rest_tile(j))),
                  pl.BlockSpec((k, tn), lambda i, j: (0, 0))],
        out_specs=pl.BlockSpec((tm, tn), lambda i, j: (i, j)),
        out_shape=jax.ShapeDtypeStruct((m, N_PROJ), F32),
        scratch_shapes=[pltpu.VMEM((tm, k), BF16)],
        compiler_params=_cparams("parallel", "arbitrary"),
        name="in_projection",
    )(x, ln.reshape(1, k), w_head, w_rest, w_gate)


def _kv_relayout_kernel(c_ref, s_ref, w_ref, oc_ref, os_ref, ow_ref, *, tm):
    for x_ref, o_ref in ((c_ref, oc_ref), (s_ref, os_ref), (w_ref, ow_ref)):
        for slot in range(2 * NSA_GROUPS):
            o_ref[pl.ds(slot, tm, stride=2 * NSA_GROUPS), :] = x_ref[:, slot * NSA_HD:(slot + 1) * NSA_HD]


def _kv_relayout(proj, tm):
    m = proj.shape[0]
    slots = 2 * NSA_GROUPS
    out = jax.ShapeDtypeStruct((m * slots, NSA_HD), F32)
    return pl.pallas_call(
        functools.partial(_kv_relayout_kernel, tm=tm),
        grid=(m // tm,),
        in_specs=[pl.BlockSpec((tm, KV_COLS), lambda i, c=c: (i, c // KV_COLS)) for c in (C_KVC, C_KVS, C_KVW)],
        out_specs=[pl.BlockSpec((tm * slots, NSA_HD), lambda i: (i, 0))] * 3,
        out_shape=[out] * 3,
        compiler_params=_cparams("parallel"),
        name="kv_relayout",
    )(proj, proj, proj)


def _mm_norm_res_kernel(a_ref, w_ref, res_ref, ln_ref, o_ref, acc_ref, *, nk):
    k = pl.program_id(1)

    @pl.when(k == 0)
    def _():
        acc_ref[...] = jnp.zeros_like(acc_ref)

    acc_ref[...] += _mm(a_ref[...], w_ref[...])

    @pl.when(k == nk - 1)
    def _():
        y = acc_ref[...]
        ms = jnp.mean(y * y, axis=-1, keepdims=True)
        o_ref[...] = res_ref[...] + y * lax.rsqrt(ms + EPS) * ln_ref[...]


def _matmul_norm_res(a, w, res, ln, tm, tk):
    m, kk = a.shape
    n = w.shape[1]
    nk = kk // tk
    return pl.pallas_call(
        functools.partial(_mm_norm_res_kernel, nk=nk),
        grid=(m // tm, nk),
        in_specs=[pl.BlockSpec((tm, tk), lambda i, k: (i, k)),
                  pl.BlockSpec((tk, n), lambda i, k: (k, 0)),
                  pl.BlockSpec((tm, n), lambda i, k: (i, 0)),
                  pl.BlockSpec((1, n), lambda i, k: (0, 0))],
        out_specs=pl.BlockSpec((tm, n), lambda i, k: (i, 0)),
        out_shape=jax.ShapeDtypeStruct((m, n), F32),
        scratch_shapes=[pltpu.VMEM((tm, n), F32)],
        compiler_params=_cparams("parallel", "arbitrary"),
        name="matmul_norm_res",
    )(a, w, res, ln.reshape(1, n))


def _merge_kernel(oa_ref, ohg_ref, wa_ref, wb_ref, ga_ref, gb_ref, o_ref):
    a = _mm(oa_ref[...], wa_ref[...])
    b = _mm(ohg_ref[...], wb_ref[...])
    o_ref[...] = (jax.nn.sigmoid(ga_ref[...]) * a + jax.nn.sigmoid(gb_ref[...]) * b).astype(BF16)


def _gated_merge(o_a, o_hg, w_a, w_b, proj, tm, tn):
    m, k = o_a.shape
    n = w_a.shape[1]
    return pl.pallas_call(
        _merge_kernel,
        grid=(m // tm, n // tn),
        in_specs=[pl.BlockSpec((tm, k), lambda i, j: (i, 0)),
                  pl.BlockSpec((tm, k), lambda i, j: (i, 0)),
                  pl.BlockSpec((k, tn), lambda i, j: (0, j)),
                  pl.BlockSpec((k, tn), lambda i, j: (0, j)),
                  pl.BlockSpec((tm, tn), lambda i, j: (i, C_GA // tn + j)),
                  pl.BlockSpec((tm, tn), lambda i, j: (i, C_GB // tn + j))],
        out_specs=pl.BlockSpec((tm, tn), lambda i, j: (i, j)),
        out_shape=jax.ShapeDtypeStruct((m, n), BF16),
        compiler_params=_cparams("parallel", "arbitrary"),
        name="gated_merge",
    )(o_a, o_hg, w_a, w_b, proj, proj)


def _conv_taps(fa, fg, cwa_ref, cwg_ref, cba_ref, cbg_ref, tap):
    ca = cba_ref[...]
    cg = cbg_ref[...]
    for j in range(CONV_W):
        ca = ca + tap(fa, j) * cwa_ref[j:j + 1, :]
        cg = cg + tap(fg, j) * cwg_ref[j:j + 1, :]
    return jax.nn.gelu(cg, approximate=True) * ca


def _conv_glu_down_kernel(ua_ref, ug_ref, ha_ref, hg_ref, cwa_ref, cwg_ref, cba_ref, cbg_ref, w_ref, res_ref,
                          ln_ref, o_ref, fa_ref, fg_ref, act_ref, acc_ref, *, tm, tiles_per_seq, nk):
    k = pl.program_id(1)
    slot = k % 2

    @pl.when(k == 0)
    def _():
        acc_ref[...] = jnp.zeros_like(acc_ref)
        act_ref[1] = jnp.zeros(act_ref.shape[1:], act_ref.dtype)

    first = (pl.program_id(0) % tiles_per_seq) == 0
    fa_ref[0:8, :] = jnp.where(first, 0.0, ha_ref[...])
    fg_ref[0:8, :] = jnp.where(first, 0.0, hg_ref[...])
    fa_ref[8:, :] = ua_ref[...]
    fg_ref[8:, :] = ug_ref[...]
    n_parts = DOWN_PARTS
    cols = acc_ref.shape[1] // n_parts
    rows = tm // n_parts
    for part in range(n_parts):
        cs = slice(part * cols, (part + 1) * cols)
        acc_ref[:, cs] += _mm(act_ref[1 - slot], w_ref[:, cs])
        r0 = part * rows
        xa, xg = fa_ref[r0:r0 + rows + 8, :], fg_ref[r0:r0 + rows + 8, :]
        tap = lambda x, j: (x if j == CONV_W - 1 else pltpu.roll(x, CONV_W - 1 - j, axis=0))[8:]
        act = _conv_taps(xa, xg, cwa_ref, cwg_ref, cba_ref, cbg_ref, tap)
        act_ref[slot, r0:r0 + rows, :] = act.astype(act_ref.dtype)

    @pl.when(k == nk)
    def _():
        y = acc_ref[...]
        ms = jnp.mean(y * y, axis=-1, keepdims=True)
        o_ref[...] = res_ref[...] + y * lax.rsqrt(ms + EPS) * ln_ref[...]


def _conv_glu_down(u, conv_w, conv_b, w, res, ln, seq_len, tm, tk):
    m = u.shape[0]
    n = w.shape[1]
    nk = D_FF // tk
    hb = tm // 8
    kc = lambda k: jnp.minimum(k, nk - 1)
    kw = lambda k: jnp.maximum(k - 1, 0)
    halo = lambda i, k, off: (jnp.maximum(i * hb - 1, 0), kc(k) + off)
    return pl.pallas_call(
        functools.partial(_conv_glu_down_kernel, tm=tm, tiles_per_seq=seq_len // tm, nk=nk),
        grid=(m // tm, nk + 1),
        in_specs=[pl.BlockSpec((tm, tk), lambda i, k: (i, kc(k))),
                  pl.BlockSpec((tm, tk), lambda i, k: (i, kc(k) + nk)),
                  pl.BlockSpec((8, tk), lambda i, k: halo(i, k, 0)),
                  pl.BlockSpec((8, tk), lambda i, k: halo(i, k, nk)),
                  pl.BlockSpec((CONV_W, tk), lambda i, k: (0, kc(k))),
                  pl.BlockSpec((CONV_W, tk), lambda i, k: (0, kc(k) + nk)),
                  pl.BlockSpec((1, tk), lambda i, k: (0, kc(k))),
                  pl.BlockSpec((1, tk), lambda i, k: (0, kc(k) + nk)),
                  pl.BlockSpec((tk, n), lambda i, k: (kw(k), 0)),
                  pl.BlockSpec((tm, n), lambda i, k: (i, 0)),
                  pl.BlockSpec((1, n), lambda i, k: (0, 0))],
        out_specs=pl.BlockSpec((tm, n), lambda i, k: (i, 0)),
        out_shape=jax.ShapeDtypeStruct((m, n), F32),
        scratch_shapes=[pltpu.VMEM((tm + 8, tk), F32), pltpu.VMEM((tm + 8, tk), F32),
                        pltpu.VMEM((2, tm, tk), BF16), pltpu.VMEM((tm, n), F32)],
        compiler_params=_cparams("parallel", "arbitrary"),
        name="conv_glu_down",
    )(u, u, u, u, conv_w, conv_w, conv_b.reshape(1, -1), conv_b.reshape(1, -1), w, res, ln.reshape(1, n))


def _conv_glu_step_kernel(ua_ref, ug_ref, ba_ref, bg_ref, cwa_ref, cwg_ref, cba_ref, cbg_ref,
                          o_ref, fa_ref, fg_ref, *, t):
    fa_ref[:, 8 - (CONV_W - 1):8, :] = ba_ref[...]
    fg_ref[:, 8 - (CONV_W - 1):8, :] = bg_ref[...]
    fa_ref[:, 8:, :] = ua_ref[...]
    fg_ref[:, 8:, :] = ug_ref[...]
    tap = lambda f, j: f[:, 8 - (CONV_W - 1) + j:8 - (CONV_W - 1) + j + t, :]
    o_ref[...] = _conv_taps(fa_ref, fg_ref, cwa_ref, cwg_ref, cba_ref, cbg_ref, tap)


def _conv_glu_step(u3, buf, conv_w, conv_b, nb, tn):
    b, t, _ = u3.shape
    nj = D_FF // tn
    return pl.pallas_call(
        functools.partial(_conv_glu_step_kernel, t=t),
        grid=(b // nb, nj),
        in_specs=[pl.BlockSpec((nb, t, tn), lambda i, j: (i, 0, j)),
                  pl.BlockSpec((nb, t, tn), lambda i, j: (i, 0, j + nj)),
                  pl.BlockSpec((nb, CONV_W - 1, tn), lambda i, j: (i, 0, j)),
                  pl.BlockSpec((nb, CONV_W - 1, tn), lambda i, j: (i, 0, j + nj)),
                  pl.BlockSpec((CONV_W, tn), lambda i, j: (0, j)),
                  pl.BlockSpec((CONV_W, tn), lambda i, j: (0, j + nj)),
                  pl.BlockSpec((1, tn), lambda i, j: (0, j)),
                  pl.BlockSpec((1, tn), lambda i, j: (0, j + nj))],
        out_specs=pl.BlockSpec((nb, t, tn), lambda i, j: (i, 0, j)),
        out_shape=jax.ShapeDtypeStruct((b, t, D_FF), F32),
        scratch_shapes=[pltpu.VMEM((nb, 8 + t, tn), F32), pltpu.VMEM((nb, 8 + t, tn), F32)],
        compiler_params=_cparams("parallel", "arbitrary"),
        name="conv_glu_step",
    )(u3, u3, buf, buf, conv_w, conv_w, conv_b.reshape(1, -1), conv_b.reshape(1, -1))


def _compress_body(load_x, pe_ref, w1_ref, w2_ref, o_ref, n_seq, n_cmp):
    for c in range(2):
        acc = jnp.zeros((NSA_GROUPS * n_seq * n_cmp, NSA_HD), F32)
        for l in range(0, NSA_BLOCK, 2):
            halves = [jnp.concatenate([load_x(l + dl, c, g) for g in range(NSA_GROUPS)], axis=0) + pe_ref[c, l + dl]
                      for dl in range(2)]
            acc = acc + _mm(jnp.concatenate(halves, axis=1), w1_ref[c, l // 2])
        out = _mm(jax.nn.silu(acc), w2_ref[c])
        for g in range(NSA_GROUPS):
            for s in range(n_seq):
                r0 = (g * n_seq + s) * n_cmp
                o_ref[s, c, g] = out[r0:r0 + n_cmp]


def _compress_seq_kernel(x0_ref, x1_ref, x2_ref, x3_ref, pe_ref, w1_ref, w2_ref, o_ref, *, n_cmp):
    xs = (x0_ref, x1_ref, x2_ref, x3_ref)
    load_x = lambda l, c, g: xs[c * NSA_GROUPS + g][pl.ds(l, n_cmp, stride=NSA_BLOCK), :]
    _compress_body(load_x, pe_ref, w1_ref, w2_ref, o_ref, 1, n_cmp)


def _compress_seq(proj, b, t, pe, w1, w2):
    n_cmp = t // NSA_BLOCK
    rows = n_cmp * NSA_BLOCK
    assert rows == t
    return pl.pallas_call(
        functools.partial(_compress_seq_kernel, n_cmp=n_cmp),
        grid=(b,),
        in_specs=[pl.BlockSpec((rows, NSA_HD), lambda i, cg=cg: (i, C_KVC // NSA_HD + cg))
                  for cg in range(2 * NSA_GROUPS)] + [
                  pl.BlockSpec(pe.shape, lambda i: (0, 0, 0, 0)),
                  pl.BlockSpec(w1.shape, lambda i: (0, 0, 0, 0)),
                  pl.BlockSpec(w2.shape, lambda i: (0, 0, 0))],
        out_specs=pl.BlockSpec((1, 2, NSA_GROUPS, n_cmp, NSA_HD), lambda i: (i, 0, 0, 0, 0)),
        out_shape=jax.ShapeDtypeStruct((b, 2, NSA_GROUPS, n_cmp, NSA_HD), F32),
        compiler_params=_cparams("parallel"),
        name="compress_seq",
    )(proj, proj, proj, proj, pe, w1, w2)


KV_SLOTS = 2 * NSA_GROUPS
CMP_SEQS = 2
CMP_PITCH = NSA_BLOCK * KV_SLOTS + 8


def _compress_paged_kernel(pt_ref, *refs, n_seq, n_pages, page_rows):
    del pt_ref
    pages = refs[:n_seq * n_pages]
    pe_ref, w1_ref, w2_ref, o_ref, x_ref = refs[n_seq * n_pages:]
    blk_rows = NSA_BLOCK * KV_SLOTS
    per_page = page_rows // blk_rows
    for i, pg in enumerate(pages):
        for j in range(per_page):
            n = i * per_page + j
            x_ref[n * CMP_PITCH:n * CMP_PITCH + blk_rows, :] = pg[j * blk_rows:(j + 1) * blk_rows, :]
    n_cmp = n_pages * per_page
    load_x = lambda l, c, g: x_ref[pl.ds(l * KV_SLOTS + c * NSA_GROUPS + g, n_seq * n_cmp, stride=CMP_PITCH), :]
    _compress_body(load_x, pe_ref, w1_ref, w2_ref, o_ref, n_seq, n_cmp)


def _compress_paged(cache2d, page_table_flat, b, n_pages, page, pe, w1, w2):
    page_rows = page * KV_SLOTS
    n_cmp = n_pages * page // NSA_BLOCK
    n_seq = CMP_SEQS if b % CMP_SEQS == 0 else 1
    page_spec = lambda s, p: pl.BlockSpec(
        (page_rows, NSA_HD), lambda i, pt: (pt[(i * n_seq + s) * n_pages + p], 0))
    grid_spec = pltpu.PrefetchScalarGridSpec(
        num_scalar_prefetch=1,
        grid=(b // n_seq,),
        in_specs=[page_spec(s, p) for s in range(n_seq) for p in range(n_pages)] + [
            pl.BlockSpec(pe.shape, lambda i, pt: (0, 0, 0, 0)),
            pl.BlockSpec(w1.shape, lambda i, pt: (0, 0, 0, 0)),
            pl.BlockSpec(w2.shape, lambda i, pt: (0, 0, 0))],
        out_specs=pl.BlockSpec((n_seq, 2, NSA_GROUPS, n_cmp, NSA_HD), lambda i, pt: (i, 0, 0, 0, 0)),
        scratch_shapes=[pltpu.VMEM((n_seq * n_cmp * CMP_PITCH, NSA_HD), F32)],
    )
    return pl.pallas_call(
        functools.partial(_compress_paged_kernel, n_seq=n_seq, n_pages=n_pages, page_rows=page_rows),
        grid_spec=grid_spec,
        out_shape=jax.ShapeDtypeStruct((b, 2, NSA_GROUPS, n_cmp, NSA_HD), F32),
        compiler_params=_cparams("parallel"),
        name="compress_paged",
    )(page_table_flat, *([cache2d] * (n_seq * n_pages)), pe, w1, w2)


def _alibi_slope(g, h):
    return 2.0 ** (-8.0 * (g * NSA_HPG + h + 1.0) / NSA_HEADS)


def _row_consts(tq, g, q_pos0):
    rows = NSA_HPG * tq
    r = lax.broadcasted_iota(jnp.int32, (rows, 1), 0)
    head = r // tq
    qpos = q_pos0 + (r - head * tq)
    slope = jnp.zeros((rows, 1), F32)
    for h in range(NSA_HPG):
        slope = jnp.where(head == h, _alibi_slope(g, h), slope)
    return qpos, slope


def _stack_heads(q_ref, g):
    return jnp.concatenate(
        [q_ref[:, (g * NSA_HPG + h) * NSA_HD:(g * NSA_HPG + h + 1) * NSA_HD] for h in range(NSA_HPG)], axis=0)


def _compressed_branch(qs, ck, cv, qpos, slope, n_cmp):
    pad = jnp.zeros((LANES - n_cmp, NSA_HD), F32)
    ckp = jnp.concatenate([ck, pad], axis=0)
    cvp = jnp.concatenate([cv, pad], axis=0)
    n = lax.broadcasted_iota(jnp.int32, (1, LANES), 1)
    dist = qpos - ((n + 1) * NSA_BLOCK - 1)
    s = _mm_nt(qs, ckp) * NSA_SCALE - slope * dist.astype(F32)
    mask = (dist >= 0) & (n < n_cmp)
    s = jnp.where(mask, s, NEG_BIG)
    m = jnp.max(s, axis=-1, keepdims=True)
    e = jnp.where(mask, jnp.exp(s - m), 0.0)
    d = jnp.sum(e, axis=-1, keepdims=True)
    p = e / jnp.where(d > 0, d, 1.0)
    return _mm(p, cvp), p


def _select_blocks(imp, q_pos0, n_blk):
    tq = imp.shape[0]
    nb = -(-n_blk // 8) * 8
    imp_t = (imp if tq == LANES else _pad_rows(imp, LANES)).T[0:nb]
    blk = lax.broadcasted_iota(jnp.int32, (nb, LANES), 0)
    cur = (q_pos0 + lax.broadcasted_iota(jnp.int32, (nb, LANES), 1)) // NSA_BLOCK
    forced = (blk == 0) | (blk == cur) | (blk == cur - 1)
    valid = blk <= cur
    score = jnp.where(valid, jnp.where(forced, FORCED_SCORE, imp_t), -1.0)
    score = jnp.where(blk < n_blk, score, -2.0)
    blk_f = blk.astype(F32)
    sel = jnp.zeros((nb, LANES), F32)
    for _ in range(min(NSA_TOPN, n_blk)):
        mx = jnp.max(score, axis=0, keepdims=True)
        first = jnp.min(jnp.where(score == mx, blk_f, 1e9), axis=0, keepdims=True)
        hit = blk_f == first
        sel = jnp.where(hit, 1.0, sel)
        score = jnp.where(hit, -3.0, score)
    return _pad_rows(sel, LANES).T[0:tq]


def _expand_sel(sel, key0, nkeys):
    bi = lax.broadcasted_iota(jnp.int32, (LANES, nkeys), 0)
    ki = lax.broadcasted_iota(jnp.int32, (LANES, nkeys), 1)
    expand = (bi == (key0 + ki) // NSA_BLOCK).astype(BF16)
    return _mm(sel, expand)


def _write_gated(o_ref, gate_ref, g, tq, o_cmp, o_sel, o_win):
    sig = jax.nn.sigmoid(gate_ref[...])
    for h in range(NSA_HPG):
        c0 = (g * NSA_HPG + h) * 3
        rs = slice(h * tq, (h + 1) * tq)
        o = sig[:, c0:c0 + 1] * o_cmp[rs] + sig[:, c0 + 1:c0 + 2] * o_sel[rs] + sig[:, c0 + 2:c0 + 3] * o_win[rs]
        o_ref[:, (g * NSA_HPG + h) * NSA_HD:(g * NSA_HPG + h + 1) * NSA_HD] = o.astype(o_ref.dtype)


SEQ_TK = 256


def _lane_groups(x):
    return [x[:, i:i + LANES] for i in range(0, x.shape[1], LANES)]


def _tiled_attention(heads, kv_ref, lo, hi, tile_masks, s_ref, m_ref, l_ref, acc_ref):
    tq = m_ref.shape[0] // len(heads)
    head_rows = [slice(i * tq, (i + 1) * tq) for i in range(len(heads))]
    groups = sorted({g for _, _, g in heads})
    kcol = lambda g: slice(g * NSA_HD, (g + 1) * NSA_HD)
    vcol = lambda g: slice((NSA_GROUPS + g) * NSA_HD, (NSA_GROUPS + g + 1) * NSA_HD)
    m_ref[...] = jnp.full(m_ref.shape, NEG_BIG, F32)

    def scores(kt, carry):
        key0 = kt * SEQ_TK
        masks = tile_masks(key0)
        rows = pl.ds(pl.multiple_of(key0, SEQ_TK), SEQ_TK)
        k = {g: kv_ref[rows, kcol(g)] for g in groups}
        for (load_q, slope, g), rs in zip(heads, head_rows):
            valid, dist = masks[g]
            s = jnp.where(valid, _mm_nt(load_q(), k[g]) * NSA_SCALE - slope * dist, NEG_BIG)
            s_ref[kt, rs] = s
            m_ref[rs] = functools.reduce(jnp.maximum, [m_ref[rs]] + _lane_groups(s))
        return carry

    lax.fori_loop(lo, hi, scores, 0)
    for rs in head_rows:
        m = jnp.maximum(jnp.max(m_ref[rs], axis=-1, keepdims=True), 0.1 * NEG_BIG)
        m_ref[rs] = jnp.broadcast_to(m, (tq, LANES))
    l_ref[...] = jnp.zeros(l_ref.shape, F32)
    acc_ref[...] = jnp.zeros(acc_ref.shape, F32)

    def values(kt, carry):
        rows = pl.ds(pl.multiple_of(kt * SEQ_TK, SEQ_TK), SEQ_TK)
        v = {g: kv_ref[rows, vcol(g)] for g in groups}
        for (_, _, g), rs in zip(heads, head_rows):
            m = m_ref[rs]
            p = [jnp.exp(s - m) for s in _lane_groups(s_ref[kt, rs])]
            l_ref[rs] += sum(p[1:], p[0])
            acc_ref[rs] += _mm(jnp.concatenate(p, axis=1), v[g])
        return carry

    lax.fori_loop(lo, hi, values, 0)
    outs = []
    for rs in head_rows:
        l = jnp.sum(l_ref[rs], axis=-1, keepdims=True)
        outs.append(acc_ref[rs] / jnp.where(l > 0, l, 1.0))
    return jnp.concatenate(outs, axis=0)


def _nsa_seq_kernel(q_ref, gate_ref, ckv_ref, ks_ref, kw_ref, o_ref, s_ref, m_ref, l_ref, acc_ref,
                    *, tq, n_cmp, n_blk):
    j = pl.program_id(1)
    q0 = j * tq
    lane = lax.broadcasted_iota(jnp.int32, (1, SEQ_TK), 1)
    qpos_t = q0 + lax.broadcasted_iota(jnp.int32, (tq, 1), 0)
    hi = (q0 + tq - 1) // SEQ_TK + 1
    o_cmp, sel = [], []
    for g in range(NSA_GROUPS):
        qs = _stack_heads(q_ref, g)
        qpos, slope = _row_consts(tq, g, q0)
        o_g, p_c = _compressed_branch(qs, ckv_ref[0, 0, g], ckv_ref[0, 1, g], qpos, slope, n_cmp)
        imp = p_c[0:tq]
        for h in range(1, NSA_HPG):
            imp = imp + p_c[h * tq:(h + 1) * tq]
        o_cmp.append(o_g)
        sel.append(_select_blocks(imp, q0, n_blk))
    order = [(g, h) for h in range(NSA_HPG) for g in range(NSA_GROUPS)]
    heads = [(lambda c=(g * NSA_HPG + h) * NSA_HD: q_ref[:, c:c + NSA_HD], _alibi_slope(g, h), g) for g, h in order]
    scratch = (s_ref, m_ref, l_ref, acc_ref)

    def sel_masks(key0):
        dist = qpos_t - (key0 + lane)
        dist_f = dist.astype(F32)
        return [((dist >= 0) & (_expand_sel(sel[g], key0, SEQ_TK) > 0.5), dist_f) for g in range(NSA_GROUPS)]

    o_sel = _tiled_attention(heads, ks_ref, 0, hi, sel_masks, *scratch)

    def win_masks(key0):
        dist = qpos_t - (key0 + lane)
        return [((dist >= 0) & (dist < NSA_WINDOW), dist.astype(F32))] * NSA_GROUPS

    lo = jnp.maximum(q0 - (NSA_WINDOW - 1), 0) // SEQ_TK
    o_win = _tiled_attention(heads, kw_ref, lo, hi, win_masks, *scratch)
    sig = jax.nn.sigmoid(gate_ref[...])
    for i, (g, h) in enumerate(order):
        c0 = (g * NSA_HPG + h) * 3
        rs = slice(i * tq, (i + 1) * tq)
        o = (sig[:, c0:c0 + 1] * o_cmp[g][h * tq:(h + 1) * tq] + sig[:, c0 + 1:c0 + 2] * o_sel[rs]
             + sig[:, c0 + 2:c0 + 3] * o_win[rs])
        o_ref[:, (g * NSA_HPG + h) * NSA_HD:(g * NSA_HPG + h + 1) * NSA_HD] = o.astype(o_ref.dtype)


def _nsa_seq(proj, ckv, b, t):
    tq = 128
    n_cmp = t // NSA_BLOCK
    n_blk = -(-t // NSA_BLOCK)
    nq = t // tq
    assert t % SEQ_TK == 0 and n_blk <= LANES
    return pl.pallas_call(
        functools.partial(_nsa_seq_kernel, tq=tq, n_cmp=n_cmp, n_blk=n_blk),
        scratch_shapes=[pltpu.VMEM((t // SEQ_TK, NSA_HEADS * tq, SEQ_TK), F32)]
        + [pltpu.VMEM((NSA_HEADS * tq, LANES), F32)] * 3,
        grid=(b, nq),
        in_specs=[pl.BlockSpec((tq, NSA_HEADS * NSA_HD), lambda i, j: (i * nq + j, C_QA // 1024)),
                  pl.BlockSpec((tq, LANES), lambda i, j: (i * nq + j, C_GN // LANES)),
                  pl.BlockSpec((1, 2, NSA_GROUPS, n_cmp, NSA_HD), lambda i, j: (i, 0, 0, 0, 0)),
                  pl.BlockSpec((t, KV_COLS), lambda i, j: (i, C_KVS // KV_COLS)),
                  pl.BlockSpec((t, KV_COLS), lambda i, j: (i, C_KVW // KV_COLS))],
        out_specs=pl.BlockSpec((tq, NSA_HEADS * NSA_HD), lambda i, j: (i * nq + j, 0)),
        out_shape=jax.ShapeDtypeStruct((b * t, NSA_HEADS * NSA_HD), BF16),
        compiler_params=_cparams("parallel", "arbitrary"),
        name="nsa_seq",
    )(proj, proj, ckv, proj, proj)


def _pad_rows(x, rows):
    if x.shape[0] == rows:
        return x
    return jnp.concatenate([x, jnp.zeros((rows - x.shape[0], x.shape[1]), x.dtype)], axis=0)


def _two_pass_attention(qs, tiles):
    scores = []
    for k, _, valid, bias in tiles:
        scores.append(jnp.where(valid, _mm_nt(qs, k) * NSA_SCALE - bias, NEG_BIG))
    m = jnp.max(functools.reduce(jnp.maximum, scores), axis=-1, keepdims=True)
    acc = jnp.zeros((qs.shape[0], NSA_HD), F32)
    lsum = jnp.zeros(scores[0].shape, F32)
    for s, (_, v, _, _) in zip(scores, tiles):
        p = jnp.where(s > 0.5 * NEG_BIG, jnp.exp(s - m), 0.0)
        lsum = lsum + p
        acc = acc + _mm(p, v)
    l = jnp.sum(lsum, axis=-1, keepdims=True)
    return acc / jnp.where(l > 0, l, 1.0)


STEP_SEQS = 2


def _nsa_step_kernel(pt_ref, *refs, n_seq, t, n_pages, n_win, **kw):
    del pt_ref
    q_ref, gate_ref, ckv_ref, ksn_ref, kwn_ref, win_ref, o_ref = refs[n_seq * n_pages:]
    win_rows = n_win * KV_SLOTS
    for s in range(n_seq):
        rows = pl.ds(s * t, t)
        _nsa_step_one(refs[s * n_pages:(s + 1) * n_pages], q_ref.at[rows], gate_ref.at[rows], ckv_ref.at[pl.ds(s, 1)],
                      ksn_ref.at[rows], kwn_ref.at[rows], win_ref.at[pl.ds(s * win_rows, win_rows)], o_ref.at[rows],
                      t=t, n_pages=n_pages, n_win=n_win, **kw)


def _nsa_step_one(pages, q_ref, gate_ref, ckv_ref, ksn_ref, kwn_ref, win_ref, o_ref,
                  *, t, n_pages, page, past_len, n_win, n_cmp, n_blk):
    lane = lax.broadcasted_iota(jnp.int32, (1, page), 1)
    blocks_per_page = page // NSA_BLOCK
    for g in range(NSA_GROUPS):
        qs = _stack_heads(q_ref, g)
        qpos, slope = _row_consts(t, g, past_len)
        o_cmp, p_c = _compressed_branch(qs, ckv_ref[0, 0, g], ckv_ref[0, 1, g], qpos, slope, n_cmp)
        imp = p_c[0:t]
        for h in range(1, NSA_HPG):
            imp = imp + p_c[h * t:(h + 1) * t]
        sel = _select_blocks(imp, past_len, n_blk)
        kcol = slice(g * NSA_HD, (g + 1) * NSA_HD)
        vcol = slice((NSA_GROUPS + g) * NSA_HD, (NSA_GROUPS + g + 1) * NSA_HD)
        kslot, vslot = g, NSA_GROUPS + g

        tiles = []
        for p in range(n_pages + 1):
            key0 = p * page
            if p < n_pages:
                k = pages[p][pl.ds(kslot, page, stride=KV_SLOTS), :]
                v = pages[p][pl.ds(vslot, page, stride=KV_SLOTS), :]
            else:
                k, v = _pad_rows(ksn_ref[:, kcol], page), _pad_rows(ksn_ref[:, vcol], page)
            chosen = jnp.zeros((t, page), F32)
            for bi in range(blocks_per_page):
                blk = p * blocks_per_page + bi
                chosen = jnp.where(lane // NSA_BLOCK == bi, sel[:, blk:blk + 1], chosen)
            chosen = jnp.concatenate([chosen] * NSA_HPG, axis=0)
            dist = qpos - (key0 + lane)
            tiles.append((k, v, (dist >= 0) & (chosen > 0.5), slope * dist.astype(F32)))
        o_sel = _two_pass_attention(qs, tiles)

        tiles = []
        for p in range(n_win // page + 1):
            key0 = past_len - n_win + p * page
            if p < n_win // page:
                k = win_ref[pl.ds(p * page * KV_SLOTS + kslot, page, stride=KV_SLOTS), :]
                v = win_ref[pl.ds(p * page * KV_SLOTS + vslot, page, stride=KV_SLOTS), :]
            else:
                k, v = _pad_rows(kwn_ref[:, kcol], page), _pad_rows(kwn_ref[:, vcol], page)
            dist = qpos - (key0 + lane)
            valid = (dist >= 0) & (dist < NSA_WINDOW) & (key0 + lane >= 0)
            tiles.append((k, v, valid, slope * dist.astype(F32)))
        o_win = _two_pass_attention(qs, tiles)
        _write_gated(o_ref, gate_ref, g, t, o_cmp, o_sel, o_win)


def _nsa_step(proj, ckv, cache2d, page_table_flat, win2d, b, t, n_pages, page, n_win):
    past_len = n_pages * page
    assert n_win % page == 0 and t <= page
    n_cmp = (past_len + t) // NSA_BLOCK
    n_blk = -(-(past_len + t) // NSA_BLOCK)
    assert n_cmp * NSA_BLOCK == past_len and n_blk <= LANES
    n_seq = STEP_SEQS if b % STEP_SEQS == 0 else 1
    page_spec = lambda s, p: pl.BlockSpec(
        (page * KV_SLOTS, NSA_HD), lambda i, pt: (pt[(i * n_seq + s) * n_pages + p], 0))
    grid_spec = pltpu.PrefetchScalarGridSpec(
        num_scalar_prefetch=1,
        grid=(b // n_seq,),
        in_specs=[page_spec(s, p) for s in range(n_seq) for p in range(n_pages)] + [
            pl.BlockSpec((n_seq * t, NSA_HEADS * NSA_HD), lambda i, pt: (i, C_QA // 1024)),
            pl.BlockSpec((n_seq * t, LANES), lambda i, pt: (i, C_GN // LANES)),
            pl.BlockSpec((n_seq, 2, NSA_GROUPS, n_cmp, NSA_HD), lambda i, pt: (i, 0, 0, 0, 0)),
            pl.BlockSpec((n_seq * t, KV_COLS), lambda i, pt: (i, C_KVS // KV_COLS)),
            pl.BlockSpec((n_seq * t, KV_COLS), lambda i, pt: (i, C_KVW // KV_COLS)),
            pl.BlockSpec((n_seq * n_win * KV_SLOTS, NSA_HD), lambda i, pt: (i, 0))],
        out_specs=pl.BlockSpec((n_seq * t, NSA_HEADS * NSA_HD), lambda i, pt: (i, 0)),
    )
    return pl.pallas_call(
        functools.partial(_nsa_step_kernel, n_seq=n_seq, t=t, n_pages=n_pages, page=page, past_len=past_len,
                          n_win=n_win, n_cmp=n_cmp, n_blk=n_blk),
        grid_spec=grid_spec,
        out_shape=jax.ShapeDtypeStruct((b * t, NSA_HEADS * NSA_HD), F32),
        compiler_params=_cparams("parallel"),
        name="nsa_step",
    )(page_table_flat, *([cache2d] * (n_seq * n_pages)), proj, proj, ckv, proj, proj, win2d)


HG_ROWS = 128


def _hgrn_kernel(q_ref, f_ref, i_ref, og_ref, lb_ref, nw_ref, s0_ref, o_ref, sfin_ref,
                 st_ref, oacc_ref, *, rows_in, n_tblk, has_state):
    tb = pl.program_id(1)

    @pl.when(tb == 0)
    def _():
        for h in range(HG_HEADS):
            if has_state:
                st_ref[h] = s0_ref[0, h].T
            else:
                st_ref[h] = jnp.zeros((HG_DV, HG_DK), F32)

    pr = HG_CHUNK if rows_in <= HG_CHUNK else HG_ROWS
    assert rows_in <= pr

    def padded(ref):
        x = ref[...]
        return x if rows_in == pr else _pad_rows(x, pr)

    def key_rows(x):
        return x if pr == HG_ROWS else _pad_rows(x, HG_ROWS)

    q, f, v = padded(q_ref), padded(f_ref), padded(i_ref)
    lb = lb_ref[...]
    row = lax.broadcasted_iota(jnp.int32, (pr, 1), 0)
    live = row < rows_in
    forget = lb + (1.0 - lb) * jax.nn.sigmoid(f)
    k = jnp.where(live, (1.0 - lb) * jax.nn.sigmoid(-f), 0.0)
    gl = jnp.where(live, jnp.log(forget), 0.0)
    rc = row % HG_CHUNK
    cum, suf = gl, gl
    s = 1
    while s < HG_CHUNK:
        cum = cum + jnp.where(rc >= s, pltpu.roll(cum, s, axis=0), 0.0)
        suf = suf + jnp.where(rc < HG_CHUNK - s, pltpu.roll(suf, pr - s, axis=0), 0.0)
        s *= 2
    ki = k * jnp.exp(-cum)
    qd = {HG_CHUNK: q * jnp.exp(cum)}
    ke = {HG_CHUNK: k * jnp.exp(suf - gl)}
    ci = lax.broadcasted_iota(jnp.int32, (pr, HG_ROWS), 0)
    cj = lax.broadcasted_iota(jnp.int32, (pr, HG_ROWS), 1)
    masks = {HG_CHUNK: (ci // HG_CHUNK == cj // HG_CHUNK) & (ci >= cj)}
    w = HG_CHUNK
    while w < pr:
        tot = cum + suf - gl
        odd = (row // w) % 2 == 1
        cum = cum + jnp.where(odd, pltpu.roll(tot, w, axis=0), 0.0)
        suf = suf + jnp.where(odd, 0.0, pltpu.roll(tot, pr - w, axis=0))
        masks[2 * w] = ((ci // w) % 2 == 1) & (cj // w == ci // w - 1)
        w *= 2
        qd[w] = q * jnp.exp(cum)
        ke[w] = k * jnp.exp(suf - gl)
    decay = jnp.exp((cum + suf - gl)[0:1, :])
    for h in range(HG_HEADS):
        hs = slice(h * HG_DK, (h + 1) * HG_DK)
        att = jnp.where(masks[HG_CHUNK], _mm_nt(qd[HG_CHUNK][:, hs], key_rows(ki[:, hs])), 0.0)
        w = HG_CHUNK
        while w < pr:
            att = att + jnp.where(masks[2 * w], _mm_nt(qd[w][:, hs], key_rows(ke[w][:, hs])), 0.0)
            w *= 2
        v_h = key_rows(v[:, hs])
        st = st_ref[h]
        oacc_ref[0:pr, hs] = _mm(att, v_h) + _mm_nt(qd[pr][:, hs], st)
        st_ref[h] = decay[:, hs] * st + _mm(v_h.T, key_rows(ke[pr][:, hs]))
    og = og_ref[...]
    nw = nw_ref[...]
    for h in range(HG_HEADS):
        hs = slice(h * HG_DV, (h + 1) * HG_DV)
        x = oacc_ref[0:rows_in, hs]
        ms = jnp.mean(x * x, axis=-1, keepdims=True)
        y = x * lax.rsqrt(ms + EPS) * nw
        o_ref[:, hs] = (y * jax.nn.silu(og[:, hs])).astype(o_ref.dtype)

    @pl.when(tb == n_tblk - 1)
    def _():
        for h in range(HG_HEADS):
            sfin_ref[0, h] = st_ref[h].T


def _hgrn(proj, lb, norm_w, s0, b, t):
    rows_in = min(t, HG_ROWS)
    n_tblk = t // rows_in
    assert rows_in * n_tblk == t and rows_in % 8 == 0
    has_state = s0 is not None
    if s0 is None:
        s0 = jnp.zeros((1, HG_HEADS, HG_DK, HG_DV), F32)
    width = HG_HEADS * HG_DK
    col = lambda c: pl.BlockSpec((rows_in, width), lambda i, j: (i * n_tblk + j, c // width))
    return pl.pallas_call(
        functools.partial(_hgrn_kernel, rows_in=rows_in, n_tblk=n_tblk, has_state=has_state),
        grid=(b, n_tblk),
        in_specs=[col(C_QB), col(C_FB), col(C_IB), col(C_OG),
                  pl.BlockSpec((1, width), lambda i, j: (0, 0)),
                  pl.BlockSpec((1, HG_DV), lambda i, j: (0, 0)),
                  pl.BlockSpec((1, HG_HEADS, HG_DK, HG_DV),
                               (lambda i, j: (i, 0, 0, 0)) if has_state else (lambda i, j: (0, 0, 0, 0)))],
        out_specs=[pl.BlockSpec((rows_in, width), lambda i, j: (i * n_tblk + j, 0)),
                   pl.BlockSpec((1, HG_HEADS, HG_DK, HG_DV), lambda i, j: (i, 0, 0, 0))],
        out_shape=[jax.ShapeDtypeStruct((b * t, width), BF16),
                   jax.ShapeDtypeStruct((b, HG_HEADS, HG_DK, HG_DV), F32)],
        scratch_shapes=[pltpu.VMEM((HG_HEADS, HG_DV, HG_DK), F32), pltpu.VMEM((HG_ROWS, width), F32)],
        compiler_params=_cparams("parallel", "arbitrary"),
        name="hgrn2",
    )(proj, proj, proj, proj, lb.reshape(1, width), norm_w.reshape(1, HG_DV), s0)


def _decoder_layer(x, past, lb, params):
    (w_in, w_a, w_b, w_o, pe, w1, w2, hg_norm, ln1, ln2, ln3, ln4, w_up, cw, cb, w_dn) = params
    b, t, d = x.shape
    m = b * t
    x2 = x.reshape(m, d)
    tm = min(MM_ROWS, m)
    proj = _in_projection(x2, ln1, *w_in, tm)
    kv_shape = (b, t, 2, NSA_GROUPS, NSA_HD)
    kvc, kvs, kvw = (a.reshape(kv_shape) for a in _kv_relayout(proj, min(CONV_ROWS, m)))
    if past is None:
        ckv = _compress_seq(proj, b, t, pe, w1, w2)
        o_a = _nsa_seq(proj, ckv, b, t)
        new_win = kvw[:, t - min(NSA_WINDOW, t):]
        s0, conv_buf = None, None
    else:
        cache_c, cache_s, page_table_flat, n_pages, page, win_buf, s0, conv_buf = past
        n_win = win_buf.shape[1]
        ckv = _compress_paged(cache_c, page_table_flat, b, n_pages, page, pe, w1, w2)
        o_a = _nsa_step(proj, ckv, cache_s, page_table_flat, win_buf.reshape(-1, NSA_HD), b, t, n_pages, page, n_win)
        new_win = jnp.concatenate([win_buf, kvw], axis=1)[:, t:]
    o_hg, s_fin = _hgrn(proj, lb, hg_norm, s0, b, t)
    mixed = _gated_merge(o_a.astype(BF16), o_hg, w_a, w_b, proj, tm, 512)
    x1 = _matmul_norm_res(mixed, w_o, x2, ln2, tm, 512)
    u = _norm_matmul(x1, ln3, w_up, tm, 512)
    if past is None:
        y = _conv_glu_down(u, cw, cb, w_dn, x1, ln4, t, min(CONV_ROWS, t), 512)
        new_conv = u.reshape(b, t, 2 * D_FF)[:, t - (CONV_W - 1):]
    else:
        act = _conv_glu_step(u.reshape(b, t, 2 * D_FF), conv_buf, cw, cb, min(64, b), 512).reshape(m, D_FF)
        new_conv = jnp.concatenate([conv_buf, u.reshape(b, t, 2 * D_FF)], axis=1)[:, t:] if t < CONV_W - 1 else \
            u.reshape(b, t, 2 * D_FF)[:, t - (CONV_W - 1):]
        y = _matmul_norm_res(act, w_dn, x1, ln4, tm, 512)
    return y.reshape(b, t, d), (kvc, kvs, new_win, s_fin, new_conv)


def _split_w_in(w):
    o_gate = NSA_HEADS * NSA_HD + 3 * KV_COLS
    o_rest = o_gate + 3 * NSA_HEADS
    gate = jnp.pad(w[:, o_gate:o_rest], ((0, 0), (0, PROJ_TN - 3 * NSA_HEADS)))
    return w[:, :o_gate].astype(BF16), w[:, o_rest:].astype(BF16), gate.astype(BF16)


def kernel(x_prompt, x_sample, cache_cmp_kv, cache_sel_kv, state_win_kv, state_hgrn, state_conv, page_table,
           w_in, w_branch_a, w_branch_b, w_out, cmp_pe, cmp_w1, cmp_w2, hg_lb_raw, hg_norm_w, ln_mix_pre,
           ln_mix_post, ln_ffn_pre, ln_ffn_post, w_up, conv_w, conv_b, w_down):
    depth = w_in.shape[0]
    dec_b, n_pages = page_table.shape
    page = cache_cmp_kv.shape[2]
    lb_all = jnp.cumsum(jax.nn.softmax(hg_lb_raw.astype(F32), axis=0), axis=0)
    pt_flat = page_table.reshape(-1).astype(jnp.int32)
    y_p, y_s = x_prompt, x_sample
    new_p, new_s = [], []
    for l in range(depth):
        params = (_split_w_in(w_in[l]), w_branch_a[l].astype(BF16), w_branch_b[l].astype(BF16),
                  w_out[l].astype(BF16), cmp_pe[l].transpose(1, 0, 2)[:, :, None, :],
                  cmp_w1[l].astype(BF16).transpose(1, 0, 2, 3).reshape(2, NSA_BLOCK // 2, 2 * NSA_HD, NSA_HD),
                  cmp_w2[l].astype(BF16), hg_norm_w[l], ln_mix_pre[l], ln_mix_post[l], ln_ffn_pre[l],
                  ln_ffn_post[l], w_up[l].astype(BF16), conv_w[l], conv_b[l], w_down[l].astype(BF16))
        y_p, st_p = _decoder_layer(y_p, None, lb_all[l], params)
        past = (cache_cmp_kv[l].reshape(-1, NSA_HD), cache_sel_kv[l].reshape(-1, NSA_HD),
                pt_flat, n_pages, page, state_win_kv[l], state_hgrn[l], state_conv[l])
        y_s, st_s = _decoder_layer(y_s, past, lb_all[l], params)
        new_p.append(st_p)
        new_s.append(st_s)

    def stack(group, i):
        return jnp.stack([st[i] for st in group], axis=0)

    return (y_p, y_s, stack(new_p, 0), stack(new_s, 0), stack(new_p, 1), stack(new_s, 1), stack(new_p, 2),
            stack(new_s, 2), stack(new_p, 3), stack(new_s, 3), stack(new_p, 4), stack(new_s, 4))
```

```python
import functools

import jax
import jax.numpy as jnp
from jax import lax
from jax.experimental import pallas as pl
from jax.experimental.pallas import tpu as pltpu

F32 = jnp.float32
BF16 = jnp.bfloat16

D_MODEL = 2048
NSA_HEADS = 8
NSA_GROUPS = 2
NSA_HPG = NSA_HEADS // NSA_GROUPS
NSA_HD = 128
NSA_BLOCK = 64
NSA_TOPN = 8
NSA_WINDOW = 512
NSA_SCALE = NSA_HD ** -0.5
FORCED_SCORE = NSA_HPG + 1.0
HG_HEADS = 8
HG_DK = 128
HG_DV = 128
HG_CHUNK = 16
D_FF = 5632
CONV_W = 3
EPS = 1e-6
KV_COLS = 2 * NSA_GROUPS * NSA_HD

C_QA = 0
C_QB = 1024
C_FB = 2048
C_IB = 3072
C_OG = 4096
C_GA = 5120
C_GB = 7168
C_KVC = 9216
C_KVS = 9728
C_KVW = 10240
C_GN = 10752

LANES = 128
MM_ROWS = 1024
UP_TN = 1024
CONV_ROWS = 512
DOWN_PARTS = 8
NEG_BIG = -1e30
VMEM_LIMIT = 56 * 1024 * 1024


def _cparams(*sem):
    return pltpu.CompilerParams(dimension_semantics=sem, vmem_limit_bytes=VMEM_LIMIT)


def _mm(a, b):
    return jnp.dot(a.astype(BF16), b.astype(BF16), preferred_element_type=F32)


def _mm_nt(a, b):
    return lax.dot_general(a.astype(BF16), b.astype(BF16), (((1,), (1,)), ((), ())),
                           preferred_element_type=F32)


def _norm_mm_kernel(x_ref, ln_ref, w_ref, o_ref, h_ref):
    @pl.when(pl.program_id(1) == 0)
    def _():
        x = x_ref[...]
        ms = jnp.mean(x * x, axis=-1, keepdims=True)
        h_ref[...] = (x * lax.rsqrt(ms + EPS) * ln_ref[...]).astype(BF16)

    o_ref[...] = _mm(h_ref[...], w_ref[...])


def _norm_matmul(x, ln, w, tm, tn):
    m, k = x.shape
    n = w.shape[1]
    return pl.pallas_call(
        _norm_mm_kernel,
        grid=(m // tm, n // tn),
        in_specs=[pl.BlockSpec((tm, k), lambda i, j: (i, 0)),
                  pl.BlockSpec((1, k), lambda i, j: (0, 0)),
                  pl.BlockSpec((k, tn), lambda i, j: (0, j))],
        out_specs=pl.BlockSpec((tm, tn), lambda i, j: (i, j)),
        out_shape=jax.ShapeDtypeStruct((m, n), F32),
        scratch_shapes=[pltpu.VMEM((tm, k), BF16)],
        compiler_params=_cparams("parallel", "arbitrary"),
        name="norm_matmul",
    )(x, ln.reshape(1, k), w)


PROJ_TN = 512
TILES_Q = (C_QB - C_QA) // PROJ_TN
TILE_REST0 = C_QB // PROJ_TN
TILE_KV0 = C_KVC // PROJ_TN
TILE_GATE = C_GN // PROJ_TN
N_PROJ = (TILE_GATE + 1) * PROJ_TN


def _in_proj_kernel(x_ref, ln_ref, wh_ref, wr_ref, wg_ref, o_ref, h_ref):
    j = pl.program_id(1)

    @pl.when(j == 0)
    def _():
        x = x_ref[...]
        ms = jnp.mean(x * x, axis=-1, keepdims=True)
        h_ref[...] = (x * lax.rsqrt(ms + EPS) * ln_ref[...]).astype(BF16)

    from_rest = (j >= TILE_REST0) & (j < TILE_KV0)
    from_gate = j == TILE_GATE

    @pl.when(from_rest)
    def _():
        o_ref[...] = _mm(h_ref[...], wr_ref[...])

    @pl.when(from_gate)
    def _():
        o_ref[...] = _mm(h_ref[...], wg_ref[...])

    @pl.when(jnp.logical_not(from_rest | from_gate))
    def _():
        o_ref[...] = _mm(h_ref[...], wh_ref[...])


def _in_projection(x, ln, w_head, w_rest, w_gate, tm):
    m, k = x.shape
    tn = PROJ_TN
    n_head = w_head.shape[1] // tn
    n_rest = w_rest.shape[1] // tn
    assert n_head == TILES_Q + TILE_GATE - TILE_KV0 and n_rest == TILE_KV0 - TILE_REST0 and w_gate.shape[1] == tn
    head_tile = lambda j: jnp.where(j < TILES_Q, j, jnp.clip(j - (TILE_KV0 - TILES_Q), TILES_Q - 1, n_head - 1))
    rest_tile = lambda j: jnp.clip(j - TILE_REST0, 0, n_rest - 1)
    return pl.pallas_call(
        _in_proj_kernel,
        grid=(m // tm, N_PROJ // tn),
        in_specs=[pl.BlockSpec((tm, k), lambda i, j: (i, 0)),
                  pl.BlockSpec((1, k), lambda i, j: (0, 0)),
                  pl.BlockSpec((k, tn), lambda i, j: (0, head_tile(j))),
                  pl.BlockSpec((k, tn), lambda i, j: (0, rest_tile(j))),
                  pl.BlockSpec((k, tn), lambda i, j: (0, 0))],
        out_specs=pl.BlockSpec((tm, tn), lambda i, j: (i, j)),
        out_shape=jax.ShapeDtypeStruct((m, N_PROJ), F32),
        scratch_shapes=[pltpu.VMEM((tm, k), BF16)],
        compiler_params=_cparams("parallel", "arbitrary"),
        name="in_projection",
    )(x, ln.reshape(1, k), w_head, w_rest, w_gate)


def _kv_relayout_kernel(c_ref, s_ref, w_ref, oc_ref, os_ref, ow_ref, *, tm):
    for x_ref, o_ref in ((c_ref, oc_ref), (s_ref, os_ref), (w_ref, ow_ref)):
        for slot in range(2 * NSA_GROUPS):
            o_ref[pl.ds(slot, tm, stride=2 * NSA_GROUPS), :] = x_ref[:, slot * NSA_HD:(slot + 1) * NSA_HD]


def _kv_relayout(proj, tm):
    m = proj.shape[0]
    slots = 2 * NSA_GROUPS
    out = jax.ShapeDtypeStruct((m * slots, NSA_HD), F32)
    return pl.pallas_call(
        functools.partial(_kv_relayout_kernel, tm=tm),
        grid=(m // tm,),
        in_specs=[pl.BlockSpec((tm, KV_COLS), lambda i, c=c: (i, c // KV_COLS)) for c in (C_KVC, C_KVS, C_KVW)],
        out_specs=[pl.BlockSpec((tm * slots, NSA_HD), lambda i: (i, 0))] * 3,
        out_shape=[out] * 3,
        compiler_params=_cparams("parallel"),
        name="kv_relayout",
    )(proj, proj, proj)


def _mm_norm_res_kernel(a_ref, w_ref, res_ref, ln_ref, o_ref, acc_ref, *, nk):
    k = pl.program_id(1)

    @pl.when(k == 0)
    def _():
        acc_ref[...] = jnp.zeros_like(acc_ref)

    acc_ref[...] += _mm(a_ref[...], w_ref[...])

    @pl.when(k == nk - 1)
    def _():
        y = acc_ref[...]
        ms = jnp.mean(y * y, axis=-1, keepdims=True)
        o_ref[...] = res_ref[...] + y * lax.rsqrt(ms + EPS) * ln_ref[...]


def _matmul_norm_res(a, w, res, ln, tm, tk):
    m, kk = a.shape
    n = w.shape[1]
    nk = kk // tk
    return pl.pallas_call(
        functools.partial(_mm_norm_res_kernel, nk=nk),
        grid=(m // tm, nk),
        in_specs=[pl.BlockSpec((tm, tk), lambda i, k: (i, k)),
                  pl.BlockSpec((tk, n), lambda i, k: (k, 0)),
                  pl.BlockSpec((tm, n), lambda i, k: (i, 0)),
                  pl.BlockSpec((1, n), lambda i, k: (0, 0))],
        out_specs=pl.BlockSpec((tm, n), lambda i, k: (i, 0)),
        out_shape=jax.ShapeDtypeStruct((m, n), F32),
        scratch_shapes=[pltpu.VMEM((tm, n), F32)],
        compiler_params=_cparams("parallel", "arbitrary"),
        name="matmul_norm_res",
    )(a, w, res, ln.reshape(1, n))


def _merge_kernel(oa_ref, ohg_ref, wa_ref, wb_ref, ga_ref, gb_ref, o_ref):
    a = _mm(oa_ref[...], wa_ref[...])
    b = _mm(ohg_ref[...], wb_ref[...])
    o_ref[...] = (jax.nn.sigmoid(ga_ref[...]) * a + jax.nn.sigmoid(gb_ref[...]) * b).astype(BF16)


def _gated_merge(o_a, o_hg, w_a, w_b, proj, tm, tn):
    m, k = o_a.shape
    n = w_a.shape[1]
    return pl.pallas_call(
        _merge_kernel,
        grid=(m // tm, n // tn),
        in_specs=[pl.BlockSpec((tm, k), lambda i, j: (i, 0)),
                  pl.BlockSpec((tm, k), lambda i, j: (i, 0)),
                  pl.BlockSpec((k, tn), lambda i, j: (0, j)),
                  pl.BlockSpec((k, tn), lambda i, j: (0, j)),
                  pl.BlockSpec((tm, tn), lambda i, j: (i, C_GA // tn + j)),
                  pl.BlockSpec((tm, tn), lambda i, j: (i, C_GB // tn + j))],
        out_specs=pl.BlockSpec((tm, tn), lambda i, j: (i, j)),
        out_shape=jax.ShapeDtypeStruct((m, n), BF16),
        compiler_params=_cparams("parallel", "arbitrary"),
        name="gated_merge",
    )(o_a, o_hg, w_a, w_b, proj, proj)


def _conv_taps(fa, fg, cwa_ref, cwg_ref, cba_ref, cbg_ref, tap):
    ca = cba_ref[...]
    cg = cbg_ref[...]
    for j in range(CONV_W):
        ca = ca + tap(fa, j) * cwa_ref[j:j + 1, :]
        cg = cg + tap(fg, j) * cwg_ref[j:j + 1, :]
    return jax.nn.gelu(cg, approximate=True) * ca


def _conv_glu_down_kernel(ua_ref, ug_ref, ha_ref, hg_ref, cwa_ref, cwg_ref, cba_ref, cbg_ref, w_ref, res_ref,
                          ln_ref, o_ref, fa_ref, fg_ref, act_ref, acc_ref, *, tm, tiles_per_seq, nk):
    k = pl.program_id(1)
    slot = k % 2

    @pl.when(k == 0)
    def _():
        acc_ref[...] = jnp.zeros_like(acc_ref)
        act_ref[1] = jnp.zeros(act_ref.shape[1:], act_ref.dtype)

    first = (pl.program_id(0) % tiles_per_seq) == 0
    fa_ref[0:8, :] = jnp.where(first, 0.0, ha_ref[...])
    fg_ref[0:8, :] = jnp.where(first, 0.0, hg_ref[...])
    fa_ref[8:, :] = ua_ref[...]
    fg_ref[8:, :] = ug_ref[...]
    n_parts = DOWN_PARTS
    cols = acc_ref.shape[1] // n_parts
    rows = tm // n_parts
    for part in range(n_parts):
        cs = slice(part * cols, (part + 1) * cols)
        acc_ref[:, cs] += _mm(act_ref[1 - slot], w_ref[:, cs])
        r0 = part * rows
        xa, xg = fa_ref[r0:r0 + rows + 8, :], fg_ref[r0:r0 + rows + 8, :]
        tap = lambda x, j: (x if j == CONV_W - 1 else pltpu.roll(x, CONV_W - 1 - j, axis=0))[8:]
        act = _conv_taps(xa, xg, cwa_ref, cwg_ref, cba_ref, cbg_ref, tap)
        act_ref[slot, r0:r0 + rows, :] = act.astype(act_ref.dtype)

    @pl.when(k == nk)
    def _():
        y = acc_ref[...]
        ms = jnp.mean(y * y, axis=-1, keepdims=True)
        o_ref[...] = res_ref[...] + y * lax.rsqrt(ms + EPS) * ln_ref[...]


def _conv_glu_down(u, conv_w, conv_b, w, res, ln, seq_len, tm, tk):
    m = u.shape[0]
    n = w.shape[1]
    nk = D_FF // tk
    hb = tm // 8
    kc = lambda k: jnp.minimum(k, nk - 1)
    kw = lambda k: jnp.maximum(k - 1, 0)
    halo = lambda i, k, off: (jnp.maximum(i * hb - 1, 0), kc(k) + off)
    return pl.pallas_call(
        functools.partial(_conv_glu_down_kernel, tm=tm, tiles_per_seq=seq_len // tm, nk=nk),
        grid=(m // tm, nk + 1),
        in_specs=[pl.BlockSpec((tm, tk), lambda i, k: (i, kc(k))),
                  pl.BlockSpec((tm, tk), lambda i, k: (i, kc(k) + nk)),
                  pl.BlockSpec((8, tk), lambda i, k: halo(i, k, 0)),
                  pl.BlockSpec((8, tk), lambda i, k: halo(i, k, nk)),
                  pl.BlockSpec((CONV_W, tk), lambda i, k: (0, kc(k))),
                  pl.BlockSpec((CONV_W, tk), lambda i, k: (0, kc(k) + nk)),
                  pl.BlockSpec((1, tk), lambda i, k: (0, kc(k))),
                  pl.BlockSpec((1, tk), lambda i, k: (0, kc(k) + nk)),
                  pl.BlockSpec((tk, n), lambda i, k: (kw(k), 0)),
                  pl.BlockSpec((tm, n), lambda i, k: (i, 0)),
                  pl.BlockSpec((1, n), lambda i, k: (0, 0))],
        out_specs=pl.BlockSpec((tm, n), lambda i, k: (i, 0)),
        out_shape=jax.ShapeDtypeStruct((m, n), F32),
        scratch_shapes=[pltpu.VMEM((tm + 8, tk), F32), pltpu.VMEM((tm + 8, tk), F32),
                        pltpu.VMEM((2, tm, tk), BF16), pltpu.VMEM((tm, n), F32)],
        compiler_params=_cparams("parallel", "arbitrary"),
        name="conv_glu_down",
    )(u, u, u, u, conv_w, conv_w, conv_b.reshape(1, -1), conv_b.reshape(1, -1), w, res, ln.reshape(1, n))


def _conv_glu_step_kernel(ua_ref, ug_ref, ba_ref, bg_ref, cwa_ref, cwg_ref, cba_ref, cbg_ref,
                          o_ref, fa_ref, fg_ref, *, t):
    fa_ref[:, 8 - (CONV_W - 1):8, :] = ba_ref[...]
    fg_ref[:, 8 - (CONV_W - 1):8, :] = bg_ref[...]
    fa_ref[:, 8:, :] = ua_ref[...]
    fg_ref[:, 8:, :] = ug_ref[...]
    tap = lambda f, j: f[:, 8 - (CONV_W - 1) + j:8 - (CONV_W - 1) + j + t, :]
    o_ref[...] = _conv_taps(fa_ref, fg_ref, cwa_ref, cwg_ref, cba_ref, cbg_ref, tap)


def _conv_glu_step(u3, buf, conv_w, conv_b, nb, tn):
    b, t, _ = u3.shape
    nj = D_FF // tn
    return pl.pallas_call(
        functools.partial(_conv_glu_step_kernel, t=t),
        grid=(b // nb, nj),
        in_specs=[pl.BlockSpec((nb, t, tn), lambda i, j: (i, 0, j)),
                  pl.BlockSpec((nb, t, tn), lambda i, j: (i, 0, j + nj)),
                  pl.BlockSpec((nb, CONV_W - 1, tn), lambda i, j: (i, 0, j)),
                  pl.BlockSpec((nb, CONV_W - 1, tn), lambda i, j: (i, 0, j + nj)),
                  pl.BlockSpec((CONV_W, tn), lambda i, j: (0, j)),
                  pl.BlockSpec((CONV_W, tn), lambda i, j: (0, j + nj)),
                  pl.BlockSpec((1, tn), lambda i, j: (0, j)),
                  pl.BlockSpec((1, tn), lambda i, j: (0, j + nj))],
        out_specs=pl.BlockSpec((nb, t, tn), lambda i, j: (i, 0, j)),
        out_shape=jax.ShapeDtypeStruct((b, t, D_FF), F32),
        scratch_shapes=[pltpu.VMEM((nb, 8 + t, tn), F32), pltpu.VMEM((nb, 8 + t, tn), F32)],
        compiler_params=_cparams("parallel", "arbitrary"),
        name="conv_glu_step",
    )(u3, u3, buf, buf, conv_w, conv_w, conv_b.reshape(1, -1), conv_b.reshape(1, -1))


def _compress_body(load_x, pe_ref, w1_ref, w2_ref, o_ref, n_seq, n_cmp):
    for c in range(2):
        acc = jnp.zeros((NSA_GROUPS * n_seq * n_cmp, NSA_HD), F32)
        for l in range(0, NSA_BLOCK, 2):
            halves = [jnp.concatenate([load_x(l + dl, c, g) for g in range(NSA_GROUPS)], axis=0) + pe_ref[c, l + dl]
                      for dl in range(2)]
            acc = acc + _mm(jnp.concatenate(halves, axis=1), w1_ref[c, l // 2])
        out = _mm(jax.nn.silu(acc), w2_ref[c])
        for g in range(NSA_GROUPS):
            for s in range(n_seq):
                r0 = (g * n_seq + s) * n_cmp
                o_ref[s, c, g] = out[r0:r0 + n_cmp]


def _compress_seq_kernel(x0_ref, x1_ref, x2_ref, x3_ref, pe_ref, w1_ref, w2_ref, o_ref, *, n_cmp):
    xs = (x0_ref, x1_ref, x2_ref, x3_ref)
    load_x = lambda l, c, g: xs[c * NSA_GROUPS + g][pl.ds(l, n_cmp, stride=NSA_BLOCK), :]
    _compress_body(load_x, pe_ref, w1_ref, w2_ref, o_ref, 1, n_cmp)


def _compress_seq(proj, b, t, pe, w1, w2):
    n_cmp = t // NSA_BLOCK
    rows = n_cmp * NSA_BLOCK
    assert rows == t
    return pl.pallas_call(
        functools.partial(_compress_seq_kernel, n_cmp=n_cmp),
        grid=(b,),
        in_specs=[pl.BlockSpec((rows, NSA_HD), lambda i, cg=cg: (i, C_KVC // NSA_HD + cg))
                  for cg in range(2 * NSA_GROUPS)] + [
                  pl.BlockSpec(pe.shape, lambda i: (0, 0, 0, 0)),
                  pl.BlockSpec(w1.shape, lambda i: (0, 0, 0, 0)),
                  pl.BlockSpec(w2.shape, lambda i: (0, 0, 0))],
        out_specs=pl.BlockSpec((1, 2, NSA_GROUPS, n_cmp, NSA_HD), lambda i: (i, 0, 0, 0, 0)),
        out_shape=jax.ShapeDtypeStruct((b, 2, NSA_GROUPS, n_cmp, NSA_HD), F32),
        compiler_params=_cparams("parallel"),
        name="compress_seq",
    )(proj, proj, proj, proj, pe, w1, w2)


KV_SLOTS = 2 * NSA_GROUPS
CMP_SEQS = 2
CMP_PITCH = NSA_BLOCK * KV_SLOTS + 8


def _compress_paged_kernel(pt_ref, *refs, n_seq, n_pages, page_rows):
    del pt_ref
    pages = refs[:n_seq * n_pages]
    pe_ref, w1_ref, w2_ref, o_ref, x_ref = refs[n_seq * n_pages:]
    blk_rows = NSA_BLOCK * KV_SLOTS
    per_page = page_rows // blk_rows
    for i, pg in enumerate(pages):
        for j in range(per_page):
            n = i * per_page + j
            x_ref[n * CMP_PITCH:n * CMP_PITCH + blk_rows, :] = pg[j * blk_rows:(j + 1) * blk_rows, :]
    n_cmp = n_pages * per_page
    load_x = lambda l, c, g: x_ref[pl.ds(l * KV_SLOTS + c * NSA_GROUPS + g, n_seq * n_cmp, stride=CMP_PITCH), :]
    _compress_body(load_x, pe_ref, w1_ref, w2_ref, o_ref, n_seq, n_cmp)


def _compress_paged(cache2d, page_table_flat, b, n_pages, page, pe, w1, w2):
    page_rows = page * KV_SLOTS
    n_cmp = n_pages * page // NSA_BLOCK
    n_seq = CMP_SEQS if b % CMP_SEQS == 0 else 1
    page_spec = lambda s, p: pl.BlockSpec(
        (page_rows, NSA_HD), lambda i, pt: (pt[(i * n_seq + s) * n_pages + p], 0))
    grid_spec = pltpu.PrefetchScalarGridSpec(
        num_scalar_prefetch=1,
        grid=(b // n_seq,),
        in_specs=[page_spec(s, p) for s in range(n_seq) for p in range(n_pages)] + [
            pl.BlockSpec(pe.shape, lambda i, pt: (0, 0, 0, 0)),
            pl.BlockSpec(w1.shape, lambda i, pt: (0, 0, 0, 0)),
            pl.BlockSpec(w2.shape, lambda i, pt: (0, 0, 0))],
        out_specs=pl.BlockSpec((n_seq, 2, NSA_GROUPS, n_cmp, NSA_HD), lambda i, pt: (i, 0, 0, 0, 0)),
        scratch_shapes=[pltpu.VMEM((n_seq * n_cmp * CMP_PITCH, NSA_HD), F32)],
    )
    return pl.pallas_call(
        functools.partial(_compress_paged_kernel, n_seq=n_seq, n_pages=n_pages, page_rows=page_rows),
        grid_spec=grid_spec,
        out_shape=jax.ShapeDtypeStruct((b, 2, NSA_GROUPS, n_cmp, NSA_HD), F32),
        compiler_params=_cparams("parallel"),
        name="compress_paged",
    )(page_table_flat, *([cache2d] * (n_seq * n_pages)), pe, w1, w2)


def _alibi_slope(g, h):
    return 2.0 ** (-8.0 * (g * NSA_HPG + h + 1.0) / NSA_HEADS)


def _row_consts(tq, g, q_pos0):
    rows = NSA_HPG * tq
    r = lax.broadcasted_iota(jnp.int32, (rows, 1), 0)
    head = r // tq
    qpos = q_pos0 + (r - head * tq)
    slope = jnp.zeros((rows, 1), F32)
    for h in range(NSA_HPG):
        slope = jnp.where(head == h, _alibi_slope(g, h), slope)
    return qpos, slope


def _stack_heads(q_ref, g):
    return jnp.concatenate(
        [q_ref[:, (g * NSA_HPG + h) * NSA_HD:(g * NSA_HPG + h + 1) * NSA_HD] for h in range(NSA_HPG)], axis=0)


def _compressed_branch(qs, ck, cv, qpos, slope, n_cmp):
    pad = jnp.zeros((LANES - n_cmp, NSA_HD), F32)
    ckp = jnp.concatenate([ck, pad], axis=0)
    cvp = jnp.concatenate([cv, pad], axis=0)
    n = lax.broadcasted_iota(jnp.int32, (1, LANES), 1)
    dist = qpos - ((n + 1) * NSA_BLOCK - 1)
    s = _mm_nt(qs, ckp) * NSA_SCALE - slope * dist.astype(F32)
    mask = (dist >= 0) & (n < n_cmp)
    s = jnp.where(mask, s, NEG_BIG)
    m = jnp.max(s, axis=-1, keepdims=True)
    e = jnp.where(mask, jnp.exp(s - m), 0.0)
    d = jnp.sum(e, axis=-1, keepdims=True)
    p = e / jnp.where(d > 0, d, 1.0)
    return _mm(p, cvp), p


def _select_blocks(imp, q_pos0, n_blk):
    tq = imp.shape[0]
    nb = -(-n_blk // 8) * 8
    imp_t = (imp if tq == LANES else _pad_rows(imp, LANES)).T[0:nb]
    blk = lax.broadcasted_iota(jnp.int32, (nb, LANES), 0)
    cur = (q_pos0 + lax.broadcasted_iota(jnp.int32, (nb, LANES), 1)) // NSA_BLOCK
    forced = (blk == 0) | (blk == cur) | (blk == cur - 1)
    valid = blk <= cur
    score = jnp.where(valid, jnp.where(forced, FORCED_SCORE, imp_t), -1.0)
    score = jnp.where(blk < n_blk, score, -2.0)
    blk_f = blk.astype(F32)
    sel = jnp.zeros((nb, LANES), F32)
    for _ in range(min(NSA_TOPN, n_blk)):
        mx = jnp.max(score, axis=0, keepdims=True)
        first = jnp.min(jnp.where(score == mx, blk_f, 1e9), axis=0, keepdims=True)
        hit = blk_f == first
        sel = jnp.where(hit, 1.0, sel)
        score = jnp.where(hit, -3.0, score)
    return _pad_rows(sel, LANES).T[0:tq]


def _expand_sel(sel, key0, nkeys):
    bi = lax.broadcasted_iota(jnp.int32, (LANES, nkeys), 0)
    ki = lax.broadcasted_iota(jnp.int32, (LANES, nkeys), 1)
    expand = (bi == (key0 + ki) // NSA_BLOCK).astype(BF16)
    return _mm(sel, expand)


def _write_gated(o_ref, gate_ref, g, tq, o_cmp, o_sel, o_win):
    sig = jax.nn.sigmoid(gate_ref[...])
    for h in range(NSA_HPG):
        c0 = (g * NSA_HPG + h) * 3
        rs = slice(h * tq, (h + 1) * tq)
        o = sig[:, c0:c0 + 1] * o_cmp[rs] + sig[:, c0 + 1:c0 + 2] * o_sel[rs] + sig[:, c0 + 2:c0 + 3] * o_win[rs]
        o_ref[:, (g * NSA_HPG + h) * NSA_HD:(g * NSA_HPG + h + 1) * NSA_HD] = o.astype(o_ref.dtype)


SEQ_TK = 256


def _lane_groups(x):
    return [x[:, i:i + LANES] for i in range(0, x.shape[1], LANES)]


def _tiled_attention(heads, kv_ref, lo, hi, tile_masks, s_ref, m_ref, l_ref, acc_ref, with_tile0=False):
    tq = m_ref.shape[0] // len(heads)
    head_rows = [slice(i * tq, (i + 1) * tq) for i in range(len(heads))]
    groups = sorted({g for _, _, g in heads})
    kcol = lambda g: slice(g * NSA_HD, (g + 1) * NSA_HD)
    vcol = lambda g: slice((NSA_GROUPS + g) * NSA_HD, (NSA_GROUPS + g + 1) * NSA_HD)
    m_ref[...] = jnp.full(m_ref.shape, NEG_BIG, F32)

    def scores(kt, carry):
        key0 = kt * SEQ_TK
        masks = tile_masks(key0)
        rows = pl.ds(pl.multiple_of(key0, SEQ_TK), SEQ_TK)
        k = {g: kv_ref[rows, kcol(g)] for g in groups}
        for (load_q, slope, g), rs in zip(heads, head_rows):
            valid, dist = masks[g]
            s = jnp.where(valid, _mm_nt(load_q(), k[g]) * NSA_SCALE - slope * dist, NEG_BIG)
            s_ref[kt, rs] = s
            m_ref[rs] = functools.reduce(jnp.maximum, [m_ref[rs]] + _lane_groups(s))
        return carry

    if with_tile0:
        scores(jnp.int32(0), 0)
    lax.fori_loop(lo, hi, scores, 0)
    for rs in head_rows:
        m = jnp.maximum(jnp.max(m_ref[rs], axis=-1, keepdims=True), 0.1 * NEG_BIG)
        m_ref[rs] = jnp.broadcast_to(m, (tq, LANES))
    l_ref[...] = jnp.zeros(l_ref.shape, F32)
    acc_ref[...] = jnp.zeros(acc_ref.shape, F32)

    def values(kt, carry):
        rows = pl.ds(pl.multiple_of(kt * SEQ_TK, SEQ_TK), SEQ_TK)
        v = {g: kv_ref[rows, vcol(g)] for g in groups}
        for (_, _, g), rs in zip(heads, head_rows):
            m = m_ref[rs]
            p = [jnp.exp(s - m) for s in _lane_groups(s_ref[kt, rs])]
            l_ref[rs] += sum(p[1:], p[0])
            acc_ref[rs] += _mm(jnp.concatenate(p, axis=1), v[g])
        return carry

    if with_tile0:
        values(jnp.int32(0), 0)
    lax.fori_loop(lo, hi, values, 0)
    outs = []
    for rs in head_rows:
        l = jnp.sum(l_ref[rs], axis=-1, keepdims=True)
        outs.append(acc_ref[rs] / jnp.where(l > 0, l, 1.0))
    return jnp.concatenate(outs, axis=0)


def _nsa_seq_kernel(q_ref, gate_ref, ckv_ref, ks_ref, kw_ref, o_ref, s_ref, m_ref, l_ref, acc_ref,
                    *, tq, n_cmp, n_blk):
    j = pl.program_id(1)
    q0 = j * tq
    lane = lax.broadcasted_iota(jnp.int32, (1, SEQ_TK), 1)
    qpos_t = q0 + lax.broadcasted_iota(jnp.int32, (tq, 1), 0)
    hi = (q0 + tq - 1) // SEQ_TK + 1
    o_cmp, sel = [], []
    for g in range(NSA_GROUPS):
        qs = _stack_heads(q_ref, g)
        qpos, slope = _row_consts(tq, g, q0)
        o_g, p_c = _compressed_branch(qs, ckv_ref[0, 0, g], ckv_ref[0, 1, g], qpos, slope, n_cmp)
        imp = p_c[0:tq]
        for h in range(1, NSA_HPG):
            imp = imp + p_c[h * tq:(h + 1) * tq]
        o_cmp.append(o_g)
        sel.append(_select_blocks(imp, q0, n_blk))
    order = [(g, h) for h in range(NSA_HPG) for g in range(NSA_GROUPS)]
    heads = [(lambda c=(g * NSA_HPG + h) * NSA_HD: q_ref[:, c:c + NSA_HD], _alibi_slope(g, h), g) for g, h in order]
    scratch = (s_ref, m_ref, l_ref, acc_ref)

    def sel_masks(key0):
        dist = qpos_t - (key0 + lane)
        dist_f = dist.astype(F32)
        return [((dist >= 0) & (_expand_sel(sel[g], key0, SEQ_TK) > 0.5), dist_f) for g in range(NSA_GROUPS)]

    picked = functools.reduce(jnp.maximum, sel)
    blk = lax.broadcasted_iota(jnp.int32, picked.shape, 1)
    first_blk = jnp.min(jnp.where((picked > 0.5) & (blk >= 1), blk.astype(F32), float(LANES)))
    first_tile = jnp.clip(first_blk.astype(jnp.int32) // (SEQ_TK // NSA_BLOCK), 1, hi)
    o_sel = _tiled_attention(heads, ks_ref, first_tile, hi, sel_masks, *scratch, with_tile0=True)

    def win_masks(key0):
        dist = qpos_t - (key0 + lane)
        return [((dist >= 0) & (dist < NSA_WINDOW), dist.astype(F32))] * NSA_GROUPS

    lo = jnp.maximum(q0 - (NSA_WINDOW - 1), 0) // SEQ_TK
    o_win = _tiled_attention(heads, kw_ref, lo, hi, win_masks, *scratch)
    sig = jax.nn.sigmoid(gate_ref[...])
    for i, (g, h) in enumerate(order):
        c0 = (g * NSA_HPG + h) * 3
        rs = slice(i * tq, (i + 1) * tq)
        o = (sig[:, c0:c0 + 1] * o_cmp[g][h * tq:(h + 1) * tq] + sig[:, c0 + 1:c0 + 2] * o_sel[rs]
             + sig[:, c0 + 2:c0 + 3] * o_win[rs])
        o_ref[:, (g * NSA_HPG + h) * NSA_HD:(g * NSA_HPG + h + 1) * NSA_HD] = o.astype(o_ref.dtype)


def _nsa_seq(proj, ckv, b, t):
    tq = 128
    n_cmp = t // NSA_BLOCK
    n_blk = -(-t // NSA_BLOCK)
    nq = t // tq
    assert t % SEQ_TK == 0 and n_blk <= LANES
    return pl.pallas_call(
        functools.partial(_nsa_seq_kernel, tq=tq, n_cmp=n_cmp, n_blk=n_blk),
        scratch_shapes=[pltpu.VMEM((t // SEQ_TK, NSA_HEADS * tq, SEQ_TK), F32)]
        + [pltpu.VMEM((NSA_HEADS * tq, LANES), F32)] * 3,
        grid=(b, nq),
        in_specs=[pl.BlockSpec((tq, NSA_HEADS * NSA_HD), lambda i, j: (i * nq + j, C_QA // 1024)),
                  pl.BlockSpec((tq, LANES), lambda i, j: (i * nq + j, C_GN // LANES)),
                  pl.BlockSpec((1, 2, NSA_GROUPS, n_cmp, NSA_HD), lambda i, j: (i, 0, 0, 0, 0)),
                  pl.BlockSpec((t, KV_COLS), lambda i, j: (i, C_KVS // KV_COLS)),
                  pl.BlockSpec((t, KV_COLS), lambda i, j: (i, C_KVW // KV_COLS))],
        out_specs=pl.BlockSpec((tq, NSA_HEADS * NSA_HD), lambda i, j: (i * nq + j, 0)),
        out_shape=jax.ShapeDtypeStruct((b * t, NSA_HEADS * NSA_HD), BF16),
        compiler_params=_cparams("parallel", "arbitrary"),
        name="nsa_seq",
    )(proj, proj, ckv, proj, proj)


def _pad_rows(x, rows):
    if x.shape[0] == rows:
        return x
    return jnp.concatenate([x, jnp.zeros((rows - x.shape[0], x.shape[1]), x.dtype)], axis=0)


def _two_pass_attention(qs, tiles):
    scores = []
    for k, _, valid, bias in tiles:
        scores.append(jnp.where(valid, _mm_nt(qs, k) * NSA_SCALE - bias, NEG_BIG))
    m = jnp.max(functools.reduce(jnp.maximum, scores), axis=-1, keepdims=True)
    acc = jnp.zeros((qs.shape[0], NSA_HD), F32)
    lsum = jnp.zeros(scores[0].shape, F32)
    for s, (_, v, _, _) in zip(scores, tiles):
        p = jnp.where(s > 0.5 * NEG_BIG, jnp.exp(s - m), 0.0)
        lsum = lsum + p
        acc = acc + _mm(p, v)
    l = jnp.sum(lsum, axis=-1, keepdims=True)
    return acc / jnp.where(l > 0, l, 1.0)


STEP_SEQS = 2


def _nsa_step_kernel(pt_ref, *refs, n_seq, t, n_pages, n_win, **kw):
    del pt_ref
    q_ref, gate_ref, ckv_ref, ksn_ref, kwn_ref, win_ref, o_ref = refs[n_seq * n_pages:]
    win_rows = n_win * KV_SLOTS
    for s in range(n_seq):
        rows = pl.ds(s * t, t)
        _nsa_step_one(refs[s * n_pages:(s + 1) * n_pages], q_ref.at[rows], gate_ref.at[rows], ckv_ref.at[pl.ds(s, 1)],
                      ksn_ref.at[rows], kwn_ref.at[rows], win_ref.at[pl.ds(s * win_rows, win_rows)], o_ref.at[rows],
                      t=t, n_pages=n_pages, n_win=n_win, **kw)


def _nsa_step_one(pages, q_ref, gate_ref, ckv_ref, ksn_ref, kwn_ref, win_ref, o_ref,
                  *, t, n_pages, page, past_len, n_win, n_cmp, n_blk):
    lane = lax.broadcasted_iota(jnp.int32, (1, page), 1)
    blocks_per_page = page // NSA_BLOCK
    for g in range(NSA_GROUPS):
        qs = _stack_heads(q_ref, g)
        qpos, slope = _row_consts(t, g, past_len)
        o_cmp, p_c = _compressed_branch(qs, ckv_ref[0, 0, g], ckv_ref[0, 1, g], qpos, slope, n_cmp)
        imp = p_c[0:t]
        for h in range(1, NSA_HPG):
            imp = imp + p_c[h * t:(h + 1) * t]
        sel = _select_blocks(imp, past_len, n_blk)
        kcol = slice(g * NSA_HD, (g + 1) * NSA_HD)
        vcol = slice((NSA_GROUPS + g) * NSA_HD, (NSA_GROUPS + g + 1) * NSA_HD)
        kslot, vslot = g, NSA_GROUPS + g

        tiles = []
        for p in range(n_pages + 1):
            key0 = p * page
            if p < n_pages:
                k = pages[p][pl.ds(kslot, page, stride=KV_SLOTS), :]
                v = pages[p][pl.ds(vslot, page, stride=KV_SLOTS), :]
            else:
                k, v = _pad_rows(ksn_ref[:, kcol], page), _pad_rows(ksn_ref[:, vcol], page)
            chosen = jnp.zeros((t, page), F32)
            for bi in range(blocks_per_page):
                blk = p * blocks_per_page + bi
                chosen = jnp.where(lane // NSA_BLOCK == bi, sel[:, blk:blk + 1], chosen)
            chosen = jnp.concatenate([chosen] * NSA_HPG, axis=0)
            dist = qpos - (key0 + lane)
            tiles.append((k, v, (dist >= 0) & (chosen > 0.5), slope * dist.astype(F32)))
        o_sel = _two_pass_attention(qs, tiles)

        tiles = []
        for p in range(n_win // page + 1):
            key0 = past_len - n_win + p * page
            if p < n_win // page:
                k = win_ref[pl.ds(p * page * KV_SLOTS + kslot, page, stride=KV_SLOTS), :]
                v = win_ref[pl.ds(p * page * KV_SLOTS + vslot, page, stride=KV_SLOTS), :]
            else:
                k, v = _pad_rows(kwn_ref[:, kcol], page), _pad_rows(kwn_ref[:, vcol], page)
            dist = qpos - (key0 + lane)
            valid = (dist >= 0) & (dist < NSA_WINDOW) & (key0 + lane >= 0)
            tiles.append((k, v, valid, slope * dist.astype(F32)))
        o_win = _two_pass_attention(qs, tiles)
        _write_gated(o_ref, gate_ref, g, t, o_cmp, o_sel, o_win)


def _nsa_step(proj, ckv, cache2d, page_table_flat, win2d, b, t, n_pages, page, n_win):
    past_len = n_pages * page
    assert n_win % page == 0 and t <= page
    n_cmp = (past_len + t) // NSA_BLOCK
    n_blk = -(-(past_len + t) // NSA_BLOCK)
    assert n_cmp * NSA_BLOCK == past_len and n_blk <= LANES
    n_seq = STEP_SEQS if b % STEP_SEQS == 0 else 1
    page_spec = lambda s, p: pl.BlockSpec(
        (page * KV_SLOTS, NSA_HD), lambda i, pt: (pt[(i * n_seq + s) * n_pages + p], 0))
    grid_spec = pltpu.PrefetchScalarGridSpec(
        num_scalar_prefetch=1,
        grid=(b // n_seq,),
        in_specs=[page_spec(s, p) for s in range(n_seq) for p in range(n_pages)] + [
            pl.BlockSpec((n_seq * t, NSA_HEADS * NSA_HD), lambda i, pt: (i, C_QA // 1024)),
            pl.BlockSpec((n_seq * t, LANES), lambda i, pt: (i, C_GN // LANES)),
            pl.BlockSpec((n_seq, 2, NSA_GROUPS, n_cmp, NSA_HD), lambda i, pt: (i, 0, 0, 0, 0)),
            pl.BlockSpec((n_seq * t, KV_COLS), lambda i, pt: (i, C_KVS // KV_COLS)),
            pl.BlockSpec((n_seq * t, KV_COLS), lambda i, pt: (i, C_KVW // KV_COLS)),
            pl.BlockSpec((n_seq * n_win * KV_SLOTS, NSA_HD), lambda i, pt: (i, 0))],
        out_specs=pl.BlockSpec((n_seq * t, NSA_HEADS * NSA_HD), lambda i, pt: (i, 0)),
    )
    return pl.pallas_call(
        functools.partial(_nsa_step_kernel, n_seq=n_seq, t=t, n_pages=n_pages, page=page, past_len=past_len,
                          n_win=n_win, n_cmp=n_cmp, n_blk=n_blk),
        grid_spec=grid_spec,
        out_shape=jax.ShapeDtypeStruct((b * t, NSA_HEADS * NSA_HD), F32),
        compiler_params=_cparams("parallel"),
        name="nsa_step",
    )(page_table_flat, *([cache2d] * (n_seq * n_pages)), proj, proj, ckv, proj, proj, win2d)


HG_ROWS = 128


HG_SEQS = 2


def _hgrn_kernel(q_ref, f_ref, i_ref, og_ref, lb_ref, nw_ref, s0_ref, o_ref, sfin_ref, st_ref, oacc_ref,
                 *, n_seq, **kw):
    for s in range(n_seq):
        one = pl.ds(s, 1)
        _hgrn_one(q_ref.at[s], f_ref.at[s], i_ref.at[s], og_ref.at[s], lb_ref, nw_ref, s0_ref.at[one],
                  o_ref.at[s], sfin_ref.at[one], st_ref.at[s], oacc_ref.at[s], **kw)


def _hgrn_one(q_ref, f_ref, i_ref, og_ref, lb_ref, nw_ref, s0_ref, o_ref, sfin_ref,
              st_ref, oacc_ref, *, rows_in, n_tblk, has_state):
    tb = pl.program_id(1)

    @pl.when(tb == 0)
    def _():
        for h in range(HG_HEADS):
            if has_state:
                st_ref[h] = s0_ref[0, h].T
            else:
                st_ref[h] = jnp.zeros((HG_DV, HG_DK), F32)

    pr = HG_CHUNK if rows_in <= HG_CHUNK else HG_ROWS
    assert rows_in <= pr

    def padded(ref):
        x = ref[...]
        return x if rows_in == pr else _pad_rows(x, pr)

    def key_rows(x):
        return x if pr == HG_ROWS else _pad_rows(x, HG_ROWS)

    q, f, v = padded(q_ref), padded(f_ref), padded(i_ref)
    lb = lb_ref[...]
    row = lax.broadcasted_iota(jnp.int32, (pr, 1), 0)
    live = row < rows_in
    forget = lb + (1.0 - lb) * jax.nn.sigmoid(f)
    k = jnp.where(live, (1.0 - lb) * jax.nn.sigmoid(-f), 0.0)
    gl = jnp.where(live, jnp.log(forget), 0.0)
    rc = row % HG_CHUNK
    cum, suf = gl, gl
    s = 1
    while s < HG_CHUNK:
        cum = cum + jnp.where(rc >= s, pltpu.roll(cum, s, axis=0), 0.0)
        suf = suf + jnp.where(rc < HG_CHUNK - s, pltpu.roll(suf, pr - s, axis=0), 0.0)
        s *= 2
    ki = k * jnp.exp(-cum)
    qd = {HG_CHUNK: q * jnp.exp(cum)}
    ke = {HG_CHUNK: k * jnp.exp(suf - gl)}
    ci = lax.broadcasted_iota(jnp.int32, (pr, HG_ROWS), 0)
    cj = lax.broadcasted_iota(jnp.int32, (pr, HG_ROWS), 1)
    masks = {HG_CHUNK: (ci // HG_CHUNK == cj // HG_CHUNK) & (ci >= cj)}
    w = HG_CHUNK
    while w < pr:
        tot = cum + suf - gl
        odd = (row // w) % 2 == 1
        cum = cum + jnp.where(odd, pltpu.roll(tot, w, axis=0), 0.0)
        suf = suf + jnp.where(odd, 0.0, pltpu.roll(tot, pr - w, axis=0))
        masks[2 * w] = ((ci // w) % 2 == 1) & (cj // w == ci // w - 1)
        w *= 2
        qd[w] = q * jnp.exp(cum)
        ke[w] = k * jnp.exp(suf - gl)
    decay = jnp.exp((cum + suf - gl)[0:1, :])
    for h in range(HG_HEADS):
        hs = slice(h * HG_DK, (h + 1) * HG_DK)
        att = jnp.where(masks[HG_CHUNK], _mm_nt(qd[HG_CHUNK][:, hs], key_rows(ki[:, hs])), 0.0)
        w = HG_CHUNK
        while w < pr:
            att = att + jnp.where(masks[2 * w], _mm_nt(qd[w][:, hs], key_rows(ke[w][:, hs])), 0.0)
            w *= 2
        v_h = key_rows(v[:, hs])
        st = st_ref[h]
        oacc_ref[0:pr, hs] = _mm(att, v_h) + _mm_nt(qd[pr][:, hs], st)
        st_ref[h] = decay[:, hs] * st + _mm(v_h.T, key_rows(ke[pr][:, hs]))
    og = og_ref[...]
    nw = nw_ref[...]
    for h in range(HG_HEADS):
        hs = slice(h * HG_DV, (h + 1) * HG_DV)
        x = oacc_ref[0:rows_in, hs]
        ms = jnp.mean(x * x, axis=-1, keepdims=True)
        y = x * lax.rsqrt(ms + EPS) * nw
        o_ref[:, hs] = (y * jax.nn.silu(og[:, hs])).astype(o_ref.dtype)

    @pl.when(tb == n_tblk - 1)
    def _():
        for h in range(HG_HEADS):
            sfin_ref[0, h] = st_ref[h].T


def _hgrn(proj, lb, norm_w, s0, b, t):
    rows_in = min(t, HG_ROWS)
    n_tblk = t // rows_in
    assert rows_in * n_tblk == t and rows_in % 8 == 0
    has_state = s0 is not None
    n_seq = HG_SEQS if b % HG_SEQS == 0 else 1
    if s0 is None:
        s0 = jnp.zeros((n_seq, HG_HEADS, HG_DK, HG_DV), F32)
    width = HG_HEADS * HG_DK
    proj3 = proj.reshape(b, t, proj.shape[1])
    col = lambda c: pl.BlockSpec((n_seq, rows_in, width), lambda i, j: (i, j, c // width))
    state = (n_seq, HG_HEADS, HG_DK, HG_DV)
    o, s_fin = pl.pallas_call(
        functools.partial(_hgrn_kernel, n_seq=n_seq, rows_in=rows_in, n_tblk=n_tblk, has_state=has_state),
        grid=(b // n_seq, n_tblk),
        in_specs=[col(C_QB), col(C_FB), col(C_IB), col(C_OG),
                  pl.BlockSpec((1, width), lambda i, j: (0, 0)),
                  pl.BlockSpec((1, HG_DV), lambda i, j: (0, 0)),
                  pl.BlockSpec(state, (lambda i, j: (i, 0, 0, 0)) if has_state else (lambda i, j: (0, 0, 0, 0)))],
        out_specs=[pl.BlockSpec((n_seq, rows_in, width), lambda i, j: (i, j, 0)),
                   pl.BlockSpec(state, lambda i, j: (i, 0, 0, 0))],
        out_shape=[jax.ShapeDtypeStruct((b, t, width), BF16),
                   jax.ShapeDtypeStruct((b, HG_HEADS, HG_DK, HG_DV), F32)],
        scratch_shapes=[pltpu.VMEM((n_seq, HG_HEADS, HG_DV, HG_DK), F32), pltpu.VMEM((n_seq, HG_ROWS, width), F32)],
        compiler_params=_cparams("parallel", "arbitrary"),
        name="hgrn2",
    )(proj3, proj3, proj3, proj3, lb.reshape(1, width), norm_w.reshape(1, HG_DV), s0)
    return o.reshape(b * t, width), s_fin


def _decoder_layer(x, past, lb, params):
    (w_in, w_a, w_b, w_o, pe, w1, w2, hg_norm, ln1, ln2, ln3, ln4, w_up, cw, cb, w_dn) = params
    b, t, d = x.shape
    m = b * t
    x2 = x.reshape(m, d)
    tm = min(MM_ROWS, m)
    proj = _in_projection(x2, ln1, *w_in, tm)
    kv_shape = (b, t, 2, NSA_GROUPS, NSA_HD)
    kvc, kvs, kvw = (a.reshape(kv_shape) for a in _kv_relayout(proj, min(CONV_ROWS, m)))
    if past is None:
        ckv = _compress_seq(proj, b, t, pe, w1, w2)
        o_a = _nsa_seq(proj, ckv, b, t)
        new_win = kvw[:, t - min(NSA_WINDOW, t):]
        s0, conv_buf = None, None
    else:
        cache_c, cache_s, page_table_flat, n_pages, page, win_buf, s0, conv_buf = past
        n_win = win_buf.shape[1]
        ckv = _compress_paged(cache_c, page_table_flat, b, n_pages, page, pe, w1, w2)
        o_a = _nsa_step(proj, ckv, cache_s, page_table_flat, win_buf.reshape(-1, NSA_HD), b, t, n_pages, page, n_win)
        new_win = jnp.concatenate([win_buf, kvw], axis=1)[:, t:]
    o_hg, s_fin = _hgrn(proj, lb, hg_norm, s0, b, t)
    mixed = _gated_merge(o_a.astype(BF16), o_hg, w_a, w_b, proj, tm, 512)
    x1 = _matmul_norm_res(mixed, w_o, x2, ln2, tm, 512)
    u = _norm_matmul(x1, ln3, w_up, tm, UP_TN)
    if past is None:
        y = _conv_glu_down(u, cw, cb, w_dn, x1, ln4, t, min(CONV_ROWS, t), 512)
        new_conv = u.reshape(b, t, 2 * D_FF)[:, t - (CONV_W - 1):]
    else:
        act = _conv_glu_step(u.reshape(b, t, 2 * D_FF), conv_buf, cw, cb, min(64, b), 512).reshape(m, D_FF)
        new_conv = jnp.concatenate([conv_buf, u.reshape(b, t, 2 * D_FF)], axis=1)[:, t:] if t < CONV_W - 1 else \
            u.reshape(b, t, 2 * D_FF)[:, t - (CONV_W - 1):]
        y = _matmul_norm_res(act, w_dn, x1, ln4, tm, 512)
    return y.reshape(b, t, d), (kvc, kvs, new_win, s_fin, new_conv)


def _split_w_in(w):
    o_gate = NSA_HEADS * NSA_HD + 3 * KV_COLS
    o_rest = o_gate + 3 * NSA_HEADS
    gate = jnp.pad(w[:, o_gate:o_rest], ((0, 0), (0, PROJ_TN - 3 * NSA_HEADS)))
    return w[:, :o_gate].astype(BF16), w[:, o_rest:].astype(BF16), gate.astype(BF16)


def kernel(x_prompt, x_sample, cache_cmp_kv, cache_sel_kv, state_win_kv, state_hgrn, state_conv, page_table,
           w_in, w_branch_a, w_branch_b, w_out, cmp_pe, cmp_w1, cmp_w2, hg_lb_raw, hg_norm_w, ln_mix_pre,
           ln_mix_post, ln_ffn_pre, ln_ffn_post, w_up, conv_w, conv_b, w_down):
    depth = w_in.shape[0]
    dec_b, n_pages = page_table.shape
    page = cache_cmp_kv.shape[2]
    lb_all = jnp.cumsum(jax.nn.softmax(hg_lb_raw.astype(F32), axis=0), axis=0)
    pt_flat = page_table.reshape(-1).astype(jnp.int32)
    y_p, y_s = x_prompt, x_sample
    new_p, new_s = [], []
    for l in range(depth):
        params = (_split_w_in(w_in[l]), w_branch_a[l].astype(BF16), w_branch_b[l].astype(BF16),
                  w_out[l].astype(BF16), cmp_pe[l].transpose(1, 0, 2)[:, :, None, :],
                  cmp_w1[l].astype(BF16).transpose(1, 0, 2, 3).reshape(2, NSA_BLOCK // 2, 2 * NSA_HD, NSA_HD),
                  cmp_w2[l].astype(BF16), hg_norm_w[l], ln_mix_pre[l], ln_mix_post[l], ln_ffn_pre[l],
                  ln_ffn_post[l], w_up[l].astype(BF16), conv_w[l], conv_b[l], w_down[l].astype(BF16))
        y_p, st_p = _decoder_layer(y_p, None, lb_all[l], params)
        past = (cache_cmp_kv[l].reshape(-1, NSA_HD), cache_sel_kv[l].reshape(-1, NSA_HD),
                pt_flat, n_pages, page, state_win_kv[l], state_hgrn[l], state_conv[l])
        y_s, st_s = _decoder_layer(y_s, past, lb_all[l], params)
        new_p.append(st_p)
        new_s.append(st_s)

    def stack(group, i):
        return jnp.stack([st[i] for st in group], axis=0)

    return (y_p, y_s, stack(new_p, 0), stack(new_s, 0), stack(new_p, 1), stack(new_s, 1), stack(new_p, 2),
            stack(new_s, 2), stack(new_p, 3), stack(new_s, 3), stack(new_p, 4), stack(new_s, 4))
```

```python
import functools

import jax
import jax.numpy as jnp
from jax import lax
from jax.experimental import pallas as pl
from jax.experimental.pallas import tpu as pltpu

F32 = jnp.float32
BF16 = jnp.bfloat16

D_MODEL = 2048
NSA_HEADS = 8
NSA_GROUPS = 2
NSA_HPG = NSA_HEADS // NSA_GROUPS
NSA_HD = 128
NSA_BLOCK = 64
NSA_TOPN = 8
NSA_WINDOW = 512
NSA_SCALE = NSA_HD ** -0.5
FORCED_SCORE = NSA_HPG + 1.0
HG_HEADS = 8
HG_DK = 128
HG_DV = 128
HG_CHUNK = 16
D_FF = 5632
CONV_W = 3
EPS = 1e-6
KV_COLS = 2 * NSA_GROUPS * NSA_HD

C_QA = 0
C_QB = 1024
C_FB = 2048
C_IB = 3072
C_OG = 4096
C_GA = 5120
C_GB = 7168
C_KVC = 9216
C_KVS = 9728
C_KVW = 10240
C_GN = 10752

LANES = 128
MM_ROWS = 1024
UP_TN = 1024
CONV_ROWS = 512
DOWN_PARTS = 8
NEG_BIG = -1e30
VMEM_LIMIT = 56 * 1024 * 1024


def _cparams(*sem):
    return pltpu.CompilerParams(dimension_semantics=sem, vmem_limit_bytes=VMEM_LIMIT)


def _mm(a, b):
    return jnp.dot(a.astype(BF16), b.astype(BF16), preferred_element_type=F32)


def _mm_nt(a, b):
    return lax.dot_general(a.astype(BF16), b.astype(BF16), (((1,), (1,)), ((), ())),
                           preferred_element_type=F32)


def _norm_mm_kernel(x_ref, ln_ref, w_ref, o_ref, h_ref):
    @pl.when(pl.program_id(1) == 0)
    def _():
        x = x_ref[...]
        ms = jnp.mean(x * x, axis=-1, keepdims=True)
        h_ref[...] = (x * lax.rsqrt(ms + EPS) * ln_ref[...]).astype(BF16)

    o_ref[...] = _mm(h_ref[...], w_ref[...])


def _norm_matmul(x, ln, w, tm, tn):
    m, k = x.shape
    n = w.shape[1]
    return pl.pallas_call(
        _norm_mm_kernel,
        grid=(m // tm, n // tn),
        in_specs=[pl.BlockSpec((tm, k), lambda i, j: (i, 0)),
                  pl.BlockSpec((1, k), lambda i, j: (0, 0)),
                  pl.BlockSpec((k, tn), lambda i, j: (0, j))],
        out_specs=pl.BlockSpec((tm, tn), lambda i, j: (i, j)),
        out_shape=jax.ShapeDtypeStruct((m, n), F32),
        scratch_shapes=[pltpu.VMEM((tm, k), BF16)],
        compiler_params=_cparams("parallel", "arbitrary"),
        name="norm_matmul",
    )(x, ln.reshape(1, k), w)


PROJ_TN = 1024
TILE_REST0 = C_QB // PROJ_TN
TILE_KV0 = C_KVC // PROJ_TN
N_PROJ = C_KVC + 2 * PROJ_TN
EDGE_COLS = N_PROJ - (C_KVC - C_QB)


def _in_proj_kernel(x_ref, ln_ref, we_ref, wr_ref, o_ref, h_ref):
    j = pl.program_id(1)

    @pl.when(j == 0)
    def _():
        x = x_ref[...]
        ms = jnp.mean(x * x, axis=-1, keepdims=True)
        h_ref[...] = (x * lax.rsqrt(ms + EPS) * ln_ref[...]).astype(BF16)

    from_rest = (j >= TILE_REST0) & (j < TILE_KV0)

    @pl.when(from_rest)
    def _():
        o_ref[...] = _mm(h_ref[...], wr_ref[...])

    @pl.when(jnp.logical_not(from_rest))
    def _():
        o_ref[...] = _mm(h_ref[...], we_ref[...])


def _in_projection(x, ln, w_edge, w_rest, tm):
    m, k = x.shape
    tn = PROJ_TN
    n_rest = w_rest.shape[1] // tn
    assert n_rest == TILE_KV0 - TILE_REST0 and w_edge.shape[1] == EDGE_COLS and TILE_REST0 == 1
    edge_tile = lambda j: jnp.clip(j - (TILE_KV0 - TILE_REST0), 0, EDGE_COLS // tn - 1)
    rest_tile = lambda j: jnp.clip(j - TILE_REST0, 0, n_rest - 1)
    return pl.pallas_call(
        _in_proj_kernel,
        grid=(m // tm, N_PROJ // tn),
        in_specs=[pl.BlockSpec((tm, k), lambda i, j: (i, 0)),
                  pl.BlockSpec((1, k), lambda i, j: (0, 0)),
                  pl.BlockSpec((k, tn), lambda i, j: (0, edge_tile(j))),
                  pl.BlockSpec((k, tn), lambda i, j: (0, rest_tile(j)))],
        out_specs=pl.BlockSpec((tm, tn), lambda i, j: (i, j)),
        out_shape=jax.ShapeDtypeStruct((m, N_PROJ), F32),
        scratch_shapes=[pltpu.VMEM((tm, k), BF16)],
        compiler_params=_cparams("parallel", "arbitrary"),
        name="in_projection",
    )(x, ln.reshape(1, k), w_edge, w_rest)


def _kv_relayout_kernel(c_ref, s_ref, w_ref, oc_ref, os_ref, ow_ref, *, tm):
    for x_ref, o_ref in ((c_ref, oc_ref), (s_ref, os_ref), (w_ref, ow_ref)):
        for slot in range(2 * NSA_GROUPS):
            o_ref[pl.ds(slot, tm, stride=2 * NSA_GROUPS), :] = x_ref[:, slot * NSA_HD:(slot + 1) * NSA_HD]


def _kv_relayout(proj, tm):
    m = proj.shape[0]
    slots = 2 * NSA_GROUPS
    out = jax.ShapeDtypeStruct((m * slots, NSA_HD), F32)
    return pl.pallas_call(
        functools.partial(_kv_relayout_kernel, tm=tm),
        grid=(m // tm,),
        in_specs=[pl.BlockSpec((tm, KV_COLS), lambda i, c=c: (i, c // KV_COLS)) for c in (C_KVC, C_KVS, C_KVW)],
        out_specs=[pl.BlockSpec((tm * slots, NSA_HD), lambda i: (i, 0))] * 3,
        out_shape=[out] * 3,
        compiler_params=_cparams("parallel"),
        name="kv_relayout",
    )(proj, proj, proj)


def _mm_norm_res_kernel(a_ref, w_ref, res_ref, ln_ref, o_ref, acc_ref, *, nk):
    k = pl.program_id(1)

    @pl.when(k == 0)
    def _():
        acc_ref[...] = jnp.zeros_like(acc_ref)

    acc_ref[...] += _mm(a_ref[...], w_ref[...])

    @pl.when(k == nk - 1)
    def _():
        y = acc_ref[...]
        ms = jnp.mean(y * y, axis=-1, keepdims=True)
        o_ref[...] = res_ref[...] + y * lax.rsqrt(ms + EPS) * ln_ref[...]


def _matmul_norm_res(a, w, res, ln, tm, tk):
    m, kk = a.shape
    n = w.shape[1]
    nk = kk // tk
    return pl.pallas_call(
        functools.partial(_mm_norm_res_kernel, nk=nk),
        grid=(m // tm, nk),
        in_specs=[pl.BlockSpec((tm, tk), lambda i, k: (i, k)),
                  pl.BlockSpec((tk, n), lambda i, k: (k, 0)),
                  pl.BlockSpec((tm, n), lambda i, k: (i, 0)),
                  pl.BlockSpec((1, n), lambda i, k: (0, 0))],
        out_specs=pl.BlockSpec((tm, n), lambda i, k: (i, 0)),
        out_shape=jax.ShapeDtypeStruct((m, n), F32),
        scratch_shapes=[pltpu.VMEM((tm, n), F32)],
        compiler_params=_cparams("parallel", "arbitrary"),
        name="matmul_norm_res",
    )(a, w, res, ln.reshape(1, n))


def _merge_kernel(oa_ref, ohg_ref, wa_ref, wb_ref, ga_ref, gb_ref, o_ref):
    a = _mm(oa_ref[...], wa_ref[...])
    b = _mm(ohg_ref[...], wb_ref[...])
    o_ref[...] = (jax.nn.sigmoid(ga_ref[...]) * a + jax.nn.sigmoid(gb_ref[...]) * b).astype(BF16)


def _gated_merge(o_a, o_hg, w_a, w_b, proj, tm, tn):
    m, k = o_a.shape
    n = w_a.shape[1]
    return pl.pallas_call(
        _merge_kernel,
        grid=(m // tm, n // tn),
        in_specs=[pl.BlockSpec((tm, k), lambda i, j: (i, 0)),
                  pl.BlockSpec((tm, k), lambda i, j: (i, 0)),
                  pl.BlockSpec((k, tn), lambda i, j: (0, j)),
                  pl.BlockSpec((k, tn), lambda i, j: (0, j)),
                  pl.BlockSpec((tm, tn), lambda i, j: (i, C_GA // tn + j)),
                  pl.BlockSpec((tm, tn), lambda i, j: (i, C_GB // tn + j))],
        out_specs=pl.BlockSpec((tm, tn), lambda i, j: (i, j)),
        out_shape=jax.ShapeDtypeStruct((m, n), BF16),
        compiler_params=_cparams("parallel", "arbitrary"),
        name="gated_merge",
    )(o_a, o_hg, w_a, w_b, proj, proj)


def _conv_taps(fa, fg, cwa_ref, cwg_ref, cba_ref, cbg_ref, tap):
    ca = cba_ref[...]
    cg = cbg_ref[...]
    for j in range(CONV_W):
        ca = ca + tap(fa, j) * cwa_ref[j:j + 1, :]
        cg = cg + tap(fg, j) * cwg_ref[j:j + 1, :]
    return jax.nn.gelu(cg, approximate=True) * ca


def _conv_glu_down_kernel(ua_ref, ug_ref, ha_ref, hg_ref, cwa_ref, cwg_ref, cba_ref, cbg_ref, w_ref, res_ref,
                          ln_ref, o_ref, fa_ref, fg_ref, act_ref, acc_ref, *, tm, tiles_per_seq, nk):
    k = pl.program_id(1)
    slot = k % 2

    @pl.when(k == 0)
    def _():
        acc_ref[...] = jnp.zeros_like(acc_ref)
        act_ref[1] = jnp.zeros(act_ref.shape[1:], act_ref.dtype)

    first = (pl.program_id(0) % tiles_per_seq) == 0
    fa_ref[0:8, :] = jnp.where(first, 0.0, ha_ref[...])
    fg_ref[0:8, :] = jnp.where(first, 0.0, hg_ref[...])
    fa_ref[8:, :] = ua_ref[...]
    fg_ref[8:, :] = ug_ref[...]
    n_parts = DOWN_PARTS
    cols = acc_ref.shape[1] // n_parts
    rows = tm // n_parts
    for part in range(n_parts):
        cs = slice(part * cols, (part + 1) * cols)
        acc_ref[:, cs] += _mm(act_ref[1 - slot], w_ref[:, cs])
        r0 = part * rows
        xa, xg = fa_ref[r0:r0 + rows + 8, :], fg_ref[r0:r0 + rows + 8, :]
        tap = lambda x, j: (x if j == CONV_W - 1 else pltpu.roll(x, CONV_W - 1 - j, axis=0))[8:]
        act = _conv_taps(xa, xg, cwa_ref, cwg_ref, cba_ref, cbg_ref, tap)
        act_ref[slot, r0:r0 + rows, :] = act.astype(act_ref.dtype)

    @pl.when(k == nk)
    def _():
        y = acc_ref[...]
        ms = jnp.mean(y * y, axis=-1, keepdims=True)
        o_ref[...] = res_ref[...] + y * lax.rsqrt(ms + EPS) * ln_ref[...]


def _conv_glu_down(u, conv_w, conv_b, w, res, ln, seq_len, tm, tk):
    m = u.shape[0]
    n = w.shape[1]
    nk = D_FF // tk
    hb = tm // 8
    kc = lambda k: jnp.minimum(k, nk - 1)
    kw = lambda k: jnp.maximum(k - 1, 0)
    halo = lambda i, k, off: (jnp.maximum(i * hb - 1, 0), kc(k) + off)
    return pl.pallas_call(
        functools.partial(_conv_glu_down_kernel, tm=tm, tiles_per_seq=seq_len // tm, nk=nk),
        grid=(m // tm, nk + 1),
        in_specs=[pl.BlockSpec((tm, tk), lambda i, k: (i, kc(k))),
                  pl.BlockSpec((tm, tk), lambda i, k: (i, kc(k) + nk)),
                  pl.BlockSpec((8, tk), lambda i, k: halo(i, k, 0)),
                  pl.BlockSpec((8, tk), lambda i, k: halo(i, k, nk)),
                  pl.BlockSpec((CONV_W, tk), lambda i, k: (0, kc(k))),
                  pl.BlockSpec((CONV_W, tk), lambda i, k: (0, kc(k) + nk)),
                  pl.BlockSpec((1, tk), lambda i, k: (0, kc(k))),
                  pl.BlockSpec((1, tk), lambda i, k: (0, kc(k) + nk)),
                  pl.BlockSpec((tk, n), lambda i, k: (kw(k), 0)),
                  pl.BlockSpec((tm, n), lambda i, k: (i, 0)),
                  pl.BlockSpec((1, n), lambda i, k: (0, 0))],
        out_specs=pl.BlockSpec((tm, n), lambda i, k: (i, 0)),
        out_shape=jax.ShapeDtypeStruct((m, n), F32),
        scratch_shapes=[pltpu.VMEM((tm + 8, tk), F32), pltpu.VMEM((tm + 8, tk), F32),
                        pltpu.VMEM((2, tm, tk), BF16), pltpu.VMEM((tm, n), F32)],
        compiler_params=_cparams("parallel", "arbitrary"),
        name="conv_glu_down",
    )(u, u, u, u, conv_w, conv_w, conv_b.reshape(1, -1), conv_b.reshape(1, -1), w, res, ln.reshape(1, n))


def _conv_glu_step_kernel(ua_ref, ug_ref, ba_ref, bg_ref, cwa_ref, cwg_ref, cba_ref, cbg_ref,
                          o_ref, fa_ref, fg_ref, *, t):
    fa_ref[:, 8 - (CONV_W - 1):8, :] = ba_ref[...]
    fg_ref[:, 8 - (CONV_W - 1):8, :] = bg_ref[...]
    fa_ref[:, 8:, :] = ua_ref[...]
    fg_ref[:, 8:, :] = ug_ref[...]
    tap = lambda f, j: f[:, 8 - (CONV_W - 1) + j:8 - (CONV_W - 1) + j + t, :]
    o_ref[...] = _conv_taps(fa_ref, fg_ref, cwa_ref, cwg_ref, cba_ref, cbg_ref, tap)


def _conv_glu_step(u3, buf, conv_w, conv_b, nb, tn):
    b, t, _ = u3.shape
    nj = D_FF // tn
    return pl.pallas_call(
        functools.partial(_conv_glu_step_kernel, t=t),
        grid=(b // nb, nj),
        in_specs=[pl.BlockSpec((nb, t, tn), lambda i, j: (i, 0, j)),
                  pl.BlockSpec((nb, t, tn), lambda i, j: (i, 0, j + nj)),
                  pl.BlockSpec((nb, CONV_W - 1, tn), lambda i, j: (i, 0, j)),
                  pl.BlockSpec((nb, CONV_W - 1, tn), lambda i, j: (i, 0, j + nj)),
                  pl.BlockSpec((CONV_W, tn), lambda i, j: (0, j)),
                  pl.BlockSpec((CONV_W, tn), lambda i, j: (0, j + nj)),
                  pl.BlockSpec((1, tn), lambda i, j: (0, j)),
                  pl.BlockSpec((1, tn), lambda i, j: (0, j + nj))],
        out_specs=pl.BlockSpec((nb, t, tn), lambda i, j: (i, 0, j)),
        out_shape=jax.ShapeDtypeStruct((b, t, D_FF), F32),
        scratch_shapes=[pltpu.VMEM((nb, 8 + t, tn), F32), pltpu.VMEM((nb, 8 + t, tn), F32)],
        compiler_params=_cparams("parallel", "arbitrary"),
        name="conv_glu_step",
    )(u3, u3, buf, buf, conv_w, conv_w, conv_b.reshape(1, -1), conv_b.reshape(1, -1))


def _compress_body(load_x, pe_ref, w1_ref, w2_ref, o_ref, n_seq, n_cmp):
    for c in range(2):
        acc = jnp.zeros((NSA_GROUPS * n_seq * n_cmp, NSA_HD), F32)
        for l in range(0, NSA_BLOCK, 2):
            halves = [jnp.concatenate([load_x(l + dl, c, g) for g in range(NSA_GROUPS)], axis=0) + pe_ref[c, l + dl]
                      for dl in range(2)]
            acc = acc + _mm(jnp.concatenate(halves, axis=1), w1_ref[c, l // 2])
        out = _mm(jax.nn.silu(acc), w2_ref[c])
        for g in range(NSA_GROUPS):
            for s in range(n_seq):
                r0 = (g * n_seq + s) * n_cmp
                o_ref[s, c, g] = out[r0:r0 + n_cmp]


def _compress_seq_kernel(x0_ref, x1_ref, x2_ref, x3_ref, pe_ref, w1_ref, w2_ref, o_ref, *, n_cmp):
    xs = (x0_ref, x1_ref, x2_ref, x3_ref)
    load_x = lambda l, c, g: xs[c * NSA_GROUPS + g][pl.ds(l, n_cmp, stride=NSA_BLOCK), :]
    _compress_body(load_x, pe_ref, w1_ref, w2_ref, o_ref, 1, n_cmp)


def _compress_seq(proj, b, t, pe, w1, w2):
    n_cmp = t // NSA_BLOCK
    rows = n_cmp * NSA_BLOCK
    assert rows == t
    return pl.pallas_call(
        functools.partial(_compress_seq_kernel, n_cmp=n_cmp),
        grid=(b,),
        in_specs=[pl.BlockSpec((rows, NSA_HD), lambda i, cg=cg: (i, C_KVC // NSA_HD + cg))
                  for cg in range(2 * NSA_GROUPS)] + [
                  pl.BlockSpec(pe.shape, lambda i: (0, 0, 0, 0)),
                  pl.BlockSpec(w1.shape, lambda i: (0, 0, 0, 0)),
                  pl.BlockSpec(w2.shape, lambda i: (0, 0, 0))],
        out_specs=pl.BlockSpec((1, 2, NSA_GROUPS, n_cmp, NSA_HD), lambda i: (i, 0, 0, 0, 0)),
        out_shape=jax.ShapeDtypeStruct((b, 2, NSA_GROUPS, n_cmp, NSA_HD), F32),
        compiler_params=_cparams("parallel"),
        name="compress_seq",
    )(proj, proj, proj, proj, pe, w1, w2)


KV_SLOTS = 2 * NSA_GROUPS
CMP_SEQS = 2
CMP_PITCH = NSA_BLOCK * KV_SLOTS + 8


def _compress_paged_kernel(pt_ref, *refs, n_seq, n_pages, page_rows):
    del pt_ref
    pages = refs[:n_seq * n_pages]
    pe_ref, w1_ref, w2_ref, o_ref, x_ref = refs[n_seq * n_pages:]
    blk_rows = NSA_BLOCK * KV_SLOTS
    per_page = page_rows // blk_rows
    for i, pg in enumerate(pages):
        for j in range(per_page):
            n = i * per_page + j
            x_ref[n * CMP_PITCH:n * CMP_PITCH + blk_rows, :] = pg[j * blk_rows:(j + 1) * blk_rows, :]
    n_cmp = n_pages * per_page
    load_x = lambda l, c, g: x_ref[pl.ds(l * KV_SLOTS + c * NSA_GROUPS + g, n_seq * n_cmp, stride=CMP_PITCH), :]
    _compress_body(load_x, pe_ref, w1_ref, w2_ref, o_ref, n_seq, n_cmp)


def _compress_paged(cache2d, page_table_flat, b, n_pages, page, pe, w1, w2):
    page_rows = page * KV_SLOTS
    n_cmp = n_pages * page // NSA_BLOCK
    n_seq = CMP_SEQS if b % CMP_SEQS == 0 else 1
    page_spec = lambda s, p: pl.BlockSpec(
        (page_rows, NSA_HD), lambda i, pt: (pt[(i * n_seq + s) * n_pages + p], 0))
    grid_spec = pltpu.PrefetchScalarGridSpec(
        num_scalar_prefetch=1,
        grid=(b // n_seq,),
        in_specs=[page_spec(s, p) for s in range(n_seq) for p in range(n_pages)] + [
            pl.BlockSpec(pe.shape, lambda i, pt: (0, 0, 0, 0)),
            pl.BlockSpec(w1.shape, lambda i, pt: (0, 0, 0, 0)),
            pl.BlockSpec(w2.shape, lambda i, pt: (0, 0, 0))],
        out_specs=pl.BlockSpec((n_seq, 2, NSA_GROUPS, n_cmp, NSA_HD), lambda i, pt: (i, 0, 0, 0, 0)),
        scratch_shapes=[pltpu.VMEM((n_seq * n_cmp * CMP_PITCH, NSA_HD), F32)],
    )
    return pl.pallas_call(
        functools.partial(_compress_paged_kernel, n_seq=n_seq, n_pages=n_pages, page_rows=page_rows),
        grid_spec=grid_spec,
        out_shape=jax.ShapeDtypeStruct((b, 2, NSA_GROUPS, n_cmp, NSA_HD), F32),
        compiler_params=_cparams("parallel"),
        name="compress_paged",
    )(page_table_flat, *([cache2d] * (n_seq * n_pages)), pe, w1, w2)


def _alibi_slope(g, h):
    return 2.0 ** (-8.0 * (g * NSA_HPG + h + 1.0) / NSA_HEADS)


def _row_consts(tq, g, q_pos0):
    rows = NSA_HPG * tq
    r = lax.broadcasted_iota(jnp.int32, (rows, 1), 0)
    head = r // tq
    qpos = q_pos0 + (r - head * tq)
    slope = jnp.zeros((rows, 1), F32)
    for h in range(NSA_HPG):
        slope = jnp.where(head == h, _alibi_slope(g, h), slope)
    return qpos, slope


def _stack_heads(q_ref, g):
    return jnp.concatenate(
        [q_ref[:, (g * NSA_HPG + h) * NSA_HD:(g * NSA_HPG + h + 1) * NSA_HD] for h in range(NSA_HPG)], axis=0)


def _compressed_branch(qs, ck, cv, qpos, slope, n_cmp):
    pad = jnp.zeros((LANES - n_cmp, NSA_HD), F32)
    ckp = jnp.concatenate([ck, pad], axis=0)
    cvp = jnp.concatenate([cv, pad], axis=0)
    n = lax.broadcasted_iota(jnp.int32, (1, LANES), 1)
    dist = qpos - ((n + 1) * NSA_BLOCK - 1)
    s = _mm_nt(qs, ckp) * NSA_SCALE - slope * dist.astype(F32)
    mask = (dist >= 0) & (n < n_cmp)
    s = jnp.where(mask, s, NEG_BIG)
    m = jnp.max(s, axis=-1, keepdims=True)
    e = jnp.where(mask, jnp.exp(s - m), 0.0)
    d = jnp.sum(e, axis=-1, keepdims=True)
    p = e / jnp.where(d > 0, d, 1.0)
    return _mm(p, cvp), p


def _select_blocks(imp, q_pos0, n_blk):
    tq = imp.shape[0]
    nb = -(-n_blk // 8) * 8
    imp_t = (imp if tq == LANES else _pad_rows(imp, LANES)).T[0:nb]
    blk = lax.broadcasted_iota(jnp.int32, (nb, LANES), 0)
    cur = (q_pos0 + lax.broadcasted_iota(jnp.int32, (nb, LANES), 1)) // NSA_BLOCK
    forced = (blk == 0) | (blk == cur) | (blk == cur - 1)
    valid = blk <= cur
    score = jnp.where(valid, jnp.where(forced, FORCED_SCORE, imp_t), -1.0)
    score = jnp.where(blk < n_blk, score, -2.0)
    blk_f = blk.astype(F32)
    sel = jnp.zeros((nb, LANES), F32)
    for _ in range(min(NSA_TOPN, n_blk)):
        mx = jnp.max(score, axis=0, keepdims=True)
        first = jnp.min(jnp.where(score == mx, blk_f, 1e9), axis=0, keepdims=True)
        hit = blk_f == first
        sel = jnp.where(hit, 1.0, sel)
        score = jnp.where(hit, -3.0, score)
    return _pad_rows(sel, LANES).T[0:tq]


def _expand_sel(sel, key0, nkeys):
    bi = lax.broadcasted_iota(jnp.int32, (LANES, nkeys), 0)
    ki = lax.broadcasted_iota(jnp.int32, (LANES, nkeys), 1)
    expand = (bi == (key0 + ki) // NSA_BLOCK).astype(BF16)
    return _mm(sel, expand)


def _write_gated(o_ref, gate_ref, g, tq, o_cmp, o_sel, o_win):
    sig = jax.nn.sigmoid(gate_ref[...])
    for h in range(NSA_HPG):
        c0 = (g * NSA_HPG + h) * 3
        rs = slice(h * tq, (h + 1) * tq)
        o = sig[:, c0:c0 + 1] * o_cmp[rs] + sig[:, c0 + 1:c0 + 2] * o_sel[rs] + sig[:, c0 + 2:c0 + 3] * o_win[rs]
        o_ref[:, (g * NSA_HPG + h) * NSA_HD:(g * NSA_HPG + h + 1) * NSA_HD] = o.astype(o_ref.dtype)


SEQ_TK = 256


def _lane_groups(x):
    return [x[:, i:i + LANES] for i in range(0, x.shape[1], LANES)]


def _tiled_attention(heads, kv_ref, lo, hi, tile_masks, s_ref, m_ref, l_ref, acc_ref, with_tile0=False):
    tq = m_ref.shape[0] // len(heads)
    head_rows = [slice(i * tq, (i + 1) * tq) for i in range(len(heads))]
    groups = sorted({g for _, _, g in heads})
    kcol = lambda g: slice(g * NSA_HD, (g + 1) * NSA_HD)
    vcol = lambda g: slice((NSA_GROUPS + g) * NSA_HD, (NSA_GROUPS + g + 1) * NSA_HD)
    m_ref[...] = jnp.full(m_ref.shape, NEG_BIG, F32)

    def scores(kt, carry):
        key0 = kt * SEQ_TK
        masks = tile_masks(key0)
        rows = pl.ds(pl.multiple_of(key0, SEQ_TK), SEQ_TK)
        k = {g: kv_ref[rows, kcol(g)] for g in groups}
        kpos = (key0 + lax.broadcasted_iota(jnp.int32, (1, SEQ_TK), 1)).astype(F32)
        for (load_q, slope, g), rs in zip(heads, head_rows):
            valid = masks[g]
            s = jnp.where(valid, _mm_nt(load_q(), k[g]) * NSA_SCALE + slope * kpos, NEG_BIG)
            s_ref[kt, rs] = s
            m_ref[rs] = functools.reduce(jnp.maximum, [m_ref[rs]] + _lane_groups(s))
        return carry

    if with_tile0:
        scores(jnp.int32(0), 0)
    lax.fori_loop(lo, hi, scores, 0)
    for rs in head_rows:
        m = jnp.maximum(jnp.max(m_ref[rs], axis=-1, keepdims=True), 0.1 * NEG_BIG)
        m_ref[rs] = jnp.broadcast_to(m, (tq, LANES))
    l_ref[...] = jnp.zeros(l_ref.shape, F32)
    acc_ref[...] = jnp.zeros(acc_ref.shape, F32)

    def values(kt, carry):
        rows = pl.ds(pl.multiple_of(kt * SEQ_TK, SEQ_TK), SEQ_TK)
        v = {g: kv_ref[rows, vcol(g)] for g in groups}
        for (_, _, g), rs in zip(heads, head_rows):
            m = m_ref[rs]
            p = [jnp.exp(s - m) for s in _lane_groups(s_ref[kt, rs])]
            l_ref[rs] += sum(p[1:], p[0])
            acc_ref[rs] += _mm(jnp.concatenate(p, axis=1), v[g])
        return carry

    if with_tile0:
        values(jnp.int32(0), 0)
    lax.fori_loop(lo, hi, values, 0)
    outs = []
    for rs in head_rows:
        l = jnp.sum(l_ref[rs], axis=-1, keepdims=True)
        outs.append(acc_ref[rs] / jnp.where(l > 0, l, 1.0))
    return jnp.concatenate(outs, axis=0)


def _nsa_seq_kernel(q_ref, gate_ref, ckv_ref, ks_ref, kw_ref, o_ref, s_ref, m_ref, l_ref, acc_ref,
                    *, tq, n_cmp, n_blk):
    j = pl.program_id(1)
    q0 = j * tq
    lane = lax.broadcasted_iota(jnp.int32, (1, SEQ_TK), 1)
    qpos_t = q0 + lax.broadcasted_iota(jnp.int32, (tq, 1), 0)
    hi = (q0 + tq - 1) // SEQ_TK + 1
    o_cmp, sel = [], []
    for g in range(NSA_GROUPS):
        qs = _stack_heads(q_ref, g)
        qpos, slope = _row_consts(tq, g, q0)
        o_g, p_c = _compressed_branch(qs, ckv_ref[0, 0, g], ckv_ref[0, 1, g], qpos, slope, n_cmp)
        imp = p_c[0:tq]
        for h in range(1, NSA_HPG):
            imp = imp + p_c[h * tq:(h + 1) * tq]
        o_cmp.append(o_g)
        sel.append(_select_blocks(imp, q0, n_blk))
    order = [(g, h) for h in range(NSA_HPG) for g in range(NSA_GROUPS)]
    heads = [(lambda c=(g * NSA_HPG + h) * NSA_HD: q_ref[:, c:c + NSA_HD], _alibi_slope(g, h), g) for g, h in order]
    scratch = (s_ref, m_ref, l_ref, acc_ref)

    def sel_masks(key0):
        dist = qpos_t - (key0 + lane)
        return [(dist >= 0) & (_expand_sel(sel[g], key0, SEQ_TK) > 0.5) for g in range(NSA_GROUPS)]

    picked = functools.reduce(jnp.maximum, sel)
    blk = lax.broadcasted_iota(jnp.int32, picked.shape, 1)
    first_blk = jnp.min(jnp.where((picked > 0.5) & (blk >= 1), blk.astype(F32), float(LANES)))
    first_tile = jnp.clip(first_blk.astype(jnp.int32) // (SEQ_TK // NSA_BLOCK), 1, hi)
    o_sel = _tiled_attention(heads, ks_ref, first_tile, hi, sel_masks, *scratch, with_tile0=True)

    def win_masks(key0):
        dist = qpos_t - (key0 + lane)
        return [(dist >= 0) & (dist < NSA_WINDOW)] * NSA_GROUPS

    lo = jnp.maximum(q0 - (NSA_WINDOW - 1), 0) // SEQ_TK
    o_win = _tiled_attention(heads, kw_ref, lo, hi, win_masks, *scratch)
    sig = jax.nn.sigmoid(gate_ref[...])
    for i, (g, h) in enumerate(order):
        c0 = (g * NSA_HPG + h) * 3
        rs = slice(i * tq, (i + 1) * tq)
        o = (sig[:, c0:c0 + 1] * o_cmp[g][h * tq:(h + 1) * tq] + sig[:, c0 + 1:c0 + 2] * o_sel[rs]
             + sig[:, c0 + 2:c0 + 3] * o_win[rs])
        o_ref[:, (g * NSA_HPG + h) * NSA_HD:(g * NSA_HPG + h + 1) * NSA_HD] = o.astype(o_ref.dtype)


def _nsa_seq(proj, ckv, b, t):
    tq = 128
    n_cmp = t // NSA_BLOCK
    n_blk = -(-t // NSA_BLOCK)
    nq = t // tq
    assert t % SEQ_TK == 0 and n_blk <= LANES
    return pl.pallas_call(
        functools.partial(_nsa_seq_kernel, tq=tq, n_cmp=n_cmp, n_blk=n_blk),
        scratch_shapes=[pltpu.VMEM((t // SEQ_TK, NSA_HEADS * tq, SEQ_TK), F32)]
        + [pltpu.VMEM((NSA_HEADS * tq, LANES), F32)] * 3,
        grid=(b, nq),
        in_specs=[pl.BlockSpec((tq, NSA_HEADS * NSA_HD), lambda i, j: (i * nq + j, C_QA // 1024)),
                  pl.BlockSpec((tq, LANES), lambda i, j: (i * nq + j, C_GN // LANES)),
                  pl.BlockSpec((1, 2, NSA_GROUPS, n_cmp, NSA_HD), lambda i, j: (i, 0, 0, 0, 0)),
                  pl.BlockSpec((t, KV_COLS), lambda i, j: (i, C_KVS // KV_COLS)),
                  pl.BlockSpec((t, KV_COLS), lambda i, j: (i, C_KVW // KV_COLS))],
        out_specs=pl.BlockSpec((tq, NSA_HEADS * NSA_HD), lambda i, j: (i * nq + j, 0)),
        out_shape=jax.ShapeDtypeStruct((b * t, NSA_HEADS * NSA_HD), BF16),
        compiler_params=_cparams("parallel", "arbitrary"),
        name="nsa_seq",
    )(proj, proj, ckv, proj, proj)


def _pad_rows(x, rows):
    if x.shape[0] == rows:
        return x
    return jnp.concatenate([x, jnp.zeros((rows - x.shape[0], x.shape[1]), x.dtype)], axis=0)


def _two_pass_attention(qs, tiles):
    scores = []
    for k, _, valid, bias in tiles:
        scores.append(jnp.where(valid, _mm_nt(qs, k) * NSA_SCALE - bias, NEG_BIG))
    m = jnp.max(functools.reduce(jnp.maximum, scores), axis=-1, keepdims=True)
    acc = jnp.zeros((qs.shape[0], NSA_HD), F32)
    lsum = jnp.zeros(scores[0].shape, F32)
    for s, (_, v, _, _) in zip(scores, tiles):
        p = jnp.where(s > 0.5 * NEG_BIG, jnp.exp(s - m), 0.0)
        lsum = lsum + p
        acc = acc + _mm(p, v)
    l = jnp.sum(lsum, axis=-1, keepdims=True)
    return acc / jnp.where(l > 0, l, 1.0)


STEP_SEQS = 2


def _nsa_step_kernel(pt_ref, *refs, n_seq, t, n_pages, n_win, **kw):
    del pt_ref
    q_ref, gate_ref, ckv_ref, ksn_ref, kwn_ref, win_ref, o_ref = refs[n_seq * n_pages:]
    win_rows = n_win * KV_SLOTS
    for s in range(n_seq):
        rows = pl.ds(s * t, t)
        _nsa_step_one(refs[s * n_pages:(s + 1) * n_pages], q_ref.at[rows], gate_ref.at[rows], ckv_ref.at[pl.ds(s, 1)],
                      ksn_ref.at[rows], kwn_ref.at[rows], win_ref.at[pl.ds(s * win_rows, win_rows)], o_ref.at[rows],
                      t=t, n_pages=n_pages, n_win=n_win, **kw)


def _nsa_step_one(pages, q_ref, gate_ref, ckv_ref, ksn_ref, kwn_ref, win_ref, o_ref,
                  *, t, n_pages, page, past_len, n_win, n_cmp, n_blk):
    lane = lax.broadcasted_iota(jnp.int32, (1, page), 1)
    blocks_per_page = page // NSA_BLOCK
    for g in range(NSA_GROUPS):
        qs = _stack_heads(q_ref, g)
        qpos, slope = _row_consts(t, g, past_len)
        o_cmp, p_c = _compressed_branch(qs, ckv_ref[0, 0, g], ckv_ref[0, 1, g], qpos, slope, n_cmp)
        imp = p_c[0:t]
        for h in range(1, NSA_HPG):
            imp = imp + p_c[h * t:(h + 1) * t]
        sel = _select_blocks(imp, past_len, n_blk)
        kcol = slice(g * NSA_HD, (g + 1) * NSA_HD)
        vcol = slice((NSA_GROUPS + g) * NSA_HD, (NSA_GROUPS + g + 1) * NSA_HD)
        kslot, vslot = g, NSA_GROUPS + g

        tiles = []
        for p in range(n_pages + 1):
            key0 = p * page
            if p < n_pages:
                k = pages[p][pl.ds(kslot, page, stride=KV_SLOTS), :]
                v = pages[p][pl.ds(vslot, page, stride=KV_SLOTS), :]
            else:
                k, v = _pad_rows(ksn_ref[:, kcol], page), _pad_rows(ksn_ref[:, vcol], page)
            chosen = jnp.zeros((t, page), F32)
            for bi in range(blocks_per_page):
                blk = p * blocks_per_page + bi
                chosen = jnp.where(lane // NSA_BLOCK == bi, sel[:, blk:blk + 1], chosen)
            chosen = jnp.concatenate([chosen] * NSA_HPG, axis=0)
            dist = qpos - (key0 + lane)
            tiles.append((k, v, (dist >= 0) & (chosen > 0.5), slope * dist.astype(F32)))
        o_sel = _two_pass_attention(qs, tiles)

        tiles = []
        for p in range(n_win // page + 1):
            key0 = past_len - n_win + p * page
            if p < n_win // page:
                k = win_ref[pl.ds(p * page * KV_SLOTS + kslot, page, stride=KV_SLOTS), :]
                v = win_ref[pl.ds(p * page * KV_SLOTS + vslot, page, stride=KV_SLOTS), :]
            else:
                k, v = _pad_rows(kwn_ref[:, kcol], page), _pad_rows(kwn_ref[:, vcol], page)
            dist = qpos - (key0 + lane)
            valid = (dist >= 0) & (dist < NSA_WINDOW) & (key0 + lane >= 0)
            tiles.append((k, v, valid, slope * dist.astype(F32)))
        o_win = _two_pass_attention(qs, tiles)
        _write_gated(o_ref, gate_ref, g, t, o_cmp, o_sel, o_win)


def _nsa_step(proj, ckv, cache2d, page_table_flat, win2d, b, t, n_pages, page, n_win):
    past_len = n_pages * page
    assert n_win % page == 0 and t <= page
    n_cmp = (past_len + t) // NSA_BLOCK
    n_blk = -(-(past_len + t) // NSA_BLOCK)
    assert n_cmp * NSA_BLOCK == past_len and n_blk <= LANES
    n_seq = STEP_SEQS if b % STEP_SEQS == 0 else 1
    page_spec = lambda s, p: pl.BlockSpec(
        (page * KV_SLOTS, NSA_HD), lambda i, pt: (pt[(i * n_seq + s) * n_pages + p], 0))
    grid_spec = pltpu.PrefetchScalarGridSpec(
        num_scalar_prefetch=1,
        grid=(b // n_seq,),
        in_specs=[page_spec(s, p) for s in range(n_seq) for p in range(n_pages)] + [
            pl.BlockSpec((n_seq * t, NSA_HEADS * NSA_HD), lambda i, pt: (i, C_QA // 1024)),
            pl.BlockSpec((n_seq * t, LANES), lambda i, pt: (i, C_GN // LANES)),
            pl.BlockSpec((n_seq, 2, NSA_GROUPS, n_cmp, NSA_HD), lambda i, pt: (i, 0, 0, 0, 0)),
            pl.BlockSpec((n_seq * t, KV_COLS), lambda i, pt: (i, C_KVS // KV_COLS)),
            pl.BlockSpec((n_seq * t, KV_COLS), lambda i, pt: (i, C_KVW // KV_COLS)),
            pl.BlockSpec((n_seq * n_win * KV_SLOTS, NSA_HD), lambda i, pt: (i, 0))],
        out_specs=pl.BlockSpec((n_seq * t, NSA_HEADS * NSA_HD), lambda i, pt: (i, 0)),
    )
    return pl.pallas_call(
        functools.partial(_nsa_step_kernel, n_seq=n_seq, t=t, n_pages=n_pages, page=page, past_len=past_len,
                          n_win=n_win, n_cmp=n_cmp, n_blk=n_blk),
        grid_spec=grid_spec,
        out_shape=jax.ShapeDtypeStruct((b * t, NSA_HEADS * NSA_HD), F32),
        compiler_params=_cparams("parallel"),
        name="nsa_step",
    )(page_table_flat, *([cache2d] * (n_seq * n_pages)), proj, proj, ckv, proj, proj, win2d)


HG_ROWS = 128


HG_SEQS = 2


def _hgrn_kernel(q_ref, f_ref, i_ref, og_ref, lb_ref, nw_ref, s0_ref, o_ref, sfin_ref, st_ref, oacc_ref,
                 *, n_seq, **kw):
    for s in range(n_seq):
        one = pl.ds(s, 1)
        _hgrn_one(q_ref.at[s], f_ref.at[s], i_ref.at[s], og_ref.at[s], lb_ref, nw_ref, s0_ref.at[one],
                  o_ref.at[s], sfin_ref.at[one], st_ref.at[s], oacc_ref.at[s], **kw)


def _hgrn_one(q_ref, f_ref, i_ref, og_ref, lb_ref, nw_ref, s0_ref, o_ref, sfin_ref,
              st_ref, oacc_ref, *, rows_in, n_tblk, has_state):
    tb = pl.program_id(1)

    @pl.when(tb == 0)
    def _():
        for h in range(HG_HEADS):
            if has_state:
                st_ref[h] = s0_ref[0, h].T
            else:
                st_ref[h] = jnp.zeros((HG_DV, HG_DK), F32)

    pr = HG_CHUNK if rows_in <= HG_CHUNK else HG_ROWS
    assert rows_in <= pr

    def padded(ref):
        x = ref[...]
        return x if rows_in == pr else _pad_rows(x, pr)

    def key_rows(x):
        return x if pr == HG_ROWS else _pad_rows(x, HG_ROWS)

    q, f, v = padded(q_ref), padded(f_ref), padded(i_ref)
    lb = lb_ref[...]
    row = lax.broadcasted_iota(jnp.int32, (pr, 1), 0)
    live = row < rows_in
    forget = lb + (1.0 - lb) * jax.nn.sigmoid(f)
    k = jnp.where(live, (1.0 - lb) * jax.nn.sigmoid(-f), 0.0)
    gl = jnp.where(live, jnp.log(forget), 0.0)
    rc = row % HG_CHUNK
    cum, suf = gl, gl
    s = 1
    while s < HG_CHUNK:
        cum = cum + jnp.where(rc >= s, pltpu.roll(cum, s, axis=0), 0.0)
        suf = suf + jnp.where(rc < HG_CHUNK - s, pltpu.roll(suf, pr - s, axis=0), 0.0)
        s *= 2
    ki = k * jnp.exp(-cum)
    qd = {HG_CHUNK: q * jnp.exp(cum)}
    ke = {HG_CHUNK: k * jnp.exp(suf - gl)}
    ci = lax.broadcasted_iota(jnp.int32, (pr, HG_ROWS), 0)
    cj = lax.broadcasted_iota(jnp.int32, (pr, HG_ROWS), 1)
    masks = {HG_CHUNK: (ci // HG_CHUNK == cj // HG_CHUNK) & (ci >= cj)}
    w = HG_CHUNK
    while w < pr:
        tot = cum + suf - gl
        odd = (row // w) % 2 == 1
        cum = cum + jnp.where(odd, pltpu.roll(tot, w, axis=0), 0.0)
        suf = suf + jnp.where(odd, 0.0, pltpu.roll(tot, pr - w, axis=0))
        masks[2 * w] = ((ci // w) % 2 == 1) & (cj // w == ci // w - 1)
        w *= 2
        qd[w] = q * jnp.exp(cum)
        ke[w] = k * jnp.exp(suf - gl)
    decay = jnp.exp((cum + suf - gl)[0:1, :])
    for h in range(HG_HEADS):
        hs = slice(h * HG_DK, (h + 1) * HG_DK)
        att = jnp.where(masks[HG_CHUNK], _mm_nt(qd[HG_CHUNK][:, hs], key_rows(ki[:, hs])), 0.0)
        w = HG_CHUNK
        while w < pr:
            att = att + jnp.where(masks[2 * w], _mm_nt(qd[w][:, hs], key_rows(ke[w][:, hs])), 0.0)
            w *= 2
        v_h = key_rows(v[:, hs])
        st = st_ref[h]
        oacc_ref[0:pr, hs] = _mm(att, v_h) + _mm_nt(qd[pr][:, hs], st)
        st_ref[h] = decay[:, hs] * st + _mm(v_h.T, key_rows(ke[pr][:, hs]))
    og = og_ref[...]
    nw = nw_ref[...]
    for h in range(HG_HEADS):
        hs = slice(h * HG_DV, (h + 1) * HG_DV)
        x = oacc_ref[0:rows_in, hs]
        ms = jnp.mean(x * x, axis=-1, keepdims=True)
        y = x * lax.rsqrt(ms + EPS) * nw
        o_ref[:, hs] = (y * jax.nn.silu(og[:, hs])).astype(o_ref.dtype)

    @pl.when(tb == n_tblk - 1)
    def _():
        for h in range(HG_HEADS):
            sfin_ref[0, h] = st_ref[h].T


def _hgrn(proj, lb, norm_w, s0, b, t):
    rows_in = min(t, HG_ROWS)
    n_tblk = t // rows_in
    assert rows_in * n_tblk == t and rows_in % 8 == 0
    has_state = s0 is not None
    n_seq = HG_SEQS if b % HG_SEQS == 0 else 1
    if s0 is None:
        s0 = jnp.zeros((n_seq, HG_HEADS, HG_DK, HG_DV), F32)
    width = HG_HEADS * HG_DK
    proj3 = proj.reshape(b, t, proj.shape[1])
    col = lambda c: pl.BlockSpec((n_seq, rows_in, width), lambda i, j: (i, j, c // width))
    state = (n_seq, HG_HEADS, HG_DK, HG_DV)
    o, s_fin = pl.pallas_call(
        functools.partial(_hgrn_kernel, n_seq=n_seq, rows_in=rows_in, n_tblk=n_tblk, has_state=has_state),
        grid=(b // n_seq, n_tblk),
        in_specs=[col(C_QB), col(C_FB), col(C_IB), col(C_OG),
                  pl.BlockSpec((1, width), lambda i, j: (0, 0)),
                  pl.BlockSpec((1, HG_DV), lambda i, j: (0, 0)),
                  pl.BlockSpec(state, (lambda i, j: (i, 0, 0, 0)) if has_state else (lambda i, j: (0, 0, 0, 0)))],
        out_specs=[pl.BlockSpec((n_seq, rows_in, width), lambda i, j: (i, j, 0)),
                   pl.BlockSpec(state, lambda i, j: (i, 0, 0, 0))],
        out_shape=[jax.ShapeDtypeStruct((b, t, width), BF16),
                   jax.ShapeDtypeStruct((b, HG_HEADS, HG_DK, HG_DV), F32)],
        scratch_shapes=[pltpu.VMEM((n_seq, HG_HEADS, HG_DV, HG_DK), F32), pltpu.VMEM((n_seq, HG_ROWS, width), F32)],
        compiler_params=_cparams("parallel", "arbitrary"),
        name="hgrn2",
    )(proj3, proj3, proj3, proj3, lb.reshape(1, width), norm_w.reshape(1, HG_DV), s0)
    return o.reshape(b * t, width), s_fin


def _decoder_layer(x, past, lb, params):
    (w_in, w_a, w_b, w_o, pe, w1, w2, hg_norm, ln1, ln2, ln3, ln4, w_up, cw, cb, w_dn) = params
    b, t, d = x.shape
    m = b * t
    x2 = x.reshape(m, d)
    tm = min(MM_ROWS, m)
    proj = _in_projection(x2, ln1, *w_in, tm)
    kv_shape = (b, t, 2, NSA_GROUPS, NSA_HD)
    kvc, kvs, kvw = (a.reshape(kv_shape) for a in _kv_relayout(proj, min(CONV_ROWS, m)))
    if past is None:
        ckv = _compress_seq(proj, b, t, pe, w1, w2)
        o_a = _nsa_seq(proj, ckv, b, t)
        new_win = kvw[:, t - min(NSA_WINDOW, t):]
        s0, conv_buf = None, None
    else:
        cache_c, cache_s, page_table_flat, n_pages, page, win_buf, s0, conv_buf = past
        n_win = win_buf.shape[1]
        ckv = _compress_paged(cache_c, page_table_flat, b, n_pages, page, pe, w1, w2)
        o_a = _nsa_step(proj, ckv, cache_s, page_table_flat, win_buf.reshape(-1, NSA_HD), b, t, n_pages, page, n_win)
        new_win = jnp.concatenate([win_buf, kvw], axis=1)[:, t:]
    o_hg, s_fin = _hgrn(proj, lb, hg_norm, s0, b, t)
    mixed = _gated_merge(o_a.astype(BF16), o_hg, w_a, w_b, proj, tm, 512)
    x1 = _matmul_norm_res(mixed, w_o, x2, ln2, tm, 512)
    u = _norm_matmul(x1, ln3, w_up, tm, UP_TN)
    if past is None:
        y = _conv_glu_down(u, cw, cb, w_dn, x1, ln4, t, min(CONV_ROWS, t), 512)
        new_conv = u.reshape(b, t, 2 * D_FF)[:, t - (CONV_W - 1):]
    else:
        act = _conv_glu_step(u.reshape(b, t, 2 * D_FF), conv_buf, cw, cb, min(64, b), 512).reshape(m, D_FF)
        new_conv = jnp.concatenate([conv_buf, u.reshape(b, t, 2 * D_FF)], axis=1)[:, t:] if t < CONV_W - 1 else \
            u.reshape(b, t, 2 * D_FF)[:, t - (CONV_W - 1):]
        y = _matmul_norm_res(act, w_dn, x1, ln4, tm, 512)
    return y.reshape(b, t, d), (kvc, kvs, new_win, s_fin, new_conv)


def _split_w_in(w):
    o_rest = NSA_HEADS * NSA_HD + 3 * KV_COLS + 3 * NSA_HEADS
    edge = jnp.pad(w[:, :o_rest].astype(BF16), ((0, 0), (0, EDGE_COLS - o_rest)))
    return edge, w[:, o_rest:].astype(BF16)


def kernel(x_prompt, x_sample, cache_cmp_kv, cache_sel_kv, state_win_kv, state_hgrn, state_conv, page_table,
           w_in, w_branch_a, w_branch_b, w_out, cmp_pe, cmp_w1, cmp_w2, hg_lb_raw, hg_norm_w, ln_mix_pre,
           ln_mix_post, ln_ffn_pre, ln_ffn_post, w_up, conv_w, conv_b, w_down):
    depth = w_in.shape[0]
    dec_b, n_pages = page_table.shape
    page = cache_cmp_kv.shape[2]
    lb_all = jnp.cumsum(jax.nn.softmax(hg_lb_raw.astype(F32), axis=0), axis=0)
    pt_flat = page_table.reshape(-1).astype(jnp.int32)
    y_p, y_s = x_prompt, x_sample
    new_p, new_s = [], []
    for l in range(depth):
        params = (_split_w_in(w_in[l]), w_branch_a[l].astype(BF16), w_branch_b[l].astype(BF16),
                  w_out[l].astype(BF16), cmp_pe[l].transpose(1, 0, 2)[:, :, None, :],
                  cmp_w1[l].astype(BF16).transpose(1, 0, 2, 3).reshape(2, NSA_BLOCK // 2, 2 * NSA_HD, NSA_HD),
                  cmp_w2[l].astype(BF16), hg_norm_w[l], ln_mix_pre[l], ln_mix_post[l], ln_ffn_pre[l],
                  ln_ffn_post[l], w_up[l].astype(BF16), conv_w[l], conv_b[l], w_down[l].astype(BF16))
        y_p, st_p = _decoder_layer(y_p, None, lb_all[l], params)
        past = (cache_cmp_kv[l].reshape(-1, NSA_HD), cache_sel_kv[l].reshape(-1, NSA_HD),
                pt_flat, n_pages, page, state_win_kv[l], state_hgrn[l], state_conv[l])
        y_s, st_s = _decoder_layer(y_s, past, lb_all[l], params)
        new_p.append(st_p)
        new_s.append(st_s)

    def stack(group, i):
        return jnp.stack([st[i] for st in group], axis=0)

    return (y_p, y_s, stack(new_p, 0), stack(new_s, 0), stack(new_p, 1), stack(new_s, 1), stack(new_p, 2),
            stack(new_s, 2), stack(new_p, 3), stack(new_s, 3), stack(new_p, 4), stack(new_s, 4))
```

```python
import functools

import jax
import jax.numpy as jnp
from jax import lax
from jax.experimental import pallas as pl
from jax.experimental.pallas import tpu as pltpu

F32 = jnp.float32
BF16 = jnp.bfloat16

D_MODEL = 2048
NSA_HEADS = 8
NSA_GROUPS = 2
NSA_HPG = NSA_HEADS // NSA_GROUPS
NSA_HD = 128
NSA_BLOCK = 64
NSA_TOPN = 8
NSA_WINDOW = 512
NSA_SCALE = NSA_HD ** -0.5
FORCED_SCORE = NSA_HPG + 1.0
HG_HEADS = 8
HG_DK = 128
HG_DV = 128
HG_CHUNK = 16
D_FF = 5632
CONV_W = 3
EPS = 1e-6
KV_COLS = 2 * NSA_GROUPS * NSA_HD

C_QA = 0
C_QB = 1024
C_FB = 2048
C_IB = 3072
C_OG = 4096
C_GA = 5120
C_GB = 7168
C_KVC = 9216
C_KVS = 9728
C_KVW = 10240
C_GN = 10752

LANES = 128
MM_ROWS = 1024
UP_TN = 1024
OUT_ROWS = 512
LOG2E = 1.4426950408889634
CONV_ROWS = 512
DOWN_PARTS = 8
NEG_BIG = -1e30
VMEM_LIMIT = 56 * 1024 * 1024


def _cparams(*sem):
    return pltpu.CompilerParams(dimension_semantics=sem, vmem_limit_bytes=VMEM_LIMIT)


def _mm(a, b):
    return jnp.dot(a.astype(BF16), b.astype(BF16), preferred_element_type=F32)


def _mm_nt(a, b):
    return lax.dot_general(a.astype(BF16), b.astype(BF16), (((1,), (1,)), ((), ())),
                           preferred_element_type=F32)


def _norm_mm_kernel(x_ref, ln_ref, w_ref, o_ref, h_ref):
    @pl.when(pl.program_id(1) == 0)
    def _():
        x = x_ref[...]
        ms = jnp.mean(x * x, axis=-1, keepdims=True)
        h_ref[...] = (x * lax.rsqrt(ms + EPS) * ln_ref[...]).astype(BF16)

    o_ref[...] = _mm(h_ref[...], w_ref[...])


def _norm_matmul(x, ln, w, tm, tn):
    m, k = x.shape
    n = w.shape[1]
    return pl.pallas_call(
        _norm_mm_kernel,
        grid=(m // tm, n // tn),
        in_specs=[pl.BlockSpec((tm, k), lambda i, j: (i, 0)),
                  pl.BlockSpec((1, k), lambda i, j: (0, 0)),
                  pl.BlockSpec((k, tn), lambda i, j: (0, j))],
        out_specs=pl.BlockSpec((tm, tn), lambda i, j: (i, j)),
        out_shape=jax.ShapeDtypeStruct((m, n), F32),
        scratch_shapes=[pltpu.VMEM((tm, k), BF16)],
        compiler_params=_cparams("parallel", "arbitrary"),
        name="norm_matmul",
    )(x, ln.reshape(1, k), w)


PROJ_TN = 1024
TILE_REST0 = C_QB // PROJ_TN
TILE_KV0 = C_KVC // PROJ_TN
N_PROJ = C_KVC + 2 * PROJ_TN
EDGE_COLS = N_PROJ - (C_KVC - C_QB)


def _in_proj_kernel(x_ref, ln_ref, we_ref, wr_ref, o_ref, h_ref):
    j = pl.program_id(1)

    @pl.when(j == 0)
    def _():
        x = x_ref[...]
        ms = jnp.mean(x * x, axis=-1, keepdims=True)
        h_ref[...] = (x * lax.rsqrt(ms + EPS) * ln_ref[...]).astype(BF16)

    from_rest = (j >= TILE_REST0) & (j < TILE_KV0)

    @pl.when(from_rest)
    def _():
        o_ref[...] = _mm(h_ref[...], wr_ref[...])

    @pl.when(jnp.logical_not(from_rest))
    def _():
        o_ref[...] = _mm(h_ref[...], we_ref[...])


def _in_projection(x, ln, w_edge, w_rest, tm):
    m, k = x.shape
    tn = PROJ_TN
    n_rest = w_rest.shape[1] // tn
    assert n_rest == TILE_KV0 - TILE_REST0 and w_edge.shape[1] == EDGE_COLS and TILE_REST0 == 1
    edge_tile = lambda j: jnp.clip(j - (TILE_KV0 - TILE_REST0), 0, EDGE_COLS // tn - 1)
    rest_tile = lambda j: jnp.clip(j - TILE_REST0, 0, n_rest - 1)
    return pl.pallas_call(
        _in_proj_kernel,
        grid=(m // tm, N_PROJ // tn),
        in_specs=[pl.BlockSpec((tm, k), lambda i, j: (i, 0)),
                  pl.BlockSpec((1, k), lambda i, j: (0, 0)),
                  pl.BlockSpec((k, tn), lambda i, j: (0, edge_tile(j))),
                  pl.BlockSpec((k, tn), lambda i, j: (0, rest_tile(j)))],
        out_specs=pl.BlockSpec((tm, tn), lambda i, j: (i, j)),
        out_shape=jax.ShapeDtypeStruct((m, N_PROJ), F32),
        scratch_shapes=[pltpu.VMEM((tm, k), BF16)],
        compiler_params=_cparams("parallel", "arbitrary"),
        name="in_projection",
    )(x, ln.reshape(1, k), w_edge, w_rest)


def _kv_relayout_kernel(c_ref, s_ref, w_ref, oc_ref, os_ref, ow_ref, *, tm):
    for x_ref, o_ref in ((c_ref, oc_ref), (s_ref, os_ref), (w_ref, ow_ref)):
        for slot in range(2 * NSA_GROUPS):
            o_ref[pl.ds(slot, tm, stride=2 * NSA_GROUPS), :] = x_ref[:, slot * NSA_HD:(slot + 1) * NSA_HD]


def _kv_relayout(proj, tm):
    m = proj.shape[0]
    slots = 2 * NSA_GROUPS
    out = jax.ShapeDtypeStruct((m * slots, NSA_HD), F32)
    return pl.pallas_call(
        functools.partial(_kv_relayout_kernel, tm=tm),
        grid=(m // tm,),
        in_specs=[pl.BlockSpec((tm, KV_COLS), lambda i, c=c: (i, c // KV_COLS)) for c in (C_KVC, C_KVS, C_KVW)],
        out_specs=[pl.BlockSpec((tm * slots, NSA_HD), lambda i: (i, 0))] * 3,
        out_shape=[out] * 3,
        compiler_params=_cparams("parallel"),
        name="kv_relayout",
    )(proj, proj, proj)


def _mm_norm_res_kernel(a_ref, w_ref, res_ref, ln_ref, o_ref, acc_ref, *, nk):
    k = pl.program_id(1)

    @pl.when(k == 0)
    def _():
        acc_ref[...] = jnp.zeros_like(acc_ref)

    acc_ref[...] += _mm(a_ref[...], w_ref[...])

    @pl.when(k == nk - 1)
    def _():
        y = acc_ref[...]
        ms = jnp.mean(y * y, axis=-1, keepdims=True)
        o_ref[...] = res_ref[...] + y * lax.rsqrt(ms + EPS) * ln_ref[...]


def _matmul_norm_res(a, w, res, ln, tm, tk):
    m, kk = a.shape
    n = w.shape[1]
    nk = kk // tk
    return pl.pallas_call(
        functools.partial(_mm_norm_res_kernel, nk=nk),
        grid=(m // tm, nk),
        in_specs=[pl.BlockSpec((tm, tk), lambda i, k: (i, k)),
                  pl.BlockSpec((tk, n), lambda i, k: (k, 0)),
                  pl.BlockSpec((tm, n), lambda i, k: (i, 0)),
                  pl.BlockSpec((1, n), lambda i, k: (0, 0))],
        out_specs=pl.BlockSpec((tm, n), lambda i, k: (i, 0)),
        out_shape=jax.ShapeDtypeStruct((m, n), F32),
        scratch_shapes=[pltpu.VMEM((tm, n), F32)],
        compiler_params=_cparams("parallel", "arbitrary"),
        name="matmul_norm_res",
    )(a, w, res, ln.reshape(1, n))


def _merge_kernel(oa_ref, ohg_ref, wa_ref, wb_ref, ga_ref, gb_ref, o_ref):
    a = _mm(oa_ref[...], wa_ref[...])
    b = _mm(ohg_ref[...], wb_ref[...])
    o_ref[...] = (jax.nn.sigmoid(ga_ref[...]) * a + jax.nn.sigmoid(gb_ref[...]) * b).astype(BF16)


def _gated_merge(o_a, o_hg, w_a, w_b, proj, tm, tn):
    m, k = o_a.shape
    n = w_a.shape[1]
    return pl.pallas_call(
        _merge_kernel,
        grid=(m // tm, n // tn),
        in_specs=[pl.BlockSpec((tm, k), lambda i, j: (i, 0)),
                  pl.BlockSpec((tm, k), lambda i, j: (i, 0)),
                  pl.BlockSpec((k, tn), lambda i, j: (0, j)),
                  pl.BlockSpec((k, tn), lambda i, j: (0, j)),
                  pl.BlockSpec((tm, tn), lambda i, j: (i, C_GA // tn + j)),
                  pl.BlockSpec((tm, tn), lambda i, j: (i, C_GB // tn + j))],
        out_specs=pl.BlockSpec((tm, tn), lambda i, j: (i, j)),
        out_shape=jax.ShapeDtypeStruct((m, n), BF16),
        compiler_params=_cparams("parallel", "arbitrary"),
        name="gated_merge",
    )(o_a, o_hg, w_a, w_b, proj, proj)


def _conv_taps(fa, fg, cwa_ref, cwg_ref, cba_ref, cbg_ref, tap):
    ca = cba_ref[...]
    cg = cbg_ref[...]
    for j in range(CONV_W):
        ca = ca + tap(fa, j) * cwa_ref[j:j + 1, :]
        cg = cg + tap(fg, j) * cwg_ref[j:j + 1, :]
    return jax.nn.gelu(cg, approximate=True) * ca


def _conv_glu_down_kernel(ua_ref, ug_ref, ha_ref, hg_ref, cwa_ref, cwg_ref, cba_ref, cbg_ref, w_ref, res_ref,
                          ln_ref, o_ref, fa_ref, fg_ref, act_ref, acc_ref, *, tm, tiles_per_seq, nk):
    k = pl.program_id(1)
    slot = k % 2

    @pl.when(k == 0)
    def _():
        acc_ref[...] = jnp.zeros_like(acc_ref)
        act_ref[1] = jnp.zeros(act_ref.shape[1:], act_ref.dtype)

    first = (pl.program_id(0) % tiles_per_seq) == 0
    fa_ref[0:8, :] = jnp.where(first, 0.0, ha_ref[...])
    fg_ref[0:8, :] = jnp.where(first, 0.0, hg_ref[...])
    fa_ref[8:, :] = ua_ref[...]
    fg_ref[8:, :] = ug_ref[...]
    n_parts = DOWN_PARTS
    cols = acc_ref.shape[1] // n_parts
    rows = tm // n_parts
    for part in range(n_parts):
        cs = slice(part * cols, (part + 1) * cols)
        acc_ref[:, cs] += _mm(act_ref[1 - slot], w_ref[:, cs])
        r0 = part * rows
        xa, xg = fa_ref[r0:r0 + rows + 8, :], fg_ref[r0:r0 + rows + 8, :]
        tap = lambda x, j: (x if j == CONV_W - 1 else pltpu.roll(x, CONV_W - 1 - j, axis=0))[8:]
        act = _conv_taps(xa, xg, cwa_ref, cwg_ref, cba_ref, cbg_ref, tap)
        act_ref[slot, r0:r0 + rows, :] = act.astype(act_ref.dtype)

    @pl.when(k == nk)
    def _():
        y = acc_ref[...]
        ms = jnp.mean(y * y, axis=-1, keepdims=True)
        o_ref[...] = res_ref[...] + y * lax.rsqrt(ms + EPS) * ln_ref[...]


def _conv_glu_down(u, conv_w, conv_b, w, res, ln, seq_len, tm, tk):
    m = u.shape[0]
    n = w.shape[1]
    nk = D_FF // tk
    hb = tm // 8
    kc = lambda k: jnp.minimum(k, nk - 1)
    kw = lambda k: jnp.maximum(k - 1, 0)
    halo = lambda i, k, off: (jnp.maximum(i * hb - 1, 0), kc(k) + off)
    return pl.pallas_call(
        functools.partial(_conv_glu_down_kernel, tm=tm, tiles_per_seq=seq_len // tm, nk=nk),
        grid=(m // tm, nk + 1),
        in_specs=[pl.BlockSpec((tm, tk), lambda i, k: (i, kc(k))),
                  pl.BlockSpec((tm, tk), lambda i, k: (i, kc(k) + nk)),
                  pl.BlockSpec((8, tk), lambda i, k: halo(i, k, 0)),
                  pl.BlockSpec((8, tk), lambda i, k: halo(i, k, nk)),
                  pl.BlockSpec((CONV_W, tk), lambda i, k: (0, kc(k))),
                  pl.BlockSpec((CONV_W, tk), lambda i, k: (0, kc(k) + nk)),
                  pl.BlockSpec((1, tk), lambda i, k: (0, kc(k))),
                  pl.BlockSpec((1, tk), lambda i, k: (0, kc(k) + nk)),
                  pl.BlockSpec((tk, n), lambda i, k: (kw(k), 0)),
                  pl.BlockSpec((tm, n), lambda i, k: (i, 0)),
                  pl.BlockSpec((1, n), lambda i, k: (0, 0))],
        out_specs=pl.BlockSpec((tm, n), lambda i, k: (i, 0)),
        out_shape=jax.ShapeDtypeStruct((m, n), F32),
        scratch_shapes=[pltpu.VMEM((tm + 8, tk), F32), pltpu.VMEM((tm + 8, tk), F32),
                        pltpu.VMEM((2, tm, tk), BF16), pltpu.VMEM((tm, n), F32)],
        compiler_params=_cparams("parallel", "arbitrary"),
        name="conv_glu_down",
    )(u, u, u, u, conv_w, conv_w, conv_b.reshape(1, -1), conv_b.reshape(1, -1), w, res, ln.reshape(1, n))


def _conv_glu_step_kernel(ua_ref, ug_ref, ba_ref, bg_ref, cwa_ref, cwg_ref, cba_ref, cbg_ref,
                          o_ref, fa_ref, fg_ref, *, t):
    fa_ref[:, 8 - (CONV_W - 1):8, :] = ba_ref[...]
    fg_ref[:, 8 - (CONV_W - 1):8, :] = bg_ref[...]
    fa_ref[:, 8:, :] = ua_ref[...]
    fg_ref[:, 8:, :] = ug_ref[...]
    tap = lambda f, j: f[:, 8 - (CONV_W - 1) + j:8 - (CONV_W - 1) + j + t, :]
    o_ref[...] = _conv_taps(fa_ref, fg_ref, cwa_ref, cwg_ref, cba_ref, cbg_ref, tap)


def _conv_glu_step(u3, buf, conv_w, conv_b, nb, tn):
    b, t, _ = u3.shape
    nj = D_FF // tn
    return pl.pallas_call(
        functools.partial(_conv_glu_step_kernel, t=t),
        grid=(b // nb, nj),
        in_specs=[pl.BlockSpec((nb, t, tn), lambda i, j: (i, 0, j)),
                  pl.BlockSpec((nb, t, tn), lambda i, j: (i, 0, j + nj)),
                  pl.BlockSpec((nb, CONV_W - 1, tn), lambda i, j: (i, 0, j)),
                  pl.BlockSpec((nb, CONV_W - 1, tn), lambda i, j: (i, 0, j + nj)),
                  pl.BlockSpec((CONV_W, tn), lambda i, j: (0, j)),
                  pl.BlockSpec((CONV_W, tn), lambda i, j: (0, j + nj)),
                  pl.BlockSpec((1, tn), lambda i, j: (0, j)),
                  pl.BlockSpec((1, tn), lambda i, j: (0, j + nj))],
        out_specs=pl.BlockSpec((nb, t, tn), lambda i, j: (i, 0, j)),
        out_shape=jax.ShapeDtypeStruct((b, t, D_FF), F32),
        scratch_shapes=[pltpu.VMEM((nb, 8 + t, tn), F32), pltpu.VMEM((nb, 8 + t, tn), F32)],
        compiler_params=_cparams("parallel", "arbitrary"),
        name="conv_glu_step",
    )(u3, u3, buf, buf, conv_w, conv_w, conv_b.reshape(1, -1), conv_b.reshape(1, -1))


def _compress_body(load_x, pe_ref, w1_ref, w2_ref, o_ref, n_seq, n_cmp):
    for c in range(2):
        acc = jnp.zeros((NSA_GROUPS * n_seq * n_cmp, NSA_HD), F32)
        for l in range(0, NSA_BLOCK, CMP_STACK):
            parts = [jnp.concatenate([load_x(l + dl, c, g) for g in range(NSA_GROUPS)], axis=0) + pe_ref[c, l + dl]
                     for dl in range(CMP_STACK)]
            acc = acc + _mm(jnp.concatenate(parts, axis=1), w1_ref[c, l // CMP_STACK])
        out = _mm(jax.nn.silu(acc), w2_ref[c])
        for g in range(NSA_GROUPS):
            for s in range(n_seq):
                r0 = (g * n_seq + s) * n_cmp
                o_ref[s, c, g] = out[r0:r0 + n_cmp]


def _compress_seq_kernel(x0_ref, x1_ref, x2_ref, x3_ref, pe_ref, w1_ref, w2_ref, o_ref, *, n_cmp):
    xs = (x0_ref, x1_ref, x2_ref, x3_ref)
    load_x = lambda l, c, g: xs[c * NSA_GROUPS + g][pl.ds(l, n_cmp, stride=NSA_BLOCK), :]
    _compress_body(load_x, pe_ref, w1_ref, w2_ref, o_ref, 1, n_cmp)


def _compress_seq(proj, b, t, pe, w1, w2):
    n_cmp = t // NSA_BLOCK
    rows = n_cmp * NSA_BLOCK
    assert rows == t
    return pl.pallas_call(
        functools.partial(_compress_seq_kernel, n_cmp=n_cmp),
        grid=(b,),
        in_specs=[pl.BlockSpec((rows, NSA_HD), lambda i, cg=cg: (i, C_KVC // NSA_HD + cg))
                  for cg in range(2 * NSA_GROUPS)] + [
                  pl.BlockSpec(pe.shape, lambda i: (0, 0, 0, 0)),
                  pl.BlockSpec(w1.shape, lambda i: (0, 0, 0, 0)),
                  pl.BlockSpec(w2.shape, lambda i: (0, 0, 0))],
        out_specs=pl.BlockSpec((1, 2, NSA_GROUPS, n_cmp, NSA_HD), lambda i: (i, 0, 0, 0, 0)),
        out_shape=jax.ShapeDtypeStruct((b, 2, NSA_GROUPS, n_cmp, NSA_HD), F32),
        compiler_params=_cparams("parallel"),
        name="compress_seq",
    )(proj, proj, proj, proj, pe, w1, w2)


KV_SLOTS = 2 * NSA_GROUPS
CMP_SEQS = 2
CMP_STACK = 2
CMP_PITCH = NSA_BLOCK * KV_SLOTS + 8


def _compress_paged_kernel(pt_ref, *refs, n_seq, n_pages, page_rows):
    del pt_ref
    pages = refs[:n_seq * n_pages]
    pe_ref, w1_ref, w2_ref, o_ref, x_ref = refs[n_seq * n_pages:]
    blk_rows = NSA_BLOCK * KV_SLOTS
    per_page = page_rows // blk_rows
    for i, pg in enumerate(pages):
        for j in range(per_page):
            n = i * per_page + j
            x_ref[n * CMP_PITCH:n * CMP_PITCH + blk_rows, :] = pg[j * blk_rows:(j + 1) * blk_rows, :]
    n_cmp = n_pages * per_page
    load_x = lambda l, c, g: x_ref[pl.ds(l * KV_SLOTS + c * NSA_GROUPS + g, n_seq * n_cmp, stride=CMP_PITCH), :]
    _compress_body(load_x, pe_ref, w1_ref, w2_ref, o_ref, n_seq, n_cmp)


def _compress_paged(cache2d, page_table_flat, b, n_pages, page, pe, w1, w2):
    page_rows = page * KV_SLOTS
    n_cmp = n_pages * page // NSA_BLOCK
    n_seq = CMP_SEQS if b % CMP_SEQS == 0 else 1
    page_spec = lambda s, p: pl.BlockSpec(
        (page_rows, NSA_HD), lambda i, pt: (pt[(i * n_seq + s) * n_pages + p], 0))
    grid_spec = pltpu.PrefetchScalarGridSpec(
        num_scalar_prefetch=1,
        grid=(b // n_seq,),
        in_specs=[page_spec(s, p) for s in range(n_seq) for p in range(n_pages)] + [
            pl.BlockSpec(pe.shape, lambda i, pt: (0, 0, 0, 0)),
            pl.BlockSpec(w1.shape, lambda i, pt: (0, 0, 0, 0)),
            pl.BlockSpec(w2.shape, lambda i, pt: (0, 0, 0))],
        out_specs=pl.BlockSpec((n_seq, 2, NSA_GROUPS, n_cmp, NSA_HD), lambda i, pt: (i, 0, 0, 0, 0)),
        scratch_shapes=[pltpu.VMEM((n_seq * n_cmp * CMP_PITCH, NSA_HD), F32)],
    )
    return pl.pallas_call(
        functools.partial(_compress_paged_kernel, n_seq=n_seq, n_pages=n_pages, page_rows=page_rows),
        grid_spec=grid_spec,
        out_shape=jax.ShapeDtypeStruct((b, 2, NSA_GROUPS, n_cmp, NSA_HD), F32),
        compiler_params=_cparams("parallel"),
        name="compress_paged",
    )(page_table_flat, *([cache2d] * (n_seq * n_pages)), pe, w1, w2)


def _alibi_slope(g, h):
    return 2.0 ** (-8.0 * (g * NSA_HPG + h + 1.0) / NSA_HEADS)


def _row_consts(tq, g, q_pos0):
    rows = NSA_HPG * tq
    r = lax.broadcasted_iota(jnp.int32, (rows, 1), 0)
    head = r // tq
    qpos = q_pos0 + (r - head * tq)
    slope = jnp.zeros((rows, 1), F32)
    for h in range(NSA_HPG):
        slope = jnp.where(head == h, _alibi_slope(g, h), slope)
    return qpos, slope


def _stack_heads(q_ref, g):
    return jnp.concatenate(
        [q_ref[:, (g * NSA_HPG + h) * NSA_HD:(g * NSA_HPG + h + 1) * NSA_HD] for h in range(NSA_HPG)], axis=0)


def _compressed_branch(qs, ck, cv, qpos, slope, n_cmp):
    pad = jnp.zeros((LANES - n_cmp, NSA_HD), F32)
    ckp = jnp.concatenate([ck, pad], axis=0)
    cvp = jnp.concatenate([cv, pad], axis=0)
    n = lax.broadcasted_iota(jnp.int32, (1, LANES), 1)
    dist = qpos - ((n + 1) * NSA_BLOCK - 1)
    s = _mm_nt(qs, ckp) * NSA_SCALE - slope * dist.astype(F32)
    mask = (dist >= 0) & (n < n_cmp)
    s = jnp.where(mask, s, NEG_BIG)
    m = jnp.max(s, axis=-1, keepdims=True)
    e = jnp.where(mask, jnp.exp(s - m), 0.0)
    d = jnp.sum(e, axis=-1, keepdims=True)
    p = e / jnp.where(d > 0, d, 1.0)
    return _mm(p, cvp), p


def _select_blocks(imp, q_pos0, n_blk):
    tq = imp.shape[0]
    nb = -(-n_blk // 8) * 8
    imp_t = (imp if tq == LANES else _pad_rows(imp, LANES)).T[0:nb]
    blk = lax.broadcasted_iota(jnp.int32, (nb, LANES), 0)
    cur = (q_pos0 + lax.broadcasted_iota(jnp.int32, (nb, LANES), 1)) // NSA_BLOCK
    forced = (blk == 0) | (blk == cur) | (blk == cur - 1)
    valid = blk <= cur
    score = jnp.where(valid, jnp.where(forced, FORCED_SCORE, imp_t), -1.0)
    score = jnp.where(blk < n_blk, score, -2.0)
    blk_f = blk.astype(F32)
    sel = jnp.zeros((nb, LANES), F32)
    for _ in range(min(NSA_TOPN, n_blk)):
        mx = jnp.max(score, axis=0, keepdims=True)
        first = jnp.min(jnp.where(score == mx, blk_f, 1e9), axis=0, keepdims=True)
        hit = blk_f == first
        sel = jnp.where(hit, 1.0, sel)
        score = jnp.where(hit, -3.0, score)
    return _pad_rows(sel, LANES).T[0:tq]


def _expand_sel(sel, key0, nkeys):
    bi = lax.broadcasted_iota(jnp.int32, (LANES, nkeys), 0)
    ki = lax.broadcasted_iota(jnp.int32, (LANES, nkeys), 1)
    expand = (bi == (key0 + ki) // NSA_BLOCK).astype(BF16)
    return _mm(sel, expand)


def _write_gated(o_ref, gate_ref, g, tq, o_cmp, o_sel, o_win):
    sig = jax.nn.sigmoid(gate_ref[...])
    for h in range(NSA_HPG):
        c0 = (g * NSA_HPG + h) * 3
        rs = slice(h * tq, (h + 1) * tq)
        o = sig[:, c0:c0 + 1] * o_cmp[rs] + sig[:, c0 + 1:c0 + 2] * o_sel[rs] + sig[:, c0 + 2:c0 + 3] * o_win[rs]
        o_ref[:, (g * NSA_HPG + h) * NSA_HD:(g * NSA_HPG + h + 1) * NSA_HD] = o.astype(o_ref.dtype)


SEQ_TK = 256


def _lane_groups(x):
    return [x[:, i:i + LANES] for i in range(0, x.shape[1], LANES)]


def _tiled_attention(heads, kv_ref, lo, hi, tile_masks, s_ref, m_ref, l_ref, acc_ref, with_tile0=False):
    tq = m_ref.shape[0] // len(heads)
    head_rows = [slice(i * tq, (i + 1) * tq) for i in range(len(heads))]
    groups = sorted({g for _, _, g in heads})
    kcol = lambda g: slice(g * NSA_HD, (g + 1) * NSA_HD)
    vcol = lambda g: slice((NSA_GROUPS + g) * NSA_HD, (NSA_GROUPS + g + 1) * NSA_HD)
    m_ref[...] = jnp.full(m_ref.shape, NEG_BIG, F32)

    def scores(kt, carry):
        key0 = kt * SEQ_TK
        masks = tile_masks(key0)
        rows = pl.ds(pl.multiple_of(key0, SEQ_TK), SEQ_TK)
        k = {g: kv_ref[rows, kcol(g)] for g in groups}
        kpos = (key0 + lax.broadcasted_iota(jnp.int32, (1, SEQ_TK), 1)).astype(F32)
        for (load_q, slope, g), rs in zip(heads, head_rows):
            valid = masks[g]
            s = jnp.where(valid, _mm_nt(load_q(), k[g]) * (NSA_SCALE * LOG2E) + (slope * LOG2E) * kpos, NEG_BIG)
            s_ref[kt, rs] = s
            m_ref[rs] = functools.reduce(jnp.maximum, [m_ref[rs]] + _lane_groups(s))
        return carry

    if with_tile0:
        scores(jnp.int32(0), 0)
    lax.fori_loop(lo, hi, scores, 0)
    for rs in head_rows:
        m = jnp.maximum(jnp.max(m_ref[rs], axis=-1, keepdims=True), 0.1 * NEG_BIG)
        m_ref[rs] = jnp.broadcast_to(m, (tq, LANES))
    l_ref[...] = jnp.zeros(l_ref.shape, F32)
    acc_ref[...] = jnp.zeros(acc_ref.shape, F32)

    def values(kt, carry):
        rows = pl.ds(pl.multiple_of(kt * SEQ_TK, SEQ_TK), SEQ_TK)
        v = {g: kv_ref[rows, vcol(g)] for g in groups}
        for (_, _, g), rs in zip(heads, head_rows):
            m = m_ref[rs]
            p = [jnp.exp2(s - m) for s in _lane_groups(s_ref[kt, rs])]
            l_ref[rs] += sum(p[1:], p[0])
            acc_ref[rs] += _mm(jnp.concatenate(p, axis=1), v[g])
        return carry

    if with_tile0:
        values(jnp.int32(0), 0)
    lax.fori_loop(lo, hi, values, 0)
    outs = []
    for rs in head_rows:
        l = jnp.sum(l_ref[rs], axis=-1, keepdims=True)
        outs.append(acc_ref[rs] / jnp.where(l > 0, l, 1.0))
    return jnp.concatenate(outs, axis=0)


def _nsa_seq_kernel(q_ref, gate_ref, ckv_ref, ks_ref, kw_ref, o_ref, s_ref, m_ref, l_ref, acc_ref,
                    *, tq, n_cmp, n_blk):
    j = pl.program_id(1)
    q0 = j * tq
    lane = lax.broadcasted_iota(jnp.int32, (1, SEQ_TK), 1)
    qpos_t = q0 + lax.broadcasted_iota(jnp.int32, (tq, 1), 0)
    hi = (q0 + tq - 1) // SEQ_TK + 1
    o_cmp, sel = [], []
    for g in range(NSA_GROUPS):
        qs = _stack_heads(q_ref, g)
        qpos, slope = _row_consts(tq, g, q0)
        o_g, p_c = _compressed_branch(qs, ckv_ref[0, 0, g], ckv_ref[0, 1, g], qpos, slope, n_cmp)
        imp = p_c[0:tq]
        for h in range(1, NSA_HPG):
            imp = imp + p_c[h * tq:(h + 1) * tq]
        o_cmp.append(o_g)
        sel.append(_select_blocks(imp, q0, n_blk))
    order = [(g, h) for h in range(NSA_HPG) for g in range(NSA_GROUPS)]
    heads = [(lambda c=(g * NSA_HPG + h) * NSA_HD: q_ref[:, c:c + NSA_HD], _alibi_slope(g, h), g) for g, h in order]
    scratch = (s_ref, m_ref, l_ref, acc_ref)

    def sel_masks(key0):
        dist = qpos_t - (key0 + lane)
        return [(dist >= 0) & (_expand_sel(sel[g], key0, SEQ_TK) > 0.5) for g in range(NSA_GROUPS)]

    picked = functools.reduce(jnp.maximum, sel)
    blk = lax.broadcasted_iota(jnp.int32, picked.shape, 1)
    first_blk = jnp.min(jnp.where((picked > 0.5) & (blk >= 1), blk.astype(F32), float(LANES)))
    first_tile = jnp.clip(first_blk.astype(jnp.int32) // (SEQ_TK // NSA_BLOCK), 1, hi)
    o_sel = _tiled_attention(heads, ks_ref, first_tile, hi, sel_masks, *scratch, with_tile0=True)

    def win_masks(key0):
        dist = qpos_t - (key0 + lane)
        return [(dist >= 0) & (dist < NSA_WINDOW)] * NSA_GROUPS

    lo = jnp.maximum(q0 - (NSA_WINDOW - 1), 0) // SEQ_TK
    o_win = _tiled_attention(heads, kw_ref, lo, hi, win_masks, *scratch)
    sig = jax.nn.sigmoid(gate_ref[...])
    for i, (g, h) in enumerate(order):
        c0 = (g * NSA_HPG + h) * 3
        rs = slice(i * tq, (i + 1) * tq)
        o = (sig[:, c0:c0 + 1] * o_cmp[g][h * tq:(h + 1) * tq] + sig[:, c0 + 1:c0 + 2] * o_sel[rs]
             + sig[:, c0 + 2:c0 + 3] * o_win[rs])
        o_ref[:, (g * NSA_HPG + h) * NSA_HD:(g * NSA_HPG + h + 1) * NSA_HD] = o.astype(o_ref.dtype)


def _nsa_seq(proj, ckv, b, t):
    tq = 128
    n_cmp = t // NSA_BLOCK
    n_blk = -(-t // NSA_BLOCK)
    nq = t // tq
    assert t % SEQ_TK == 0 and n_blk <= LANES
    return pl.pallas_call(
        functools.partial(_nsa_seq_kernel, tq=tq, n_cmp=n_cmp, n_blk=n_blk),
        scratch_shapes=[pltpu.VMEM((t // SEQ_TK, NSA_HEADS * tq, SEQ_TK), F32)]
        + [pltpu.VMEM((NSA_HEADS * tq, LANES), F32)] * 3,
        grid=(b, nq),
        in_specs=[pl.BlockSpec((tq, NSA_HEADS * NSA_HD), lambda i, j: (i * nq + j, C_QA // 1024)),
                  pl.BlockSpec((tq, LANES), lambda i, j: (i * nq + j, C_GN // LANES)),
                  pl.BlockSpec((1, 2, NSA_GROUPS, n_cmp, NSA_HD), lambda i, j: (i, 0, 0, 0, 0)),
                  pl.BlockSpec((t, KV_COLS), lambda i, j: (i, C_KVS // KV_COLS)),
                  pl.BlockSpec((t, KV_COLS), lambda i, j: (i, C_KVW // KV_COLS))],
        out_specs=pl.BlockSpec((tq, NSA_HEADS * NSA_HD), lambda i, j: (i * nq + j, 0)),
        out_shape=jax.ShapeDtypeStruct((b * t, NSA_HEADS * NSA_HD), BF16),
        compiler_params=_cparams("parallel", "arbitrary"),
        name="nsa_seq",
    )(proj, proj, ckv, proj, proj)


def _pad_rows(x, rows):
    if x.shape[0] == rows:
        return x
    return jnp.concatenate([x, jnp.zeros((rows - x.shape[0], x.shape[1]), x.dtype)], axis=0)


def _two_pass_attention(qs, tiles):
    scores = []
    for k, _, valid, bias in tiles:
        scores.append(jnp.where(valid, _mm_nt(qs, k) * NSA_SCALE - bias, NEG_BIG))
    m = jnp.max(functools.reduce(jnp.maximum, scores), axis=-1, keepdims=True)
    acc = jnp.zeros((qs.shape[0], NSA_HD), F32)
    lsum = jnp.zeros(scores[0].shape, F32)
    for s, (_, v, _, _) in zip(scores, tiles):
        p = jnp.where(s > 0.5 * NEG_BIG, jnp.exp(s - m), 0.0)
        lsum = lsum + p
        acc = acc + _mm(p, v)
    l = jnp.sum(lsum, axis=-1, keepdims=True)
    return acc / jnp.where(l > 0, l, 1.0)


STEP_SEQS = 2


def _nsa_step_kernel(pt_ref, *refs, n_seq, t, n_pages, n_win, **kw):
    del pt_ref
    q_ref, gate_ref, ckv_ref, ksn_ref, kwn_ref, win_ref, o_ref = refs[n_seq * n_pages:]
    win_rows = n_win * KV_SLOTS
    for s in range(n_seq):
        rows = pl.ds(s * t, t)
        _nsa_step_one(refs[s * n_pages:(s + 1) * n_pages], q_ref.at[rows], gate_ref.at[rows], ckv_ref.at[pl.ds(s, 1)],
                      ksn_ref.at[rows], kwn_ref.at[rows], win_ref.at[pl.ds(s * win_rows, win_rows)], o_ref.at[rows],
                      t=t, n_pages=n_pages, n_win=n_win, **kw)


def _nsa_step_one(pages, q_ref, gate_ref, ckv_ref, ksn_ref, kwn_ref, win_ref, o_ref,
                  *, t, n_pages, page, past_len, n_win, n_cmp, n_blk):
    lane = lax.broadcasted_iota(jnp.int32, (1, page), 1)
    blocks_per_page = page // NSA_BLOCK
    for g in range(NSA_GROUPS):
        qs = _stack_heads(q_ref, g)
        qpos, slope = _row_consts(t, g, past_len)
        o_cmp, p_c = _compressed_branch(qs, ckv_ref[0, 0, g], ckv_ref[0, 1, g], qpos, slope, n_cmp)
        imp = p_c[0:t]
        for h in range(1, NSA_HPG):
            imp = imp + p_c[h * t:(h + 1) * t]
        sel = _select_blocks(imp, past_len, n_blk)
        kcol = slice(g * NSA_HD, (g + 1) * NSA_HD)
        vcol = slice((NSA_GROUPS + g) * NSA_HD, (NSA_GROUPS + g + 1) * NSA_HD)
        kslot, vslot = g, NSA_GROUPS + g

        tiles = []
        for p in range(n_pages + 1):
            key0 = p * page
            if p < n_pages:
                k = pages[p][pl.ds(kslot, page, stride=KV_SLOTS), :]
                v = pages[p][pl.ds(vslot, page, stride=KV_SLOTS), :]
            else:
                k, v = _pad_rows(ksn_ref[:, kcol], page), _pad_rows(ksn_ref[:, vcol], page)
            chosen = jnp.zeros((t, page), F32)
            for bi in range(blocks_per_page):
                blk = p * blocks_per_page + bi
                chosen = jnp.where(lane // NSA_BLOCK == bi, sel[:, blk:blk + 1], chosen)
            chosen = jnp.concatenate([chosen] * NSA_HPG, axis=0)
            dist = qpos - (key0 + lane)
            tiles.append((k, v, (dist >= 0) & (chosen > 0.5), slope * dist.astype(F32)))
        o_sel = _two_pass_attention(qs, tiles)

        tiles = []
        for p in range(n_win // page + 1):
            key0 = past_len - n_win + p * page
            if p < n_win // page:
                k = win_ref[pl.ds(p * page * KV_SLOTS + kslot, page, stride=KV_SLOTS), :]
                v = win_ref[pl.ds(p * page * KV_SLOTS + vslot, page, stride=KV_SLOTS), :]
            else:
                k, v = _pad_rows(kwn_ref[:, kcol], page), _pad_rows(kwn_ref[:, vcol], page)
            dist = qpos - (key0 + lane)
            valid = (dist >= 0) & (dist < NSA_WINDOW) & (key0 + lane >= 0)
            tiles.append((k, v, valid, slope * dist.astype(F32)))
        o_win = _two_pass_attention(qs, tiles)
        _write_gated(o_ref, gate_ref, g, t, o_cmp, o_sel, o_win)


def _nsa_step(proj, ckv, cache2d, page_table_flat, win2d, b, t, n_pages, page, n_win):
    past_len = n_pages * page
    assert n_win % page == 0 and t <= page
    n_cmp = (past_len + t) // NSA_BLOCK
    n_blk = -(-(past_len + t) // NSA_BLOCK)
    assert n_cmp * NSA_BLOCK == past_len and n_blk <= LANES
    n_seq = STEP_SEQS if b % STEP_SEQS == 0 else 1
    page_spec = lambda s, p: pl.BlockSpec(
        (page * KV_SLOTS, NSA_HD), lambda i, pt: (pt[(i * n_seq + s) * n_pages + p], 0))
    grid_spec = pltpu.PrefetchScalarGridSpec(
        num_scalar_prefetch=1,
        grid=(b // n_seq,),
        in_specs=[page_spec(s, p) for s in range(n_seq) for p in range(n_pages)] + [
            pl.BlockSpec((n_seq * t, NSA_HEADS * NSA_HD), lambda i, pt: (i, C_QA // 1024)),
            pl.BlockSpec((n_seq * t, LANES), lambda i, pt: (i, C_GN // LANES)),
            pl.BlockSpec((n_seq, 2, NSA_GROUPS, n_cmp, NSA_HD), lambda i, pt: (i, 0, 0, 0, 0)),
            pl.BlockSpec((n_seq * t, KV_COLS), lambda i, pt: (i, C_KVS // KV_COLS)),
            pl.BlockSpec((n_seq * t, KV_COLS), lambda i, pt: (i, C_KVW // KV_COLS)),
            pl.BlockSpec((n_seq * n_win * KV_SLOTS, NSA_HD), lambda i, pt: (i, 0))],
        out_specs=pl.BlockSpec((n_seq * t, NSA_HEADS * NSA_HD), lambda i, pt: (i, 0)),
    )
    return pl.pallas_call(
        functools.partial(_nsa_step_kernel, n_seq=n_seq, t=t, n_pages=n_pages, page=page, past_len=past_len,
                          n_win=n_win, n_cmp=n_cmp, n_blk=n_blk),
        grid_spec=grid_spec,
        out_shape=jax.ShapeDtypeStruct((b * t, NSA_HEADS * NSA_HD), F32),
        compiler_params=_cparams("parallel"),
        name="nsa_step",
    )(page_table_flat, *([cache2d] * (n_seq * n_pages)), proj, proj, ckv, proj, proj, win2d)


HG_ROWS = 128


HG_SEQS = 2


def _hgrn_kernel(q_ref, f_ref, i_ref, og_ref, lb_ref, nw_ref, s0_ref, o_ref, sfin_ref, st_ref, oacc_ref,
                 *, n_seq, **kw):
    for s in range(n_seq):
        one = pl.ds(s, 1)
        _hgrn_one(q_ref.at[s], f_ref.at[s], i_ref.at[s], og_ref.at[s], lb_ref, nw_ref, s0_ref.at[one],
                  o_ref.at[s], sfin_ref.at[one], st_ref.at[s], oacc_ref.at[s], **kw)


def _hgrn_one(q_ref, f_ref, i_ref, og_ref, lb_ref, nw_ref, s0_ref, o_ref, sfin_ref,
              st_ref, oacc_ref, *, rows_in, n_tblk, has_state):
    tb = pl.program_id(1)

    @pl.when(tb == 0)
    def _():
        for h in range(HG_HEADS):
            if has_state:
                st_ref[h] = s0_ref[0, h].T
            else:
                st_ref[h] = jnp.zeros((HG_DV, HG_DK), F32)

    pr = HG_CHUNK if rows_in <= HG_CHUNK else HG_ROWS
    assert rows_in <= pr

    def padded(ref):
        x = ref[...]
        return x if rows_in == pr else _pad_rows(x, pr)

    def key_rows(x):
        return x if pr == HG_ROWS else _pad_rows(x, HG_ROWS)

    q, f, v = padded(q_ref), padded(f_ref), padded(i_ref)
    lb = lb_ref[...]
    row = lax.broadcasted_iota(jnp.int32, (pr, 1), 0)
    live = row < rows_in
    forget = lb + (1.0 - lb) * jax.nn.sigmoid(f)
    k = jnp.where(live, (1.0 - lb) * jax.nn.sigmoid(-f), 0.0)
    gl = jnp.where(live, jnp.log(forget), 0.0)
    rc = row % HG_CHUNK
    cum, suf = gl, gl
    s = 1
    while s < HG_CHUNK:
        cum = cum + jnp.where(rc >= s, pltpu.roll(cum, s, axis=0), 0.0)
        suf = suf + jnp.where(rc < HG_CHUNK - s, pltpu.roll(suf, pr - s, axis=0), 0.0)
        s *= 2
    ki = k * jnp.exp(-cum)
    qd = {HG_CHUNK: q * jnp.exp(cum)}
    ke = {HG_CHUNK: k * jnp.exp(suf - gl)}
    ci = lax.broadcasted_iota(jnp.int32, (pr, HG_ROWS), 0)
    cj = lax.broadcasted_iota(jnp.int32, (pr, HG_ROWS), 1)
    masks = {HG_CHUNK: (ci // HG_CHUNK == cj // HG_CHUNK) & (ci >= cj)}
    w = HG_CHUNK
    while w < pr:
        tot = cum + suf - gl
        odd = (row // w) % 2 == 1
        cum = cum + jnp.where(odd, pltpu.roll(tot, w, axis=0), 0.0)
        suf = suf + jnp.where(odd, 0.0, pltpu.roll(tot, pr - w, axis=0))
        masks[2 * w] = ((ci // w) % 2 == 1) & (cj // w == ci // w - 1)
        w *= 2
        qd[w] = q * jnp.exp(cum)
        ke[w] = k * jnp.exp(suf - gl)
    decay = jnp.exp((cum + suf - gl)[0:1, :])
    for h in range(HG_HEADS):
        hs = slice(h * HG_DK, (h + 1) * HG_DK)
        att = jnp.where(masks[HG_CHUNK], _mm_nt(qd[HG_CHUNK][:, hs], key_rows(ki[:, hs])), 0.0)
        w = HG_CHUNK
        while w < pr:
            att = att + jnp.where(masks[2 * w], _mm_nt(qd[w][:, hs], key_rows(ke[w][:, hs])), 0.0)
            w *= 2
        v_h = key_rows(v[:, hs])
        st = st_ref[h]
        oacc_ref[0:pr, hs] = _mm(att, v_h) + _mm_nt(qd[pr][:, hs], st)
        st_ref[h] = decay[:, hs] * st + _mm(v_h.T, key_rows(ke[pr][:, hs]))
    og = og_ref[...]
    nw = nw_ref[...]
    for h in range(HG_HEADS):
        hs = slice(h * HG_DV, (h + 1) * HG_DV)
        x = oacc_ref[0:rows_in, hs]
        ms = jnp.mean(x * x, axis=-1, keepdims=True)
        y = x * lax.rsqrt(ms + EPS) * nw
        o_ref[:, hs] = (y * jax.nn.silu(og[:, hs])).astype(o_ref.dtype)

    @pl.when(tb == n_tblk - 1)
    def _():
        for h in range(HG_HEADS):
            sfin_ref[0, h] = st_ref[h].T


def _hgrn(proj, lb, norm_w, s0, b, t):
    rows_in = min(t, HG_ROWS)
    n_tblk = t // rows_in
    assert rows_in * n_tblk == t and rows_in % 8 == 0
    has_state = s0 is not None
    n_seq = HG_SEQS if b % HG_SEQS == 0 else 1
    if s0 is None:
        s0 = jnp.zeros((n_seq, HG_HEADS, HG_DK, HG_DV), F32)
    width = HG_HEADS * HG_DK
    proj3 = proj.reshape(b, t, proj.shape[1])
    col = lambda c: pl.BlockSpec((n_seq, rows_in, width), lambda i, j: (i, j, c // width))
    state = (n_seq, HG_HEADS, HG_DK, HG_DV)
    o, s_fin = pl.pallas_call(
        functools.partial(_hgrn_kernel, n_seq=n_seq, rows_in=rows_in, n_tblk=n_tblk, has_state=has_state),
        grid=(b // n_seq, n_tblk),
        in_specs=[col(C_QB), col(C_FB), col(C_IB), col(C_OG),
                  pl.BlockSpec((1, width), lambda i, j: (0, 0)),
                  pl.BlockSpec((1, HG_DV), lambda i, j: (0, 0)),
                  pl.BlockSpec(state, (lambda i, j: (i, 0, 0, 0)) if has_state else (lambda i, j: (0, 0, 0, 0)))],
        out_specs=[pl.BlockSpec((n_seq, rows_in, width), lambda i, j: (i, j, 0)),
                   pl.BlockSpec(state, lambda i, j: (i, 0, 0, 0))],
        out_shape=[jax.ShapeDtypeStruct((b, t, width), BF16),
                   jax.ShapeDtypeStruct((b, HG_HEADS, HG_DK, HG_DV), F32)],
        scratch_shapes=[pltpu.VMEM((n_seq, HG_HEADS, HG_DV, HG_DK), F32), pltpu.VMEM((n_seq, HG_ROWS, width), F32)],
        compiler_params=_cparams("parallel", "arbitrary"),
        name="hgrn2",
    )(proj3, proj3, proj3, proj3, lb.reshape(1, width), norm_w.reshape(1, HG_DV), s0)
    return o.reshape(b * t, width), s_fin


def _decoder_layer(x, past, lb, params):
    (w_in, w_a, w_b, w_o, pe, w1, w2, hg_norm, ln1, ln2, ln3, ln4, w_up, cw, cb, w_dn) = params
    b, t, d = x.shape
    m = b * t
    x2 = x.reshape(m, d)
    tm = min(MM_ROWS, m)
    proj = _in_projection(x2, ln1, *w_in, tm)
    kv_shape = (b, t, 2, NSA_GROUPS, NSA_HD)
    kvc, kvs, kvw = (a.reshape(kv_shape) for a in _kv_relayout(proj, min(CONV_ROWS, m)))
    if past is None:
        ckv = _compress_seq(proj, b, t, pe, w1, w2)
        o_a = _nsa_seq(proj, ckv, b, t)
        new_win = kvw[:, t - min(NSA_WINDOW, t):]
        s0, conv_buf = None, None
    else:
        cache_c, cache_s, page_table_flat, n_pages, page, win_buf, s0, conv_buf = past
        n_win = win_buf.shape[1]
        ckv = _compress_paged(cache_c, page_table_flat, b, n_pages, page, pe, w1, w2)
        o_a = _nsa_step(proj, ckv, cache_s, page_table_flat, win_buf.reshape(-1, NSA_HD), b, t, n_pages, page, n_win)
        new_win = jnp.concatenate([win_buf, kvw], axis=1)[:, t:]
    o_hg, s_fin = _hgrn(proj, lb, hg_norm, s0, b, t)
    mixed = _gated_merge(o_a.astype(BF16), o_hg, w_a, w_b, proj, tm, 512)
    x1 = _matmul_norm_res(mixed, w_o, x2, ln2, min(OUT_ROWS, m), w_o.shape[0])
    u = _norm_matmul(x1, ln3, w_up, tm, UP_TN)
    if past is None:
        y = _conv_glu_down(u, cw, cb, w_dn, x1, ln4, t, min(CONV_ROWS, t), 512)
        new_conv = u.reshape(b, t, 2 * D_FF)[:, t - (CONV_W - 1):]
    else:
        act = _conv_glu_step(u.reshape(b, t, 2 * D_FF), conv_buf, cw, cb, min(64, b), 512).reshape(m, D_FF)
        new_conv = jnp.concatenate([conv_buf, u.reshape(b, t, 2 * D_FF)], axis=1)[:, t:] if t < CONV_W - 1 else \
            u.reshape(b, t, 2 * D_FF)[:, t - (CONV_W - 1):]
        y = _matmul_norm_res(act, w_dn, x1, ln4, tm, 512)
    return y.reshape(b, t, d), (kvc, kvs, new_win, s_fin, new_conv)


def _split_w_in(w):
    o_rest = NSA_HEADS * NSA_HD + 3 * KV_COLS + 3 * NSA_HEADS
    edge = jnp.pad(w[:, :o_rest].astype(BF16), ((0, 0), (0, EDGE_COLS - o_rest)))
    return edge, w[:, o_rest:].astype(BF16)


def kernel(x_prompt, x_sample, cache_cmp_kv, cache_sel_kv, state_win_kv, state_hgrn, state_conv, page_table,
           w_in, w_branch_a, w_branch_b, w_out, cmp_pe, cmp_w1, cmp_w2, hg_lb_raw, hg_norm_w, ln_mix_pre,
           ln_mix_post, ln_ffn_pre, ln_ffn_post, w_up, conv_w, conv_b, w_down):
    depth = w_in.shape[0]
    dec_b, n_pages = page_table.shape
    page = cache_cmp_kv.shape[2]
    lb_all = jnp.cumsum(jax.nn.softmax(hg_lb_raw.astype(F32), axis=0), axis=0)
    pt_flat = page_table.reshape(-1).astype(jnp.int32)
    y_p, y_s = x_prompt, x_sample
    new_p, new_s = [], []
    for l in range(depth):
        params = (_split_w_in(w_in[l]), w_branch_a[l].astype(BF16), w_branch_b[l].astype(BF16),
                  w_out[l].astype(BF16), cmp_pe[l].transpose(1, 0, 2)[:, :, None, :],
                  cmp_w1[l].astype(BF16).transpose(1, 0, 2, 3).reshape(
                      2, NSA_BLOCK // CMP_STACK, CMP_STACK * NSA_HD, NSA_HD),
                  cmp_w2[l].astype(BF16), hg_norm_w[l], ln_mix_pre[l], ln_mix_post[l], ln_ffn_pre[l],
                  ln_ffn_post[l], w_up[l].astype(BF16), conv_w[l], conv_b[l], w_down[l].astype(BF16))
        y_p, st_p = _decoder_layer(y_p, None, lb_all[l], params)
        past = (cache_cmp_kv[l].reshape(-1, NSA_HD), cache_sel_kv[l].reshape(-1, NSA_HD),
                pt_flat, n_pages, page, state_win_kv[l], state_hgrn[l], state_conv[l])
        y_s, st_s = _decoder_layer(y_s, past, lb_all[l], params)
        new_p.append(st_p)
        new_s.append(st_s)

    def stack(group, i):
        return jnp.stack([st[i] for st in group], axis=0)

    return (y_p, y_s, stack(new_p, 0), stack(new_s, 0), stack(new_p, 1), stack(new_s, 1), stack(new_p, 2),
            stack(new_s, 2), stack(new_p, 3), stack(new_s, 3), stack(new_p, 4), stack(new_s, 4))
```

```python
import functools

import jax
import jax.numpy as jnp
from jax import lax
from jax.experimental import pallas as pl
from jax.experimental.pallas import tpu as pltpu

F32 = jnp.float32
BF16 = jnp.bfloat16

D_MODEL = 2048
NSA_HEADS = 8
NSA_GROUPS = 2
NSA_HPG = NSA_HEADS // NSA_GROUPS
NSA_HD = 128
NSA_BLOCK = 64
NSA_TOPN = 8
NSA_WINDOW = 512
NSA_SCALE = NSA_HD ** -0.5
FORCED_SCORE = NSA_HPG + 1.0
HG_HEADS = 8
HG_DK = 128
HG_DV = 128
HG_CHUNK = 16
D_FF = 5632
CONV_W = 3
EPS = 1e-6
KV_COLS = 2 * NSA_GROUPS * NSA_HD

C_QA = 0
C_QB = 1024
C_FB = 2048
C_IB = 3072
C_OG = 4096
C_GA = 5120
C_GB = 7168
C_KVC = 9216
C_KVS = 9728
C_KVW = 10240
C_GN = 10752

LANES = 128
MM_ROWS = 1024
UP_TN = 1024
OUT_ROWS = 512
LOG2E = 1.4426950408889634
CONV_ROWS = 512
DOWN_PARTS = 8
NEG_BIG = -1e30
VMEM_LIMIT = 56 * 1024 * 1024


def _cparams(*sem):
    return pltpu.CompilerParams(dimension_semantics=sem, vmem_limit_bytes=VMEM_LIMIT)


def _mm(a, b):
    return jnp.dot(a.astype(BF16), b.astype(BF16), preferred_element_type=F32)


def _mm_nt(a, b):
    return lax.dot_general(a.astype(BF16), b.astype(BF16), (((1,), (1,)), ((), ())),
                           preferred_element_type=F32)


def _norm_mm_kernel(x_ref, ln_ref, w_ref, o_ref, h_ref):
    @pl.when(pl.program_id(1) == 0)
    def _():
        x = x_ref[...]
        ms = jnp.mean(x * x, axis=-1, keepdims=True)
        h_ref[...] = (x * lax.rsqrt(ms + EPS) * ln_ref[...]).astype(BF16)

    o_ref[...] = _mm(h_ref[...], w_ref[...])


def _norm_matmul(x, ln, w, tm, tn):
    m, k = x.shape
    n = w.shape[1]
    return pl.pallas_call(
        _norm_mm_kernel,
        grid=(m // tm, n // tn),
        in_specs=[pl.BlockSpec((tm, k), lambda i, j: (i, 0)),
                  pl.BlockSpec((1, k), lambda i, j: (0, 0)),
                  pl.BlockSpec((k, tn), lambda i, j: (0, j))],
        out_specs=pl.BlockSpec((tm, tn), lambda i, j: (i, j)),
        out_shape=jax.ShapeDtypeStruct((m, n), F32),
        scratch_shapes=[pltpu.VMEM((tm, k), BF16)],
        compiler_params=_cparams("parallel", "arbitrary"),
        name="norm_matmul",
    )(x, ln.reshape(1, k), w)


PROJ_TN = 1024
TILE_REST0 = C_QB // PROJ_TN
TILE_KV0 = C_KVC // PROJ_TN
N_PROJ = C_KVC + 2 * PROJ_TN
EDGE_COLS = N_PROJ - (C_KVC - C_QB)


def _in_proj_kernel(x_ref, ln_ref, we_ref, wr_ref, o_ref, h_ref):
    j = pl.program_id(1)

    @pl.when(j == 0)
    def _():
        x = x_ref[...]
        ms = jnp.mean(x * x, axis=-1, keepdims=True)
        h_ref[...] = (x * lax.rsqrt(ms + EPS) * ln_ref[...]).astype(BF16)

    from_rest = (j >= TILE_REST0) & (j < TILE_KV0)

    @pl.when(from_rest)
    def _():
        o_ref[...] = _mm(h_ref[...], wr_ref[...])

    @pl.when(jnp.logical_not(from_rest))
    def _():
        o_ref[...] = _mm(h_ref[...], we_ref[...])


def _in_projection(x, ln, w_edge, w_rest, tm):
    m, k = x.shape
    tn = PROJ_TN
    n_rest = w_rest.shape[1] // tn
    assert n_rest == TILE_KV0 - TILE_REST0 and w_edge.shape[1] == EDGE_COLS and TILE_REST0 == 1
    edge_tile = lambda j: jnp.clip(j - (TILE_KV0 - TILE_REST0), 0, EDGE_COLS // tn - 1)
    rest_tile = lambda j: jnp.clip(j - TILE_REST0, 0, n_rest - 1)
    return pl.pallas_call(
        _in_proj_kernel,
        grid=(m // tm, N_PROJ // tn),
        in_specs=[pl.BlockSpec((tm, k), lambda i, j: (i, 0)),
                  pl.BlockSpec((1, k), lambda i, j: (0, 0)),
                  pl.BlockSpec((k, tn), lambda i, j: (0, edge_tile(j))),
                  pl.BlockSpec((k, tn), lambda i, j: (0, rest_tile(j)))],
        out_specs=pl.BlockSpec((tm, tn), lambda i, j: (i, j)),
        out_shape=jax.ShapeDtypeStruct((m, N_PROJ), F32),
        scratch_shapes=[pltpu.VMEM((tm, k), BF16)],
        compiler_params=_cparams("parallel", "arbitrary"),
        name="in_projection",
    )(x, ln.reshape(1, k), w_edge, w_rest)


def _kv_relayout_kernel(c_ref, s_ref, w_ref, oc_ref, os_ref, ow_ref, *, tm):
    for x_ref, o_ref in ((c_ref, oc_ref), (s_ref, os_ref), (w_ref, ow_ref)):
        for slot in range(2 * NSA_GROUPS):
            o_ref[pl.ds(slot, tm, stride=2 * NSA_GROUPS), :] = x_ref[:, slot * NSA_HD:(slot + 1) * NSA_HD]


def _kv_relayout(proj, tm):
    m = proj.shape[0]
    slots = 2 * NSA_GROUPS
    out = jax.ShapeDtypeStruct((m * slots, NSA_HD), F32)
    return pl.pallas_call(
        functools.partial(_kv_relayout_kernel, tm=tm),
        grid=(m // tm,),
        in_specs=[pl.BlockSpec((tm, KV_COLS), lambda i, c=c: (i, c // KV_COLS)) for c in (C_KVC, C_KVS, C_KVW)],
        out_specs=[pl.BlockSpec((tm * slots, NSA_HD), lambda i: (i, 0))] * 3,
        out_shape=[out] * 3,
        compiler_params=_cparams("parallel"),
        name="kv_relayout",
    )(proj, proj, proj)


def _mm_norm_res_kernel(a_ref, w_ref, res_ref, ln_ref, o_ref, acc_ref, *, nk):
    k = pl.program_id(1)

    @pl.when(k == 0)
    def _():
        acc_ref[...] = jnp.zeros_like(acc_ref)

    acc_ref[...] += _mm(a_ref[...], w_ref[...])

    @pl.when(k == nk - 1)
    def _():
        y = acc_ref[...]
        ms = jnp.mean(y * y, axis=-1, keepdims=True)
        o_ref[...] = res_ref[...] + y * lax.rsqrt(ms + EPS) * ln_ref[...]


def _matmul_norm_res(a, w, res, ln, tm, tk):
    m, kk = a.shape
    n = w.shape[1]
    nk = kk // tk
    return pl.pallas_call(
        functools.partial(_mm_norm_res_kernel, nk=nk),
        grid=(m // tm, nk),
        in_specs=[pl.BlockSpec((tm, tk), lambda i, k: (i, k)),
                  pl.BlockSpec((tk, n), lambda i, k: (k, 0)),
                  pl.BlockSpec((tm, n), lambda i, k: (i, 0)),
                  pl.BlockSpec((1, n), lambda i, k: (0, 0))],
        out_specs=pl.BlockSpec((tm, n), lambda i, k: (i, 0)),
        out_shape=jax.ShapeDtypeStruct((m, n), F32),
        scratch_shapes=[pltpu.VMEM((tm, n), F32)],
        compiler_params=_cparams("parallel", "arbitrary"),
        name="matmul_norm_res",
    )(a, w, res, ln.reshape(1, n))


def _merge_kernel(oa_ref, ohg_ref, wa_ref, wb_ref, ga_ref, gb_ref, o_ref):
    a = _mm(oa_ref[...], wa_ref[...])
    b = _mm(ohg_ref[...], wb_ref[...])
    o_ref[...] = (jax.nn.sigmoid(ga_ref[...]) * a + jax.nn.sigmoid(gb_ref[...]) * b).astype(BF16)


def _gated_merge(o_a, o_hg, w_a, w_b, proj, tm, tn):
    m, k = o_a.shape
    n = w_a.shape[1]
    return pl.pallas_call(
        _merge_kernel,
        grid=(m // tm, n // tn),
        in_specs=[pl.BlockSpec((tm, k), lambda i, j: (i, 0)),
                  pl.BlockSpec((tm, k), lambda i, j: (i, 0)),
                  pl.BlockSpec((k, tn), lambda i, j: (0, j)),
                  pl.BlockSpec((k, tn), lambda i, j: (0, j)),
                  pl.BlockSpec((tm, tn), lambda i, j: (i, C_GA // tn + j)),
                  pl.BlockSpec((tm, tn), lambda i, j: (i, C_GB // tn + j))],
        out_specs=pl.BlockSpec((tm, tn), lambda i, j: (i, j)),
        out_shape=jax.ShapeDtypeStruct((m, n), BF16),
        compiler_params=_cparams("parallel", "arbitrary"),
        name="gated_merge",
    )(o_a, o_hg, w_a, w_b, proj, proj)


def _conv_taps(fa, fg, cwa_ref, cwg_ref, cba_ref, cbg_ref, tap):
    ca = cba_ref[...]
    cg = cbg_ref[...]
    for j in range(CONV_W):
        ca = ca + tap(fa, j) * cwa_ref[j:j + 1, :]
        cg = cg + tap(fg, j) * cwg_ref[j:j + 1, :]
    return jax.nn.gelu(cg, approximate=True) * ca


def _conv_glu_down_kernel(ua_ref, ug_ref, ha_ref, hg_ref, cwa_ref, cwg_ref, cba_ref, cbg_ref, w_ref, res_ref,
                          ln_ref, o_ref, fa_ref, fg_ref, act_ref, acc_ref, *, tm, tiles_per_seq, nk):
    k = pl.program_id(1)
    slot = k % 2

    @pl.when(k == 0)
    def _():
        acc_ref[...] = jnp.zeros_like(acc_ref)
        act_ref[1] = jnp.zeros(act_ref.shape[1:], act_ref.dtype)

    first = (pl.program_id(0) % tiles_per_seq) == 0
    fa_ref[0:8, :] = jnp.where(first, 0.0, ha_ref[...])
    fg_ref[0:8, :] = jnp.where(first, 0.0, hg_ref[...])
    fa_ref[8:, :] = ua_ref[...]
    fg_ref[8:, :] = ug_ref[...]
    n_parts = DOWN_PARTS
    cols = acc_ref.shape[1] // n_parts
    rows = tm // n_parts
    for part in range(n_parts):
        cs = slice(part * cols, (part + 1) * cols)
        acc_ref[:, cs] += _mm(act_ref[1 - slot], w_ref[:, cs])
        r0 = part * rows
        xa, xg = fa_ref[r0:r0 + rows + 8, :], fg_ref[r0:r0 + rows + 8, :]
        tap = lambda x, j: (x if j == CONV_W - 1 else pltpu.roll(x, CONV_W - 1 - j, axis=0))[8:]
        act = _conv_taps(xa, xg, cwa_ref, cwg_ref, cba_ref, cbg_ref, tap)
        act_ref[slot, r0:r0 + rows, :] = act.astype(act_ref.dtype)

    @pl.when(k == nk)
    def _():
        y = acc_ref[...]
        ms = jnp.mean(y * y, axis=-1, keepdims=True)
        o_ref[...] = res_ref[...] + y * lax.rsqrt(ms + EPS) * ln_ref[...]


def _conv_glu_down(u, conv_w, conv_b, w, res, ln, seq_len, tm, tk):
    m = u.shape[0]
    n = w.shape[1]
    nk = D_FF // tk
    hb = tm // 8
    kc = lambda k: jnp.minimum(k, nk - 1)
    kw = lambda k: jnp.maximum(k - 1, 0)
    halo = lambda i, k, off: (jnp.maximum(i * hb - 1, 0), kc(k) + off)
    return pl.pallas_call(
        functools.partial(_conv_glu_down_kernel, tm=tm, tiles_per_seq=seq_len // tm, nk=nk),
        grid=(m // tm, nk + 1),
        in_specs=[pl.BlockSpec((tm, tk), lambda i, k: (i, kc(k))),
                  pl.BlockSpec((tm, tk), lambda i, k: (i, kc(k) + nk)),
                  pl.BlockSpec((8, tk), lambda i, k: halo(i, k, 0)),
                  pl.BlockSpec((8, tk), lambda i, k: halo(i, k, nk)),
                  pl.BlockSpec((CONV_W, tk), lambda i, k: (0, kc(k))),
                  pl.BlockSpec((CONV_W, tk), lambda i, k: (0, kc(k) + nk)),
                  pl.BlockSpec((1, tk), lambda i, k: (0, kc(k))),
                  pl.BlockSpec((1, tk), lambda i, k: (0, kc(k) + nk)),
                  pl.BlockSpec((tk, n), lambda i, k: (kw(k), 0)),
                  pl.BlockSpec((tm, n), lambda i, k: (i, 0)),
                  pl.BlockSpec((1, n), lambda i, k: (0, 0))],
        out_specs=pl.BlockSpec((tm, n), lambda i, k: (i, 0)),
        out_shape=jax.ShapeDtypeStruct((m, n), F32),
        scratch_shapes=[pltpu.VMEM((tm + 8, tk), F32), pltpu.VMEM((tm + 8, tk), F32),
                        pltpu.VMEM((2, tm, tk), BF16), pltpu.VMEM((tm, n), F32)],
        compiler_params=_cparams("parallel", "arbitrary"),
        name="conv_glu_down",
    )(u, u, u, u, conv_w, conv_w, conv_b.reshape(1, -1), conv_b.reshape(1, -1), w, res, ln.reshape(1, n))


def _conv_glu_step_kernel(ua_ref, ug_ref, ba_ref, bg_ref, cwa_ref, cwg_ref, cba_ref, cbg_ref,
                          o_ref, fa_ref, fg_ref, *, t):
    fa_ref[:, 8 - (CONV_W - 1):8, :] = ba_ref[...]
    fg_ref[:, 8 - (CONV_W - 1):8, :] = bg_ref[...]
    fa_ref[:, 8:, :] = ua_ref[...]
    fg_ref[:, 8:, :] = ug_ref[...]
    tap = lambda f, j: f[:, 8 - (CONV_W - 1) + j:8 - (CONV_W - 1) + j + t, :]
    o_ref[...] = _conv_taps(fa_ref, fg_ref, cwa_ref, cwg_ref, cba_ref, cbg_ref, tap)


def _conv_glu_step(u3, buf, conv_w, conv_b, nb, tn):
    b, t, _ = u3.shape
    nj = D_FF // tn
    return pl.pallas_call(
        functools.partial(_conv_glu_step_kernel, t=t),
        grid=(b // nb, nj),
        in_specs=[pl.BlockSpec((nb, t, tn), lambda i, j: (i, 0, j)),
                  pl.BlockSpec((nb, t, tn), lambda i, j: (i, 0, j + nj)),
                  pl.BlockSpec((nb, CONV_W - 1, tn), lambda i, j: (i, 0, j)),
                  pl.BlockSpec((nb, CONV_W - 1, tn), lambda i, j: (i, 0, j + nj)),
                  pl.BlockSpec((CONV_W, tn), lambda i, j: (0, j)),
                  pl.BlockSpec((CONV_W, tn), lambda i, j: (0, j + nj)),
                  pl.BlockSpec((1, tn), lambda i, j: (0, j)),
                  pl.BlockSpec((1, tn), lambda i, j: (0, j + nj))],
        out_specs=pl.BlockSpec((nb, t, tn), lambda i, j: (i, 0, j)),
        out_shape=jax.ShapeDtypeStruct((b, t, D_FF), F32),
        scratch_shapes=[pltpu.VMEM((nb, 8 + t, tn), F32), pltpu.VMEM((nb, 8 + t, tn), F32)],
        compiler_params=_cparams("parallel", "arbitrary"),
        name="conv_glu_step",
    )(u3, u3, buf, buf, conv_w, conv_w, conv_b.reshape(1, -1), conv_b.reshape(1, -1))


def _compress_body(load_x, pe_ref, w1_ref, w2_ref, o_ref, n_seq, n_cmp):
    for c in range(2):
        acc = jnp.zeros((NSA_GROUPS * n_seq * n_cmp, NSA_HD), F32)
        for l in range(0, NSA_BLOCK, CMP_STACK):
            parts = [jnp.concatenate([load_x(l + dl, c, g) for g in range(NSA_GROUPS)], axis=0) + pe_ref[c, l + dl]
                     for dl in range(CMP_STACK)]
            acc = acc + _mm(jnp.concatenate(parts, axis=1), w1_ref[c, l // CMP_STACK])
        out = _mm(jax.nn.silu(acc), w2_ref[c])
        for g in range(NSA_GROUPS):
            for s in range(n_seq):
                r0 = (g * n_seq + s) * n_cmp
                o_ref[s, c, g] = out[r0:r0 + n_cmp]


def _compress_seq_kernel(x0_ref, x1_ref, x2_ref, x3_ref, pe_ref, w1_ref, w2_ref, o_ref, *, n_cmp):
    xs = (x0_ref, x1_ref, x2_ref, x3_ref)
    load_x = lambda l, c, g: xs[c * NSA_GROUPS + g][pl.ds(l, n_cmp, stride=NSA_BLOCK), :]
    _compress_body(load_x, pe_ref, w1_ref, w2_ref, o_ref, 1, n_cmp)


def _compress_seq(proj, b, t, pe, w1, w2):
    n_cmp = t // NSA_BLOCK
    rows = n_cmp * NSA_BLOCK
    assert rows == t
    return pl.pallas_call(
        functools.partial(_compress_seq_kernel, n_cmp=n_cmp),
        grid=(b,),
        in_specs=[pl.BlockSpec((rows, NSA_HD), lambda i, cg=cg: (i, C_KVC // NSA_HD + cg))
                  for cg in range(2 * NSA_GROUPS)] + [
                  pl.BlockSpec(pe.shape, lambda i: (0, 0, 0, 0)),
                  pl.BlockSpec(w1.shape, lambda i: (0, 0, 0, 0)),
                  pl.BlockSpec(w2.shape, lambda i: (0, 0, 0))],
        out_specs=pl.BlockSpec((1, 2, NSA_GROUPS, n_cmp, NSA_HD), lambda i: (i, 0, 0, 0, 0)),
        out_shape=jax.ShapeDtypeStruct((b, 2, NSA_GROUPS, n_cmp, NSA_HD), F32),
        compiler_params=_cparams("parallel"),
        name="compress_seq",
    )(proj, proj, proj, proj, pe, w1, w2)


KV_SLOTS = 2 * NSA_GROUPS
CMP_SEQS = 2
CMP_STACK = 2
CMP_PITCH = NSA_BLOCK * KV_SLOTS + 8


def _compress_paged_kernel(pt_ref, *refs, n_seq, n_pages, page_rows):
    del pt_ref
    pages = refs[:n_seq * n_pages]
    pe_ref, w1_ref, w2_ref, o_ref, x_ref = refs[n_seq * n_pages:]
    blk_rows = NSA_BLOCK * KV_SLOTS
    per_page = page_rows // blk_rows
    for i, pg in enumerate(pages):
        for j in range(per_page):
            n = i * per_page + j
            x_ref[n * CMP_PITCH:n * CMP_PITCH + blk_rows, :] = pg[j * blk_rows:(j + 1) * blk_rows, :]
    n_cmp = n_pages * per_page
    load_x = lambda l, c, g: x_ref[pl.ds(l * KV_SLOTS + c * NSA_GROUPS + g, n_seq * n_cmp, stride=CMP_PITCH), :]
    _compress_body(load_x, pe_ref, w1_ref, w2_ref, o_ref, n_seq, n_cmp)


def _compress_paged(cache2d, page_table_flat, b, n_pages, page, pe, w1, w2):
    page_rows = page * KV_SLOTS
    n_cmp = n_pages * page // NSA_BLOCK
    n_seq = CMP_SEQS if b % CMP_SEQS == 0 else 1
    page_spec = lambda s, p: pl.BlockSpec(
        (page_rows, NSA_HD), lambda i, pt: (pt[(i * n_seq + s) * n_pages + p], 0))
    grid_spec = pltpu.PrefetchScalarGridSpec(
        num_scalar_prefetch=1,
        grid=(b // n_seq,),
        in_specs=[page_spec(s, p) for s in range(n_seq) for p in range(n_pages)] + [
            pl.BlockSpec(pe.shape, lambda i, pt: (0, 0, 0, 0)),
            pl.BlockSpec(w1.shape, lambda i, pt: (0, 0, 0, 0)),
            pl.BlockSpec(w2.shape, lambda i, pt: (0, 0, 0))],
        out_specs=pl.BlockSpec((n_seq, 2, NSA_GROUPS, n_cmp, NSA_HD), lambda i, pt: (i, 0, 0, 0, 0)),
        scratch_shapes=[pltpu.VMEM((n_seq * n_cmp * CMP_PITCH, NSA_HD), F32)],
    )
    return pl.pallas_call(
        functools.partial(_compress_paged_kernel, n_seq=n_seq, n_pages=n_pages, page_rows=page_rows),
        grid_spec=grid_spec,
        out_shape=jax.ShapeDtypeStruct((b, 2, NSA_GROUPS, n_cmp, NSA_HD), F32),
        compiler_params=_cparams("parallel"),
        name="compress_paged",
    )(page_table_flat, *([cache2d] * (n_seq * n_pages)), pe, w1, w2)


def _alibi_slope(g, h):
    return 2.0 ** (-8.0 * (g * NSA_HPG + h + 1.0) / NSA_HEADS)


def _row_consts(tq, g, q_pos0):
    rows = NSA_HPG * tq
    r = lax.broadcasted_iota(jnp.int32, (rows, 1), 0)
    head = r // tq
    qpos = q_pos0 + (r - head * tq)
    slope = jnp.zeros((rows, 1), F32)
    for h in range(NSA_HPG):
        slope = jnp.where(head == h, _alibi_slope(g, h), slope)
    return qpos, slope


def _stack_heads(q_ref, g):
    return jnp.concatenate(
        [q_ref[:, (g * NSA_HPG + h) * NSA_HD:(g * NSA_HPG + h + 1) * NSA_HD] for h in range(NSA_HPG)], axis=0)


def _compressed_branches(units, n_cmp):
    pad = jnp.zeros((LANES - n_cmp, NSA_HD), F32)
    n = lax.broadcasted_iota(jnp.int32, (1, LANES), 1)
    scores = []
    for qs, ck, _, qpos, slope in units:
        dist = qpos - ((n + 1) * NSA_BLOCK - 1)
        mask = (dist >= 0) & (n < n_cmp)
        s = _mm_nt(qs, jnp.concatenate([ck, pad], axis=0)) * NSA_SCALE - slope * dist.astype(F32)
        scores.append((jnp.where(mask, s, NEG_BIG), mask))
    probs = []
    for s, mask in scores:
        m = jnp.max(s, axis=-1, keepdims=True)
        e = jnp.where(mask, jnp.exp(s - m), 0.0)
        d = jnp.sum(e, axis=-1, keepdims=True)
        probs.append(e / jnp.where(d > 0, d, 1.0))
    return [(_mm(p, jnp.concatenate([cv, pad], axis=0)), p) for p, (_, _, cv, _, _) in zip(probs, units)]


def _select_blocks(imp, q_pos, n_blk):
    tq = imp.shape[0]
    nb = -(-n_blk // 8) * 8
    imp_t = (imp if tq == LANES else _pad_rows(imp, LANES)).T[0:nb]
    blk = lax.broadcasted_iota(jnp.int32, (nb, LANES), 0)
    cur = q_pos // NSA_BLOCK
    forced = (blk == 0) | (blk == cur) | (blk == cur - 1)
    valid = blk <= cur
    score = jnp.where(valid, jnp.where(forced, FORCED_SCORE, imp_t), -1.0)
    score = jnp.where(blk < n_blk, score, -2.0)
    beaten_by = jnp.zeros((nb, LANES), F32)
    for j in range(n_blk):
        sj = score[j:j + 1, :]
        beaten_by = beaten_by + jnp.where((sj > score) | ((sj == score) & (blk > j)), 1.0, 0.0)
    sel = jnp.where(beaten_by < min(NSA_TOPN, n_blk), 1.0, 0.0)
    return _pad_rows(sel, LANES).T[0:tq]


def _expand_sel(sel, key0, nkeys):
    bi = lax.broadcasted_iota(jnp.int32, (LANES, nkeys), 0)
    ki = lax.broadcasted_iota(jnp.int32, (LANES, nkeys), 1)
    expand = (bi == (key0 + ki) // NSA_BLOCK).astype(BF16)
    return _mm(sel, expand)


def _write_gated(o_ref, gate_ref, g, tq, o_cmp, o_sel, o_win):
    sig = jax.nn.sigmoid(gate_ref[...])
    for h in range(NSA_HPG):
        c0 = (g * NSA_HPG + h) * 3
        rs = slice(h * tq, (h + 1) * tq)
        o = sig[:, c0:c0 + 1] * o_cmp[rs] + sig[:, c0 + 1:c0 + 2] * o_sel[rs] + sig[:, c0 + 2:c0 + 3] * o_win[rs]
        o_ref[:, (g * NSA_HPG + h) * NSA_HD:(g * NSA_HPG + h + 1) * NSA_HD] = o.astype(o_ref.dtype)


SEQ_TK = 256


def _lane_groups(x):
    return [x[:, i:i + LANES] for i in range(0, x.shape[1], LANES)]


def _tiled_attention(heads, kv_ref, lo, hi, tile_masks, s_ref, m_ref, l_ref, acc_ref, with_tile0=False):
    tq = m_ref.shape[0] // len(heads)
    head_rows = [slice(i * tq, (i + 1) * tq) for i in range(len(heads))]
    groups = sorted({g for _, _, g in heads})
    kcol = lambda g: slice(g * NSA_HD, (g + 1) * NSA_HD)
    vcol = lambda g: slice((NSA_GROUPS + g) * NSA_HD, (NSA_GROUPS + g + 1) * NSA_HD)
    m_ref[...] = jnp.full(m_ref.shape, NEG_BIG, F32)

    def scores(kt, carry):
        key0 = kt * SEQ_TK
        masks = tile_masks(key0)
        rows = pl.ds(pl.multiple_of(key0, SEQ_TK), SEQ_TK)
        k = {g: kv_ref[rows, kcol(g)] for g in groups}
        kpos = (key0 + lax.broadcasted_iota(jnp.int32, (1, SEQ_TK), 1)).astype(F32)
        for (load_q, slope, g), rs in zip(heads, head_rows):
            valid = masks[g]
            s = jnp.where(valid, _mm_nt(load_q(), k[g]) * (NSA_SCALE * LOG2E) + (slope * LOG2E) * kpos, NEG_BIG)
            s_ref[kt, rs] = s
            m_ref[rs] = functools.reduce(jnp.maximum, [m_ref[rs]] + _lane_groups(s))
        return carry

    if with_tile0:
        scores(jnp.int32(0), 0)
    lax.fori_loop(lo, hi, scores, 0)
    for rs in head_rows:
        m = jnp.maximum(jnp.max(m_ref[rs], axis=-1, keepdims=True), 0.1 * NEG_BIG)
        m_ref[rs] = jnp.broadcast_to(m, (tq, LANES))
    l_ref[...] = jnp.zeros(l_ref.shape, F32)
    acc_ref[...] = jnp.zeros(acc_ref.shape, F32)

    def values(kt, carry):
        rows = pl.ds(pl.multiple_of(kt * SEQ_TK, SEQ_TK), SEQ_TK)
        v = {g: kv_ref[rows, vcol(g)] for g in groups}
        for (_, _, g), rs in zip(heads, head_rows):
            m = m_ref[rs]
            p = [jnp.exp2(s - m) for s in _lane_groups(s_ref[kt, rs])]
            l_ref[rs] += sum(p[1:], p[0])
            acc_ref[rs] += _mm(jnp.concatenate(p, axis=1), v[g])
        return carry

    if with_tile0:
        values(jnp.int32(0), 0)
    lax.fori_loop(lo, hi, values, 0)
    outs = []
    for rs in head_rows:
        l = jnp.sum(l_ref[rs], axis=-1, keepdims=True)
        outs.append(acc_ref[rs] / jnp.where(l > 0, l, 1.0))
    return jnp.concatenate(outs, axis=0)


def _nsa_seq_kernel(q_ref, gate_ref, ckv_ref, ks_ref, kw_ref, o_ref, s_ref, m_ref, l_ref, acc_ref,
                    *, tq, n_cmp, n_blk):
    j = pl.program_id(1)
    q0 = j * tq
    lane = lax.broadcasted_iota(jnp.int32, (1, SEQ_TK), 1)
    qpos_t = q0 + lax.broadcasted_iota(jnp.int32, (tq, 1), 0)
    hi = (q0 + tq - 1) // SEQ_TK + 1
    cmp_out = _compressed_branches(
        [(_stack_heads(q_ref, g), ckv_ref[0, 0, g], ckv_ref[0, 1, g]) + _row_consts(tq, g, q0)
         for g in range(NSA_GROUPS)], n_cmp)
    o_cmp, sel = [], []
    for o_g, p_c in cmp_out:
        imp = p_c[0:tq]
        for h in range(1, NSA_HPG):
            imp = imp + p_c[h * tq:(h + 1) * tq]
        o_cmp.append(o_g)
        sel.append(_select_blocks(imp, q0 + lax.broadcasted_iota(jnp.int32, (1, LANES), 1), n_blk))
    order = [(g, h) for h in range(NSA_HPG) for g in range(NSA_GROUPS)]
    heads = [(lambda c=(g * NSA_HPG + h) * NSA_HD: q_ref[:, c:c + NSA_HD], _alibi_slope(g, h), g) for g, h in order]
    scratch = (s_ref, m_ref, l_ref, acc_ref)

    def sel_masks(key0):
        dist = qpos_t - (key0 + lane)
        return [(dist >= 0) & (_expand_sel(sel[g], key0, SEQ_TK) > 0.5) for g in range(NSA_GROUPS)]

    picked = functools.reduce(jnp.maximum, sel)
    blk = lax.broadcasted_iota(jnp.int32, picked.shape, 1)
    first_blk = jnp.min(jnp.where((picked > 0.5) & (blk >= 1), blk.astype(F32), float(LANES)))
    first_tile = jnp.clip(first_blk.astype(jnp.int32) // (SEQ_TK // NSA_BLOCK), 1, hi)
    o_sel = _tiled_attention(heads, ks_ref, first_tile, hi, sel_masks, *scratch, with_tile0=True)

    def win_masks(key0):
        dist = qpos_t - (key0 + lane)
        return [(dist >= 0) & (dist < NSA_WINDOW)] * NSA_GROUPS

    lo = jnp.maximum(q0 - (NSA_WINDOW - 1), 0) // SEQ_TK
    o_win = _tiled_attention(heads, kw_ref, lo, hi, win_masks, *scratch)
    sig = jax.nn.sigmoid(gate_ref[...])
    for i, (g, h) in enumerate(order):
        c0 = (g * NSA_HPG + h) * 3
        rs = slice(i * tq, (i + 1) * tq)
        o = (sig[:, c0:c0 + 1] * o_cmp[g][h * tq:(h + 1) * tq] + sig[:, c0 + 1:c0 + 2] * o_sel[rs]
             + sig[:, c0 + 2:c0 + 3] * o_win[rs])
        o_ref[:, (g * NSA_HPG + h) * NSA_HD:(g * NSA_HPG + h + 1) * NSA_HD] = o.astype(o_ref.dtype)


def _nsa_seq(proj, ckv, b, t):
    tq = 128
    n_cmp = t // NSA_BLOCK
    n_blk = -(-t // NSA_BLOCK)
    nq = t // tq
    assert t % SEQ_TK == 0 and n_blk <= LANES
    return pl.pallas_call(
        functools.partial(_nsa_seq_kernel, tq=tq, n_cmp=n_cmp, n_blk=n_blk),
        scratch_shapes=[pltpu.VMEM((t // SEQ_TK, NSA_HEADS * tq, SEQ_TK), F32)]
        + [pltpu.VMEM((NSA_HEADS * tq, LANES), F32)] * 3,
        grid=(b, nq),
        in_specs=[pl.BlockSpec((tq, NSA_HEADS * NSA_HD), lambda i, j: (i * nq + j, C_QA // 1024)),
                  pl.BlockSpec((tq, LANES), lambda i, j: (i * nq + j, C_GN // LANES)),
                  pl.BlockSpec((1, 2, NSA_GROUPS, n_cmp, NSA_HD), lambda i, j: (i, 0, 0, 0, 0)),
                  pl.BlockSpec((t, KV_COLS), lambda i, j: (i, C_KVS // KV_COLS)),
                  pl.BlockSpec((t, KV_COLS), lambda i, j: (i, C_KVW // KV_COLS))],
        out_specs=pl.BlockSpec((tq, NSA_HEADS * NSA_HD), lambda i, j: (i * nq + j, 0)),
        out_shape=jax.ShapeDtypeStruct((b * t, NSA_HEADS * NSA_HD), BF16),
        compiler_params=_cparams("parallel", "arbitrary"),
        name="nsa_seq",
    )(proj, proj, ckv, proj, proj)


def _pad_rows(x, rows):
    if x.shape[0] == rows:
        return x
    return jnp.concatenate([x, jnp.zeros((rows - x.shape[0], x.shape[1]), x.dtype)], axis=0)


def _two_pass_attention(problems):
    n_tiles = max(len(tiles) for _, tiles in problems)
    scores = [[] for _ in problems]
    for i in range(n_tiles):
        for u, (qs, tiles) in enumerate(problems):
            if i < len(tiles):
                k_fn, _, mask_fn = tiles[i]
                valid, bias = mask_fn()
                scores[u].append(jnp.where(valid, _mm_nt(qs, k_fn()) * NSA_SCALE - bias, NEG_BIG))
    ms = [jnp.max(functools.reduce(jnp.maximum, sc), axis=-1, keepdims=True) for sc in scores]
    accs = [jnp.zeros((qs.shape[0], NSA_HD), F32) for qs, _ in problems]
    lsums = [jnp.zeros(sc[0].shape, F32) for sc in scores]
    for i in range(n_tiles):
        for u, (_, tiles) in enumerate(problems):
            if i < len(tiles):
                s = scores[u][i]
                p = jnp.where(s > 0.5 * NEG_BIG, jnp.exp(s - ms[u]), 0.0)
                lsums[u] = lsums[u] + p
                accs[u] = accs[u] + _mm(p, tiles[i][1]())
    outs = []
    for acc, lsum in zip(accs, lsums):
        l = jnp.sum(lsum, axis=-1, keepdims=True)
        outs.append(acc / jnp.where(l > 0, l, 1.0))
    return outs


STEP_SEQS = 4


def _nsa_step_kernel(pt_ref, *refs, n_seq, t, n_pages, page, past_len, n_win, n_cmp, n_blk):
    del pt_ref
    q_ref, gate_ref, ckv_ref, ksn_ref, kwn_ref, win_ref, o_ref = refs[n_seq * n_pages:]
    lane = lax.broadcasted_iota(jnp.int32, (1, page), 1)
    blocks_per_page = page // NSA_BLOCK
    win_rows = n_win * KV_SLOTS
    units = [(s, g) for s in range(n_seq) for g in range(NSA_GROUPS)]
    seq_rows = lambda s: pl.ds(s * t, t)

    cmp_in = [(_stack_heads(q_ref.at[seq_rows(s)], g), ckv_ref[s, 0, g], ckv_ref[s, 1, g])
              + _row_consts(t, g, past_len) for s, g in units]
    prep = []
    for (qs, _, _, qpos, slope), (o_cmp, p_c) in zip(cmp_in, _compressed_branches(cmp_in, n_cmp)):
        imp = p_c[0:t]
        for h in range(1, NSA_HPG):
            imp = imp + p_c[h * t:(h + 1) * t]
        prep.append((qs, qpos, slope, o_cmp, imp))
    q_pos = past_len + lax.broadcasted_iota(jnp.int32, (1, LANES), 1) % t
    sel_all = _select_blocks(jnp.concatenate([p[4] for p in prep], axis=0), q_pos, n_blk)
    prep = [p[:4] + (sel_all[u * t:(u + 1) * t],) for u, p in enumerate(prep)]

    def new_rows(ref, s, g, value):
        col = ((NSA_GROUPS if value else 0) + g) * NSA_HD
        return lambda: _pad_rows(ref[seq_rows(s), col:col + NSA_HD], page)

    def strided(ref, row0, g, value):
        slot = (NSA_GROUPS if value else 0) + g
        return lambda: ref[pl.ds(row0 + slot, page, stride=KV_SLOTS), :]

    sel_problems, win_problems = [], []
    for (s, g), (qs, qpos, slope, _, sel) in zip(units, prep):
        tiles = []
        for p in range(n_pages + 1):
            def sel_mask(p=p, qpos=qpos, slope=slope, sel=sel):
                chosen = jnp.zeros((t, page), F32)
                for bi in range(blocks_per_page):
                    blk = p * blocks_per_page + bi
                    chosen = jnp.where(lane // NSA_BLOCK == bi, sel[:, blk:blk + 1], chosen)
                chosen = jnp.concatenate([chosen] * NSA_HPG, axis=0)
                dist = qpos - (p * page + lane)
                return (dist >= 0) & (chosen > 0.5), slope * dist.astype(F32)

            if p < n_pages:
                pg = refs[s * n_pages + p]
                tiles.append((strided(pg, 0, g, False), strided(pg, 0, g, True), sel_mask))
            else:
                tiles.append((new_rows(ksn_ref, s, g, False), new_rows(ksn_ref, s, g, True), sel_mask))
        sel_problems.append((qs, tiles))

        tiles = []
        for p in range(n_win // page + 1):
            def win_mask(p=p, qpos=qpos, slope=slope):
                kpos = past_len - n_win + p * page + lane
                dist = qpos - kpos
                return (dist >= 0) & (dist < NSA_WINDOW) & (kpos >= 0), slope * dist.astype(F32)

            if p < n_win // page:
                row0 = s * win_rows + p * page * KV_SLOTS
                tiles.append((strided(win_ref, row0, g, False), strided(win_ref, row0, g, True), win_mask))
            else:
                tiles.append((new_rows(kwn_ref, s, g, False), new_rows(kwn_ref, s, g, True), win_mask))
        win_problems.append((qs, tiles))

    o_sel = _two_pass_attention(sel_problems)
    o_win = _two_pass_attention(win_problems)
    for u, (s, g) in enumerate(units):
        _write_gated(o_ref.at[seq_rows(s)], gate_ref.at[seq_rows(s)], g, t, prep[u][3], o_sel[u], o_win[u])


def _nsa_step(proj, ckv, cache2d, page_table_flat, win2d, b, t, n_pages, page, n_win):
    past_len = n_pages * page
    assert n_win % page == 0 and t <= page
    n_cmp = (past_len + t) // NSA_BLOCK
    n_blk = -(-(past_len + t) // NSA_BLOCK)
    assert n_cmp * NSA_BLOCK == past_len and n_blk <= LANES
    n_seq = STEP_SEQS if b % STEP_SEQS == 0 else 1
    page_spec = lambda s, p: pl.BlockSpec(
        (page * KV_SLOTS, NSA_HD), lambda i, pt: (pt[(i * n_seq + s) * n_pages + p], 0))
    grid_spec = pltpu.PrefetchScalarGridSpec(
        num_scalar_prefetch=1,
        grid=(b // n_seq,),
        in_specs=[page_spec(s, p) for s in range(n_seq) for p in range(n_pages)] + [
            pl.BlockSpec((n_seq * t, NSA_HEADS * NSA_HD), lambda i, pt: (i, C_QA // 1024)),
            pl.BlockSpec((n_seq * t, LANES), lambda i, pt: (i, C_GN // LANES)),
            pl.BlockSpec((n_seq, 2, NSA_GROUPS, n_cmp, NSA_HD), lambda i, pt: (i, 0, 0, 0, 0)),
            pl.BlockSpec((n_seq * t, KV_COLS), lambda i, pt: (i, C_KVS // KV_COLS)),
            pl.BlockSpec((n_seq * t, KV_COLS), lambda i, pt: (i, C_KVW // KV_COLS)),
            pl.BlockSpec((n_seq * n_win * KV_SLOTS, NSA_HD), lambda i, pt: (i, 0))],
        out_specs=pl.BlockSpec((n_seq * t, NSA_HEADS * NSA_HD), lambda i, pt: (i, 0)),
    )
    return pl.pallas_call(
        functools.partial(_nsa_step_kernel, n_seq=n_seq, t=t, n_pages=n_pages, page=page, past_len=past_len,
                          n_win=n_win, n_cmp=n_cmp, n_blk=n_blk),
        grid_spec=grid_spec,
        out_shape=jax.ShapeDtypeStruct((b * t, NSA_HEADS * NSA_HD), F32),
        compiler_params=_cparams("parallel"),
        name="nsa_step",
    )(page_table_flat, *([cache2d] * (n_seq * n_pages)), proj, proj, ckv, proj, proj, win2d)


HG_ROWS = 128


HG_SEQS = 2


def _hgrn_kernel(q_ref, f_ref, i_ref, og_ref, lb_ref, nw_ref, s0_ref, o_ref, sfin_ref, st_ref, oacc_ref,
                 *, n_seq, **kw):
    for s in range(n_seq):
        one = pl.ds(s, 1)
        _hgrn_one(q_ref.at[s], f_ref.at[s], i_ref.at[s], og_ref.at[s], lb_ref, nw_ref, s0_ref.at[one],
                  o_ref.at[s], sfin_ref.at[one], st_ref.at[s], oacc_ref.at[s], **kw)


def _hgrn_one(q_ref, f_ref, i_ref, og_ref, lb_ref, nw_ref, s0_ref, o_ref, sfin_ref,
              st_ref, oacc_ref, *, rows_in, n_tblk, has_state):
    tb = pl.program_id(1)

    @pl.when(tb == 0)
    def _():
        for h in range(HG_HEADS):
            if has_state:
                st_ref[h] = s0_ref[0, h].T
            else:
                st_ref[h] = jnp.zeros((HG_DV, HG_DK), F32)

    pr = HG_CHUNK if rows_in <= HG_CHUNK else HG_ROWS
    assert rows_in <= pr

    def padded(ref):
        x = ref[...]
        return x if rows_in == pr else _pad_rows(x, pr)

    def key_rows(x):
        return x if pr == HG_ROWS else _pad_rows(x, HG_ROWS)

    q, f, v = padded(q_ref), padded(f_ref), padded(i_ref)
    lb = lb_ref[...]
    row = lax.broadcasted_iota(jnp.int32, (pr, 1), 0)
    live = row < rows_in
    forget = lb + (1.0 - lb) * jax.nn.sigmoid(f)
    k = jnp.where(live, (1.0 - lb) * jax.nn.sigmoid(-f), 0.0)
    gl = jnp.where(live, jnp.log(forget), 0.0)
    rc = row % HG_CHUNK
    cum, suf = gl, gl
    s = 1
    while s < HG_CHUNK:
        cum = cum + jnp.where(rc >= s, pltpu.roll(cum, s, axis=0), 0.0)
        suf = suf + jnp.where(rc < HG_CHUNK - s, pltpu.roll(suf, pr - s, axis=0), 0.0)
        s *= 2
    ki = k * jnp.exp(-cum)
    qd = {HG_CHUNK: q * jnp.exp(cum)}
    ke = {HG_CHUNK: k * jnp.exp(suf - gl)}
    ci = lax.broadcasted_iota(jnp.int32, (pr, HG_ROWS), 0)
    cj = lax.broadcasted_iota(jnp.int32, (pr, HG_ROWS), 1)
    masks = {HG_CHUNK: (ci // HG_CHUNK == cj // HG_CHUNK) & (ci >= cj)}
    w = HG_CHUNK
    while w < pr:
        tot = cum + suf - gl
        odd = (row // w) % 2 == 1
        cum = cum + jnp.where(odd, pltpu.roll(tot, w, axis=0), 0.0)
        suf = suf + jnp.where(odd, 0.0, pltpu.roll(tot, pr - w, axis=0))
        masks[2 * w] = ((ci // w) % 2 == 1) & (cj // w == ci // w - 1)
        w *= 2
        qd[w] = q * jnp.exp(cum)
        ke[w] = k * jnp.exp(suf - gl)
    decay = jnp.exp((cum + suf - gl)[0:1, :])
    for h in range(HG_HEADS):
        hs = slice(h * HG_DK, (h + 1) * HG_DK)
        att = jnp.where(masks[HG_CHUNK], _mm_nt(qd[HG_CHUNK][:, hs], key_rows(ki[:, hs])), 0.0)
        w = HG_CHUNK
        while w < pr:
            att = att + jnp.where(masks[2 * w], _mm_nt(qd[w][:, hs], key_rows(ke[w][:, hs])), 0.0)
            w *= 2
        v_h = key_rows(v[:, hs])
        st = st_ref[h]
        oacc_ref[0:pr, hs] = _mm(att, v_h) + _mm_nt(qd[pr][:, hs], st)
        st_ref[h] = decay[:, hs] * st + _mm(v_h.T, key_rows(ke[pr][:, hs]))
    og = og_ref[...]
    nw = nw_ref[...]
    for h in range(HG_HEADS):
        hs = slice(h * HG_DV, (h + 1) * HG_DV)
        x = oacc_ref[0:rows_in, hs]
        ms = jnp.mean(x * x, axis=-1, keepdims=True)
        y = x * lax.rsqrt(ms + EPS) * nw
        o_ref[:, hs] = (y * jax.nn.silu(og[:, hs])).astype(o_ref.dtype)

    @pl.when(tb == n_tblk - 1)
    def _():
        for h in range(HG_HEADS):
            sfin_ref[0, h] = st_ref[h].T


def _hgrn(proj, lb, norm_w, s0, b, t):
    rows_in = min(t, HG_ROWS)
    n_tblk = t // rows_in
    assert rows_in * n_tblk == t and rows_in % 8 == 0
    has_state = s0 is not None
    n_seq = HG_SEQS if b % HG_SEQS == 0 else 1
    if s0 is None:
        s0 = jnp.zeros((n_seq, HG_HEADS, HG_DK, HG_DV), F32)
    width = HG_HEADS * HG_DK
    proj3 = proj.reshape(b, t, proj.shape[1])
    col = lambda c: pl.BlockSpec((n_seq, rows_in, width), lambda i, j: (i, j, c // width))
    state = (n_seq, HG_HEADS, HG_DK, HG_DV)
    o, s_fin = pl.pallas_call(
        functools.partial(_hgrn_kernel, n_seq=n_seq, rows_in=rows_in, n_tblk=n_tblk, has_state=has_state),
        grid=(b // n_seq, n_tblk),
        in_specs=[col(C_QB), col(C_FB), col(C_IB), col(C_OG),
                  pl.BlockSpec((1, width), lambda i, j: (0, 0)),
                  pl.BlockSpec((1, HG_DV), lambda i, j: (0, 0)),
                  pl.BlockSpec(state, (lambda i, j: (i, 0, 0, 0)) if has_state else (lambda i, j: (0, 0, 0, 0)))],
        out_specs=[pl.BlockSpec((n_seq, rows_in, width), lambda i, j: (i, j, 0)),
                   pl.BlockSpec(state, lambda i, j: (i, 0, 0, 0))],
        out_shape=[jax.ShapeDtypeStruct((b, t, width), BF16),
                   jax.ShapeDtypeStruct((b, HG_HEADS, HG_DK, HG_DV), F32)],
        scratch_shapes=[pltpu.VMEM((n_seq, HG_HEADS, HG_DV, HG_DK), F32), pltpu.VMEM((n_seq, HG_ROWS, width), F32)],
        compiler_params=_cparams("parallel", "arbitrary"),
        name="hgrn2",
    )(proj3, proj3, proj3, proj3, lb.reshape(1, width), norm_w.reshape(1, HG_DV), s0)
    return o.reshape(b * t, width), s_fin


def _decoder_layer(x, past, lb, params):
    (w_in, w_a, w_b, w_o, pe, w1, w2, hg_norm, ln1, ln2, ln3, ln4, w_up, cw, cb, w_dn) = params
    b, t, d = x.shape
    m = b * t
    x2 = x.reshape(m, d)
    tm = min(MM_ROWS, m)
    proj = _in_projection(x2, ln1, *w_in, tm)
    kv_shape = (b, t, 2, NSA_GROUPS, NSA_HD)
    kvc, kvs, kvw = (a.reshape(kv_shape) for a in _kv_relayout(proj, min(CONV_ROWS, m)))
    if past is None:
        ckv = _compress_seq(proj, b, t, pe, w1, w2)
        o_a = _nsa_seq(proj, ckv, b, t)
        new_win = kvw[:, t - min(NSA_WINDOW, t):]
        s0, conv_buf = None, None
    else:
        cache_c, cache_s, page_table_flat, n_pages, page, win_buf, s0, conv_buf = past
        n_win = win_buf.shape[1]
        ckv = _compress_paged(cache_c, page_table_flat, b, n_pages, page, pe, w1, w2)
        o_a = _nsa_step(proj, ckv, cache_s, page_table_flat, win_buf.reshape(-1, NSA_HD), b, t, n_pages, page, n_win)
        new_win = jnp.concatenate([win_buf, kvw], axis=1)[:, t:]
    o_hg, s_fin = _hgrn(proj, lb, hg_norm, s0, b, t)
    mixed = _gated_merge(o_a.astype(BF16), o_hg, w_a, w_b, proj, tm, 512)
    x1 = _matmul_norm_res(mixed, w_o, x2, ln2, min(OUT_ROWS, m), w_o.shape[0])
    u = _norm_matmul(x1, ln3, w_up, tm, UP_TN)
    if past is None:
        y = _conv_glu_down(u, cw, cb, w_dn, x1, ln4, t, min(CONV_ROWS, t), 512)
        new_conv = u.reshape(b, t, 2 * D_FF)[:, t - (CONV_W - 1):]
    else:
        act = _conv_glu_step(u.reshape(b, t, 2 * D_FF), conv_buf, cw, cb, min(64, b), 512).reshape(m, D_FF)
        new_conv = jnp.concatenate([conv_buf, u.reshape(b, t, 2 * D_FF)], axis=1)[:, t:] if t < CONV_W - 1 else \
            u.reshape(b, t, 2 * D_FF)[:, t - (CONV_W - 1):]
        y = _matmul_norm_res(act, w_dn, x1, ln4, tm, 512)
    return y.reshape(b, t, d), (kvc, kvs, new_win, s_fin, new_conv)


def _split_w_in(w):
    o_rest = NSA_HEADS * NSA_HD + 3 * KV_COLS + 3 * NSA_HEADS
    edge = jnp.pad(w[:, :o_rest].astype(BF16), ((0, 0), (0, EDGE_COLS - o_rest)))
    return edge, w[:, o_rest:].astype(BF16)


def kernel(x_prompt, x_sample, cache_cmp_kv, cache_sel_kv, state_win_kv, state_hgrn, state_conv, page_table,
           w_in, w_branch_a, w_branch_b, w_out, cmp_pe, cmp_w1, cmp_w2, hg_lb_raw, hg_norm_w, ln_mix_pre,
           ln_mix_post, ln_ffn_pre, ln_ffn_post, w_up, conv_w, conv_b, w_down):
    depth = w_in.shape[0]
    dec_b, n_pages = page_table.shape
    page = cache_cmp_kv.shape[2]
    lb_all = jnp.cumsum(jax.nn.softmax(hg_lb_raw.astype(F32), axis=0), axis=0)
    pt_flat = page_table.reshape(-1).astype(jnp.int32)
    y_p, y_s = x_prompt, x_sample
    new_p, new_s = [], []
    for l in range(depth):
        params = (_split_w_in(w_in[l]), w_branch_a[l].astype(BF16), w_branch_b[l].astype(BF16),
                  w_out[l].astype(BF16), cmp_pe[l].transpose(1, 0, 2)[:, :, None, :],
                  cmp_w1[l].astype(BF16).transpose(1, 0, 2, 3).reshape(
                      2, NSA_BLOCK // CMP_STACK, CMP_STACK * NSA_HD, NSA_HD),
                  cmp_w2[l].astype(BF16), hg_norm_w[l], ln_mix_pre[l], ln_mix_post[l], ln_ffn_pre[l],
                  ln_ffn_post[l], w_up[l].astype(BF16), conv_w[l], conv_b[l], w_down[l].astype(BF16))
        y_p, st_p = _decoder_layer(y_p, None, lb_all[l], params)
        past = (cache_cmp_kv[l].reshape(-1, NSA_HD), cache_sel_kv[l].reshape(-1, NSA_HD),
                pt_flat, n_pages, page, state_win_kv[l], state_hgrn[l], state_conv[l])
        y_s, st_s = _decoder_layer(y_s, past, lb_all[l], params)
        new_p.append(st_p)
        new_s.append(st_s)

    def stack(group, i):
        return jnp.stack([st[i] for st in group], axis=0)

    return (y_p, y_s, stack(new_p, 0), stack(new_s, 0), stack(new_p, 1), stack(new_s, 1), stack(new_p, 2),
            stack(new_s, 2), stack(new_p, 3), stack(new_s, 3), stack(new_p, 4), stack(new_s, 4))
```

```python
import functools

import jax
import jax.numpy as jnp
from jax import lax
from jax.experimental import pallas as pl
from jax.experimental.pallas import tpu as pltpu

F32 = jnp.float32
BF16 = jnp.bfloat16

D_MODEL = 2048
NSA_HEADS = 8
NSA_GROUPS = 2
NSA_HPG = NSA_HEADS // NSA_GROUPS
NSA_HD = 128
NSA_BLOCK = 64
NSA_TOPN = 8
NSA_WINDOW = 512
NSA_SCALE = NSA_HD ** -0.5
FORCED_SCORE = NSA_HPG + 1.0
HG_HEADS = 8
HG_DK = 128
HG_DV = 128
HG_CHUNK = 16
D_FF = 5632
CONV_W = 3
EPS = 1e-6
KV_COLS = 2 * NSA_GROUPS * NSA_HD

C_QA = 0
C_QB = 1024
C_FB = 2048
C_IB = 3072
C_OG = 4096
C_GA = 5120
C_GB = 7168
C_KVC = 9216
C_KVS = 9728
C_KVW = 10240
C_GN = 10752

LANES = 128
MM_ROWS = 1024
UP_TN = 1024
OUT_ROWS = 512
LOG2E = 1.4426950408889634
CONV_ROWS = 512
DOWN_PARTS = 8
NEG_BIG = -1e30
VMEM_LIMIT = 56 * 1024 * 1024


def _cparams(*sem):
    return pltpu.CompilerParams(dimension_semantics=sem, vmem_limit_bytes=VMEM_LIMIT)


def _mm(a, b):
    return jnp.dot(a.astype(BF16), b.astype(BF16), preferred_element_type=F32)


def _mm_nt(a, b):
    return lax.dot_general(a.astype(BF16), b.astype(BF16), (((1,), (1,)), ((), ())),
                           preferred_element_type=F32)


def _norm_mm_kernel(x_ref, ln_ref, w_ref, o_ref, h_ref):
    @pl.when(pl.program_id(1) == 0)
    def _():
        x = x_ref[...]
        ms = jnp.mean(x * x, axis=-1, keepdims=True)
        h_ref[...] = (x * lax.rsqrt(ms + EPS) * ln_ref[...]).astype(BF16)

    o_ref[...] = _mm(h_ref[...], w_ref[...])


def _norm_matmul(x, ln, w, tm, tn):
    m, k = x.shape
    n = w.shape[1]
    return pl.pallas_call(
        _norm_mm_kernel,
        grid=(m // tm, n // tn),
        in_specs=[pl.BlockSpec((tm, k), lambda i, j: (i, 0)),
                  pl.BlockSpec((1, k), lambda i, j: (0, 0)),
                  pl.BlockSpec((k, tn), lambda i, j: (0, j))],
        out_specs=pl.BlockSpec((tm, tn), lambda i, j: (i, j)),
        out_shape=jax.ShapeDtypeStruct((m, n), F32),
        scratch_shapes=[pltpu.VMEM((tm, k), BF16)],
        compiler_params=_cparams("parallel", "arbitrary"),
        name="norm_matmul",
    )(x, ln.reshape(1, k), w)


PROJ_TN = 1024
TILE_REST0 = C_QB // PROJ_TN
TILE_KV0 = C_KVC // PROJ_TN
N_PROJ = C_KVC + 2 * PROJ_TN
EDGE_COLS = N_PROJ - (C_KVC - C_QB)


def _in_proj_kernel(x_ref, ln_ref, we_ref, wr_ref, o_ref, h_ref):
    j = pl.program_id(1)

    @pl.when(j == 0)
    def _():
        x = x_ref[...]
        ms = jnp.mean(x * x, axis=-1, keepdims=True)
        h_ref[...] = (x * lax.rsqrt(ms + EPS) * ln_ref[...]).astype(BF16)

    from_rest = (j >= TILE_REST0) & (j < TILE_KV0)

    @pl.when(from_rest)
    def _():
        o_ref[...] = _mm(h_ref[...], wr_ref[...])

    @pl.when(jnp.logical_not(from_rest))
    def _():
        o_ref[...] = _mm(h_ref[...], we_ref[...])


def _in_projection(x, ln, w_edge, w_rest, tm):
    m, k = x.shape
    tn = PROJ_TN
    n_rest = w_rest.shape[1] // tn
    assert n_rest == TILE_KV0 - TILE_REST0 and w_edge.shape[1] == EDGE_COLS and TILE_REST0 == 1
    edge_tile = lambda j: jnp.clip(j - (TILE_KV0 - TILE_REST0), 0, EDGE_COLS // tn - 1)
    rest_tile = lambda j: jnp.clip(j - TILE_REST0, 0, n_rest - 1)
    return pl.pallas_call(
        _in_proj_kernel,
        grid=(m // tm, N_PROJ // tn),
        in_specs=[pl.BlockSpec((tm, k), lambda i, j: (i, 0)),
                  pl.BlockSpec((1, k), lambda i, j: (0, 0)),
                  pl.BlockSpec((k, tn), lambda i, j: (0, edge_tile(j))),
                  pl.BlockSpec((k, tn), lambda i, j: (0, rest_tile(j)))],
        out_specs=pl.BlockSpec((tm, tn), lambda i, j: (i, j)),
        out_shape=jax.ShapeDtypeStruct((m, N_PROJ), F32),
        scratch_shapes=[pltpu.VMEM((tm, k), BF16)],
        compiler_params=_cparams("parallel", "arbitrary"),
        name="in_projection",
    )(x, ln.reshape(1, k), w_edge, w_rest)


def _kv_relayout_kernel(c_ref, s_ref, w_ref, oc_ref, os_ref, ow_ref, *, tm):
    for x_ref, o_ref in ((c_ref, oc_ref), (s_ref, os_ref), (w_ref, ow_ref)):
        for slot in range(2 * NSA_GROUPS):
            o_ref[pl.ds(slot, tm, stride=2 * NSA_GROUPS), :] = x_ref[:, slot * NSA_HD:(slot + 1) * NSA_HD]


def _kv_relayout(proj, tm):
    m = proj.shape[0]
    slots = 2 * NSA_GROUPS
    out = jax.ShapeDtypeStruct((m * slots, NSA_HD), F32)
    return pl.pallas_call(
        functools.partial(_kv_relayout_kernel, tm=tm),
        grid=(m // tm,),
        in_specs=[pl.BlockSpec((tm, KV_COLS), lambda i, c=c: (i, c // KV_COLS)) for c in (C_KVC, C_KVS, C_KVW)],
        out_specs=[pl.BlockSpec((tm * slots, NSA_HD), lambda i: (i, 0))] * 3,
        out_shape=[out] * 3,
        compiler_params=_cparams("parallel"),
        name="kv_relayout",
    )(proj, proj, proj)


def _mm_norm_res_kernel(a_ref, w_ref, res_ref, ln_ref, o_ref, acc_ref, *, nk):
    k = pl.program_id(1)

    @pl.when(k == 0)
    def _():
        acc_ref[...] = jnp.zeros_like(acc_ref)

    acc_ref[...] += _mm(a_ref[...], w_ref[...])

    @pl.when(k == nk - 1)
    def _():
        y = acc_ref[...]
        ms = jnp.mean(y * y, axis=-1, keepdims=True)
        o_ref[...] = res_ref[...] + y * lax.rsqrt(ms + EPS) * ln_ref[...]


def _matmul_norm_res(a, w, res, ln, tm, tk):
    m, kk = a.shape
    n = w.shape[1]
    nk = kk // tk
    return pl.pallas_call(
        functools.partial(_mm_norm_res_kernel, nk=nk),
        grid=(m // tm, nk),
        in_specs=[pl.BlockSpec((tm, tk), lambda i, k: (i, k)),
                  pl.BlockSpec((tk, n), lambda i, k: (k, 0)),
                  pl.BlockSpec((tm, n), lambda i, k: (i, 0)),
                  pl.BlockSpec((1, n), lambda i, k: (0, 0))],
        out_specs=pl.BlockSpec((tm, n), lambda i, k: (i, 0)),
        out_shape=jax.ShapeDtypeStruct((m, n), F32),
        scratch_shapes=[pltpu.VMEM((tm, n), F32)],
        compiler_params=_cparams("parallel", "arbitrary"),
        name="matmul_norm_res",
    )(a, w, res, ln.reshape(1, n))


def _merge_kernel(oa_ref, ohg_ref, wa_ref, wb_ref, ga_ref, gb_ref, o_ref):
    a = _mm(oa_ref[...], wa_ref[...])
    b = _mm(ohg_ref[...], wb_ref[...])
    o_ref[...] = (jax.nn.sigmoid(ga_ref[...]) * a + jax.nn.sigmoid(gb_ref[...]) * b).astype(BF16)


def _gated_merge(o_a, o_hg, w_a, w_b, proj, tm, tn):
    m, k = o_a.shape
    n = w_a.shape[1]
    return pl.pallas_call(
        _merge_kernel,
        grid=(m // tm, n // tn),
        in_specs=[pl.BlockSpec((tm, k), lambda i, j: (i, 0)),
                  pl.BlockSpec((tm, k), lambda i, j: (i, 0)),
                  pl.BlockSpec((k, tn), lambda i, j: (0, j)),
                  pl.BlockSpec((k, tn), lambda i, j: (0, j)),
                  pl.BlockSpec((tm, tn), lambda i, j: (i, C_GA // tn + j)),
                  pl.BlockSpec((tm, tn), lambda i, j: (i, C_GB // tn + j))],
        out_specs=pl.BlockSpec((tm, tn), lambda i, j: (i, j)),
        out_shape=jax.ShapeDtypeStruct((m, n), BF16),
        compiler_params=_cparams("parallel", "arbitrary"),
        name="gated_merge",
    )(o_a, o_hg, w_a, w_b, proj, proj)


def _conv_taps(fa, fg, cwa_ref, cwg_ref, cba_ref, cbg_ref, tap):
    ca = cba_ref[...]
    cg = cbg_ref[...]
    for j in range(CONV_W):
        ca = ca + tap(fa, j) * cwa_ref[j:j + 1, :]
        cg = cg + tap(fg, j) * cwg_ref[j:j + 1, :]
    return jax.nn.gelu(cg, approximate=True) * ca


def _conv_glu_down_kernel(ua_ref, ug_ref, ha_ref, hg_ref, cwa_ref, cwg_ref, cba_ref, cbg_ref, w_ref, res_ref,
                          ln_ref, o_ref, fa_ref, fg_ref, act_ref, acc_ref, *, tm, tiles_per_seq, nk):
    k = pl.program_id(1)
    slot = k % 2

    @pl.when(k == 0)
    def _():
        acc_ref[...] = jnp.zeros_like(acc_ref)
        act_ref[1] = jnp.zeros(act_ref.shape[1:], act_ref.dtype)

    first = (pl.program_id(0) % tiles_per_seq) == 0
    fa_ref[0:8, :] = jnp.where(first, 0.0, ha_ref[...])
    fg_ref[0:8, :] = jnp.where(first, 0.0, hg_ref[...])
    fa_ref[8:, :] = ua_ref[...]
    fg_ref[8:, :] = ug_ref[...]
    n_parts = DOWN_PARTS
    cols = acc_ref.shape[1] // n_parts
    rows = tm // n_parts
    for part in range(n_parts):
        cs = slice(part * cols, (part + 1) * cols)
        acc_ref[:, cs] += _mm(act_ref[1 - slot], w_ref[:, cs])
        r0 = part * rows
        xa, xg = fa_ref[r0:r0 + rows + 8, :], fg_ref[r0:r0 + rows + 8, :]
        tap = lambda x, j: (x if j == CONV_W - 1 else pltpu.roll(x, CONV_W - 1 - j, axis=0))[8:]
        act = _conv_taps(xa, xg, cwa_ref, cwg_ref, cba_ref, cbg_ref, tap)
        act_ref[slot, r0:r0 + rows, :] = act.astype(act_ref.dtype)

    @pl.when(k == nk)
    def _():
        y = acc_ref[...]
        ms = jnp.mean(y * y, axis=-1, keepdims=True)
        o_ref[...] = res_ref[...] + y * lax.rsqrt(ms + EPS) * ln_ref[...]


def _conv_glu_down(u, conv_w, conv_b, w, res, ln, seq_len, tm, tk):
    m = u.shape[0]
    n = w.shape[1]
    nk = D_FF // tk
    hb = tm // 8
    kc = lambda k: jnp.minimum(k, nk - 1)
    kw = lambda k: jnp.maximum(k - 1, 0)
    halo = lambda i, k, off: (jnp.maximum(i * hb - 1, 0), kc(k) + off)
    return pl.pallas_call(
        functools.partial(_conv_glu_down_kernel, tm=tm, tiles_per_seq=seq_len // tm, nk=nk),
        grid=(m // tm, nk + 1),
        in_specs=[pl.BlockSpec((tm, tk), lambda i, k: (i, kc(k))),
                  pl.BlockSpec((tm, tk), lambda i, k: (i, kc(k) + nk)),
                  pl.BlockSpec((8, tk), lambda i, k: halo(i, k, 0)),
                  pl.BlockSpec((8, tk), lambda i, k: halo(i, k, nk)),
                  pl.BlockSpec((CONV_W, tk), lambda i, k: (0, kc(k))),
                  pl.BlockSpec((CONV_W, tk), lambda i, k: (0, kc(k) + nk)),
                  pl.BlockSpec((1, tk), lambda i, k: (0, kc(k))),
                  pl.BlockSpec((1, tk), lambda i, k: (0, kc(k) + nk)),
                  pl.BlockSpec((tk, n), lambda i, k: (kw(k), 0)),
                  pl.BlockSpec((tm, n), lambda i, k: (i, 0)),
                  pl.BlockSpec((1, n), lambda i, k: (0, 0))],
        out_specs=pl.BlockSpec((tm, n), lambda i, k: (i, 0)),
        out_shape=jax.ShapeDtypeStruct((m, n), F32),
        scratch_shapes=[pltpu.VMEM((tm + 8, tk), F32), pltpu.VMEM((tm + 8, tk), F32),
                        pltpu.VMEM((2, tm, tk), BF16), pltpu.VMEM((tm, n), F32)],
        compiler_params=_cparams("parallel", "arbitrary"),
        name="conv_glu_down",
    )(u, u, u, u, conv_w, conv_w, conv_b.reshape(1, -1), conv_b.reshape(1, -1), w, res, ln.reshape(1, n))


def _conv_glu_step_kernel(ua_ref, ug_ref, ba_ref, bg_ref, cwa_ref, cwg_ref, cba_ref, cbg_ref,
                          o_ref, fa_ref, fg_ref, *, t):
    fa_ref[:, 8 - (CONV_W - 1):8, :] = ba_ref[...]
    fg_ref[:, 8 - (CONV_W - 1):8, :] = bg_ref[...]
    fa_ref[:, 8:, :] = ua_ref[...]
    fg_ref[:, 8:, :] = ug_ref[...]
    tap = lambda f, j: f[:, 8 - (CONV_W - 1) + j:8 - (CONV_W - 1) + j + t, :]
    o_ref[...] = _conv_taps(fa_ref, fg_ref, cwa_ref, cwg_ref, cba_ref, cbg_ref, tap)


def _conv_glu_step(u3, buf, conv_w, conv_b, nb, tn):
    b, t, _ = u3.shape
    nj = D_FF // tn
    return pl.pallas_call(
        functools.partial(_conv_glu_step_kernel, t=t),
        grid=(b // nb, nj),
        in_specs=[pl.BlockSpec((nb, t, tn), lambda i, j: (i, 0, j)),
                  pl.BlockSpec((nb, t, tn), lambda i, j: (i, 0, j + nj)),
                  pl.BlockSpec((nb, CONV_W - 1, tn), lambda i, j: (i, 0, j)),
                  pl.BlockSpec((nb, CONV_W - 1, tn), lambda i, j: (i, 0, j + nj)),
                  pl.BlockSpec((CONV_W, tn), lambda i, j: (0, j)),
                  pl.BlockSpec((CONV_W, tn), lambda i, j: (0, j + nj)),
                  pl.BlockSpec((1, tn), lambda i, j: (0, j)),
                  pl.BlockSpec((1, tn), lambda i, j: (0, j + nj))],
        out_specs=pl.BlockSpec((nb, t, tn), lambda i, j: (i, 0, j)),
        out_shape=jax.ShapeDtypeStruct((b, t, D_FF), F32),
        scratch_shapes=[pltpu.VMEM((nb, 8 + t, tn), F32), pltpu.VMEM((nb, 8 + t, tn), F32)],
        compiler_params=_cparams("parallel", "arbitrary"),
        name="conv_glu_step",
    )(u3, u3, buf, buf, conv_w, conv_w, conv_b.reshape(1, -1), conv_b.reshape(1, -1))


def _compress_body(load_x, pe_ref, w1_ref, w2_ref, o_ref, n_seq, n_cmp):
    for c in range(2):
        acc = jnp.zeros((NSA_GROUPS * n_seq * n_cmp, NSA_HD), F32)
        for l in range(0, NSA_BLOCK, CMP_STACK):
            parts = [jnp.concatenate([load_x(l + dl, c, g) for g in range(NSA_GROUPS)], axis=0) + pe_ref[c, l + dl]
                     for dl in range(CMP_STACK)]
            acc = acc + _mm(jnp.concatenate(parts, axis=1), w1_ref[c, l // CMP_STACK])
        out = _mm(jax.nn.silu(acc), w2_ref[c])
        for g in range(NSA_GROUPS):
            for s in range(n_seq):
                r0 = (g * n_seq + s) * n_cmp
                o_ref[s, c, g] = out[r0:r0 + n_cmp]


def _compress_seq_kernel(x0_ref, x1_ref, x2_ref, x3_ref, pe_ref, w1_ref, w2_ref, o_ref, *, n_cmp):
    xs = (x0_ref, x1_ref, x2_ref, x3_ref)
    load_x = lambda l, c, g: xs[c * NSA_GROUPS + g][pl.ds(l, n_cmp, stride=NSA_BLOCK), :]
    _compress_body(load_x, pe_ref, w1_ref, w2_ref, o_ref, 1, n_cmp)


def _compress_seq(proj, b, t, pe, w1, w2):
    n_cmp = t // NSA_BLOCK
    rows = n_cmp * NSA_BLOCK
    assert rows == t
    return pl.pallas_call(
        functools.partial(_compress_seq_kernel, n_cmp=n_cmp),
        grid=(b,),
        in_specs=[pl.BlockSpec((rows, NSA_HD), lambda i, cg=cg: (i, C_KVC // NSA_HD + cg))
                  for cg in range(2 * NSA_GROUPS)] + [
                  pl.BlockSpec(pe.shape, lambda i: (0, 0, 0, 0)),
                  pl.BlockSpec(w1.shape, lambda i: (0, 0, 0, 0)),
                  pl.BlockSpec(w2.shape, lambda i: (0, 0, 0))],
        out_specs=pl.BlockSpec((1, 2, NSA_GROUPS, n_cmp, NSA_HD), lambda i: (i, 0, 0, 0, 0)),
        out_shape=jax.ShapeDtypeStruct((b, 2, NSA_GROUPS, n_cmp, NSA_HD), F32),
        compiler_params=_cparams("parallel"),
        name="compress_seq",
    )(proj, proj, proj, proj, pe, w1, w2)


KV_SLOTS = 2 * NSA_GROUPS
CMP_SEQS = 2
CMP_STACK = 2
CMP_PITCH = NSA_BLOCK * KV_SLOTS + 8


def _compress_paged_kernel(pt_ref, *refs, n_seq, n_pages, page_rows):
    del pt_ref
    pages = refs[:n_seq * n_pages]
    pe_ref, w1_ref, w2_ref, o_ref, x_ref = refs[n_seq * n_pages:]
    blk_rows = NSA_BLOCK * KV_SLOTS
    per_page = page_rows // blk_rows
    for i, pg in enumerate(pages):
        for j in range(per_page):
            n = i * per_page + j
            x_ref[n * CMP_PITCH:n * CMP_PITCH + blk_rows, :] = pg[j * blk_rows:(j + 1) * blk_rows, :]
    n_cmp = n_pages * per_page
    load_x = lambda l, c, g: x_ref[pl.ds(l * KV_SLOTS + c * NSA_GROUPS + g, n_seq * n_cmp, stride=CMP_PITCH), :]
    _compress_body(load_x, pe_ref, w1_ref, w2_ref, o_ref, n_seq, n_cmp)


def _compress_paged(cache2d, page_table_flat, b, n_pages, page, pe, w1, w2):
    page_rows = page * KV_SLOTS
    n_cmp = n_pages * page // NSA_BLOCK
    n_seq = CMP_SEQS if b % CMP_SEQS == 0 else 1
    page_spec = lambda s, p: pl.BlockSpec(
        (page_rows, NSA_HD), lambda i, pt: (pt[(i * n_seq + s) * n_pages + p], 0))
    grid_spec = pltpu.PrefetchScalarGridSpec(
        num_scalar_prefetch=1,
        grid=(b // n_seq,),
        in_specs=[page_spec(s, p) for s in range(n_seq) for p in range(n_pages)] + [
            pl.BlockSpec(pe.shape, lambda i, pt: (0, 0, 0, 0)),
            pl.BlockSpec(w1.shape, lambda i, pt: (0, 0, 0, 0)),
            pl.BlockSpec(w2.shape, lambda i, pt: (0, 0, 0))],
        out_specs=pl.BlockSpec((n_seq, 2, NSA_GROUPS, n_cmp, NSA_HD), lambda i, pt: (i, 0, 0, 0, 0)),
        scratch_shapes=[pltpu.VMEM((n_seq * n_cmp * CMP_PITCH, NSA_HD), F32)],
    )
    return pl.pallas_call(
        functools.partial(_compress_paged_kernel, n_seq=n_seq, n_pages=n_pages, page_rows=page_rows),
        grid_spec=grid_spec,
        out_shape=jax.ShapeDtypeStruct((b, 2, NSA_GROUPS, n_cmp, NSA_HD), F32),
        compiler_params=_cparams("parallel"),
        name="compress_paged",
    )(page_table_flat, *([cache2d] * (n_seq * n_pages)), pe, w1, w2)


def _alibi_slope(g, h):
    return 2.0 ** (-8.0 * (g * NSA_HPG + h + 1.0) / NSA_HEADS)


def _row_consts(tq, g, q_pos0):
    rows = NSA_HPG * tq
    r = lax.broadcasted_iota(jnp.int32, (rows, 1), 0)
    head = r // tq
    qpos = q_pos0 + (r - head * tq)
    slope = jnp.zeros((rows, 1), F32)
    for h in range(NSA_HPG):
        slope = jnp.where(head == h, _alibi_slope(g, h), slope)
    return qpos, slope


def _stack_heads(q_ref, g):
    return jnp.concatenate(
        [q_ref[:, (g * NSA_HPG + h) * NSA_HD:(g * NSA_HPG + h + 1) * NSA_HD] for h in range(NSA_HPG)], axis=0)


def _compressed_branches(units, n_cmp):
    pad = jnp.zeros((LANES - n_cmp, NSA_HD), F32)
    n = lax.broadcasted_iota(jnp.int32, (1, LANES), 1)
    scores = []
    for qs, ck, _, qpos, slope in units:
        dist = qpos - ((n + 1) * NSA_BLOCK - 1)
        mask = (dist >= 0) & (n < n_cmp)
        s = _mm_nt(qs, jnp.concatenate([ck, pad], axis=0)) * NSA_SCALE - slope * dist.astype(F32)
        scores.append((jnp.where(mask, s, NEG_BIG), mask))
    probs = []
    for s, mask in scores:
        m = jnp.max(s, axis=-1, keepdims=True)
        e = jnp.where(mask, jnp.exp(s - m), 0.0)
        d = jnp.sum(e, axis=-1, keepdims=True)
        probs.append(e / jnp.where(d > 0, d, 1.0))
    return [(_mm(p, jnp.concatenate([cv, pad], axis=0)), p) for p, (_, _, cv, _, _) in zip(probs, units)]


def _select_blocks(imp, q_pos, n_blk):
    tq = imp.shape[0]
    nb = -(-n_blk // 8) * 8
    imp_t = (imp if tq == LANES else _pad_rows(imp, LANES)).T[0:nb]
    blk = lax.broadcasted_iota(jnp.int32, (nb, LANES), 0)
    cur = q_pos // NSA_BLOCK
    forced = (blk == 0) | (blk == cur) | (blk == cur - 1)
    valid = blk <= cur
    score = jnp.where(valid, jnp.where(forced, FORCED_SCORE, imp_t), -1.0)
    score = jnp.where(blk < n_blk, score, -2.0)
    beaten_by = jnp.zeros((nb, LANES), F32)
    for j in range(n_blk):
        sj = score[j:j + 1, :]
        beaten_by = beaten_by + jnp.where((sj > score) | ((sj == score) & (blk > j)), 1.0, 0.0)
    sel = jnp.where(beaten_by < min(NSA_TOPN, n_blk), 1.0, 0.0)
    return _pad_rows(sel, LANES).T[0:tq]


def _expand_sel(sel, key0, nkeys):
    bi = lax.broadcasted_iota(jnp.int32, (LANES, nkeys), 0)
    ki = lax.broadcasted_iota(jnp.int32, (LANES, nkeys), 1)
    expand = (bi == (key0 + ki) // NSA_BLOCK).astype(BF16)
    return _mm(sel, expand)


def _write_gated(o_ref, gate_ref, g, tq, o_cmp, o_sel, o_win):
    sig = jax.nn.sigmoid(gate_ref[...])
    for h in range(NSA_HPG):
        c0 = (g * NSA_HPG + h) * 3
        rs = slice(h * tq, (h + 1) * tq)
        o = sig[:, c0:c0 + 1] * o_cmp[rs] + sig[:, c0 + 1:c0 + 2] * o_sel[rs] + sig[:, c0 + 2:c0 + 3] * o_win[rs]
        o_ref[:, (g * NSA_HPG + h) * NSA_HD:(g * NSA_HPG + h + 1) * NSA_HD] = o.astype(o_ref.dtype)


SEQ_TK = 256


def _lane_groups(x):
    return [x[:, i:i + LANES] for i in range(0, x.shape[1], LANES)]


def _tiled_attention(heads, kv_ref, lo, hi, tile_masks, s_ref, m_ref, l_ref, acc_ref, with_tile0=False):
    tq = m_ref.shape[0] // len(heads)
    head_rows = [slice(i * tq, (i + 1) * tq) for i in range(len(heads))]
    groups = sorted({g for _, _, g in heads})
    kcol = lambda g: slice(g * NSA_HD, (g + 1) * NSA_HD)
    vcol = lambda g: slice((NSA_GROUPS + g) * NSA_HD, (NSA_GROUPS + g + 1) * NSA_HD)
    m_ref[...] = jnp.full(m_ref.shape, NEG_BIG, F32)

    def scores(kt, carry):
        key0 = kt * SEQ_TK
        masks = tile_masks(key0)
        rows = pl.ds(pl.multiple_of(key0, SEQ_TK), SEQ_TK)
        k = {g: kv_ref[rows, kcol(g)] for g in groups}
        kpos = (key0 + lax.broadcasted_iota(jnp.int32, (1, SEQ_TK), 1)).astype(F32)
        for (load_q, slope, g), rs in zip(heads, head_rows):
            valid = masks[g]
            s = jnp.where(valid, _mm_nt(load_q(), k[g]) * (NSA_SCALE * LOG2E) + (slope * LOG2E) * kpos, NEG_BIG)
            s_ref[kt, rs] = s
            m_ref[rs] = functools.reduce(jnp.maximum, [m_ref[rs]] + _lane_groups(s))
        return carry

    if with_tile0:
        scores(jnp.int32(0), 0)
    lax.fori_loop(lo, hi, scores, 0)
    for rs in head_rows:
        m = jnp.maximum(jnp.max(m_ref[rs], axis=-1, keepdims=True), 0.1 * NEG_BIG)
        m_ref[rs] = jnp.broadcast_to(m, (tq, LANES))
    l_ref[...] = jnp.zeros(l_ref.shape, F32)
    acc_ref[...] = jnp.zeros(acc_ref.shape, F32)

    def values(kt, carry):
        rows = pl.ds(pl.multiple_of(kt * SEQ_TK, SEQ_TK), SEQ_TK)
        v = {g: kv_ref[rows, vcol(g)] for g in groups}
        for (_, _, g), rs in zip(heads, head_rows):
            m = m_ref[rs]
            p = [jnp.exp2(s - m) for s in _lane_groups(s_ref[kt, rs])]
            l_ref[rs] += sum(p[1:], p[0])
            acc_ref[rs] += _mm(jnp.concatenate(p, axis=1), v[g])
        return carry

    if with_tile0:
        values(jnp.int32(0), 0)
    lax.fori_loop(lo, hi, values, 0)
    outs = []
    for rs in head_rows:
        l = jnp.sum(l_ref[rs], axis=-1, keepdims=True)
        outs.append(acc_ref[rs] / jnp.where(l > 0, l, 1.0))
    return jnp.concatenate(outs, axis=0)


def _nsa_seq_kernel(q_ref, gate_ref, ckv_ref, ks_ref, kw_ref, o_ref, s_ref, m_ref, l_ref, acc_ref,
                    *, tq, n_cmp, n_blk):
    j = pl.program_id(1)
    q0 = j * tq
    lane = lax.broadcasted_iota(jnp.int32, (1, SEQ_TK), 1)
    qpos_t = q0 + lax.broadcasted_iota(jnp.int32, (tq, 1), 0)
    hi = (q0 + tq - 1) // SEQ_TK + 1
    cmp_out = _compressed_branches(
        [(_stack_heads(q_ref, g), ckv_ref[0, 0, g], ckv_ref[0, 1, g]) + _row_consts(tq, g, q0)
         for g in range(NSA_GROUPS)], n_cmp)
    o_cmp, sel = [], []
    for o_g, p_c in cmp_out:
        imp = p_c[0:tq]
        for h in range(1, NSA_HPG):
            imp = imp + p_c[h * tq:(h + 1) * tq]
        o_cmp.append(o_g)
        sel.append(_select_blocks(imp, q0 + lax.broadcasted_iota(jnp.int32, (1, LANES), 1), n_blk))
    order = [(g, h) for h in range(NSA_HPG) for g in range(NSA_GROUPS)]
    heads = [(lambda c=(g * NSA_HPG + h) * NSA_HD: q_ref[:, c:c + NSA_HD], _alibi_slope(g, h), g) for g, h in order]
    scratch = (s_ref, m_ref, l_ref, acc_ref)

    def sel_masks(key0):
        dist = qpos_t - (key0 + lane)
        return [(dist >= 0) & (_expand_sel(sel[g], key0, SEQ_TK) > 0.5) for g in range(NSA_GROUPS)]

    picked = functools.reduce(jnp.maximum, sel)
    blk = lax.broadcasted_iota(jnp.int32, picked.shape, 1)
    first_blk = jnp.min(jnp.where((picked > 0.5) & (blk >= 1), blk.astype(F32), float(LANES)))
    first_tile = jnp.clip(first_blk.astype(jnp.int32) // (SEQ_TK // NSA_BLOCK), 1, hi)
    o_sel = _tiled_attention(heads, ks_ref, first_tile, hi, sel_masks, *scratch, with_tile0=True)

    def win_masks(key0):
        dist = qpos_t - (key0 + lane)
        return [(dist >= 0) & (dist < NSA_WINDOW)] * NSA_GROUPS

    lo = jnp.maximum(q0 - (NSA_WINDOW - 1), 0) // SEQ_TK
    o_win = _tiled_attention(heads, kw_ref, lo, hi, win_masks, *scratch)
    sig = jax.nn.sigmoid(gate_ref[...])
    for i, (g, h) in enumerate(order):
        c0 = (g * NSA_HPG + h) * 3
        rs = slice(i * tq, (i + 1) * tq)
        o = (sig[:, c0:c0 + 1] * o_cmp[g][h * tq:(h + 1) * tq] + sig[:, c0 + 1:c0 + 2] * o_sel[rs]
             + sig[:, c0 + 2:c0 + 3] * o_win[rs])
        o_ref[:, (g * NSA_HPG + h) * NSA_HD:(g * NSA_HPG + h + 1) * NSA_HD] = o.astype(o_ref.dtype)


def _nsa_seq(proj, ckv, b, t):
    tq = 128
    n_cmp = t // NSA_BLOCK
    n_blk = -(-t // NSA_BLOCK)
    nq = t // tq
    assert t % SEQ_TK == 0 and n_blk <= LANES
    return pl.pallas_call(
        functools.partial(_nsa_seq_kernel, tq=tq, n_cmp=n_cmp, n_blk=n_blk),
        scratch_shapes=[pltpu.VMEM((t // SEQ_TK, NSA_HEADS * tq, SEQ_TK), F32)]
        + [pltpu.VMEM((NSA_HEADS * tq, LANES), F32)] * 3,
        grid=(b, nq),
        in_specs=[pl.BlockSpec((tq, NSA_HEADS * NSA_HD), lambda i, j: (i * nq + j, C_QA // 1024)),
                  pl.BlockSpec((tq, LANES), lambda i, j: (i * nq + j, C_GN // LANES)),
                  pl.BlockSpec((1, 2, NSA_GROUPS, n_cmp, NSA_HD), lambda i, j: (i, 0, 0, 0, 0)),
                  pl.BlockSpec((t, KV_COLS), lambda i, j: (i, C_KVS // KV_COLS)),
                  pl.BlockSpec((t, KV_COLS), lambda i, j: (i, C_KVW // KV_COLS))],
        out_specs=pl.BlockSpec((tq, NSA_HEADS * NSA_HD), lambda i, j: (i * nq + j, 0)),
        out_shape=jax.ShapeDtypeStruct((b * t, NSA_HEADS * NSA_HD), BF16),
        compiler_params=_cparams("parallel", "arbitrary"),
        name="nsa_seq",
    )(proj, proj, ckv, proj, proj)


def _pad_rows(x, rows):
    if x.shape[0] == rows:
        return x
    return jnp.concatenate([x, jnp.zeros((rows - x.shape[0], x.shape[1]), x.dtype)], axis=0)


def _two_pass_attention(problems):
    n_tiles = max(len(tiles) for _, tiles in problems)
    scores = [[] for _ in problems]
    for i in range(n_tiles):
        for u, (qs, tiles) in enumerate(problems):
            if i < len(tiles):
                k_fn, _, mask_fn = tiles[i]
                valid, bias = mask_fn()
                scores[u].append(jnp.where(valid, _mm_nt(qs, k_fn()) * NSA_SCALE - bias, NEG_BIG))
    ms = [jnp.max(functools.reduce(jnp.maximum, sc), axis=-1, keepdims=True) for sc in scores]
    accs = [jnp.zeros((qs.shape[0], NSA_HD), F32) for qs, _ in problems]
    lsums = [jnp.zeros(sc[0].shape, F32) for sc in scores]
    for i in range(n_tiles):
        for u, (_, tiles) in enumerate(problems):
            if i < len(tiles):
                s = scores[u][i]
                p = jnp.where(s > 0.5 * NEG_BIG, jnp.exp(s - ms[u]), 0.0)
                lsums[u] = lsums[u] + p
                accs[u] = accs[u] + _mm(p, tiles[i][1]())
    outs = []
    for acc, lsum in zip(accs, lsums):
        l = jnp.sum(lsum, axis=-1, keepdims=True)
        outs.append(acc / jnp.where(l > 0, l, 1.0))
    return outs


STEP_SEQS = 4


def _nsa_step_kernel(pt_ref, *refs, n_seq, t, n_pages, page, past_len, n_win, n_cmp, n_blk):
    del pt_ref
    q_ref, gate_ref, ckv_ref, ksn_ref, kwn_ref, win_ref, o_ref = refs[n_seq * n_pages:]
    lane = lax.broadcasted_iota(jnp.int32, (1, page), 1)
    blocks_per_page = page // NSA_BLOCK
    win_rows = n_win * KV_SLOTS
    units = [(s, g) for s in range(n_seq) for g in range(NSA_GROUPS)]
    seq_rows = lambda s: pl.ds(s * t, t)

    cmp_in = [(_stack_heads(q_ref.at[seq_rows(s)], g), ckv_ref[s, 0, g], ckv_ref[s, 1, g])
              + _row_consts(t, g, past_len) for s, g in units]
    prep = []
    for (qs, _, _, qpos, slope), (o_cmp, p_c) in zip(cmp_in, _compressed_branches(cmp_in, n_cmp)):
        imp = p_c[0:t]
        for h in range(1, NSA_HPG):
            imp = imp + p_c[h * t:(h + 1) * t]
        prep.append((qs, qpos, slope, o_cmp, imp))
    q_pos = past_len + lax.broadcasted_iota(jnp.int32, (1, LANES), 1) % t
    sel_all = _select_blocks(jnp.concatenate([p[4] for p in prep], axis=0), q_pos, n_blk)
    prep = [p[:4] + (sel_all[u * t:(u + 1) * t],) for u, p in enumerate(prep)]

    def new_rows(ref, s, g, value):
        col = ((NSA_GROUPS if value else 0) + g) * NSA_HD
        return lambda: _pad_rows(ref[seq_rows(s), col:col + NSA_HD], page)

    def strided(ref, row0, g, value):
        slot = (NSA_GROUPS if value else 0) + g
        return lambda: ref[pl.ds(row0 + slot, page, stride=KV_SLOTS), :]

    sel_problems, win_problems = [], []
    for (s, g), (qs, qpos, slope, _, sel) in zip(units, prep):
        tiles = []
        for p in range(n_pages + 1):
            def sel_mask(p=p, qpos=qpos, slope=slope, sel=sel):
                chosen = jnp.zeros((t, page), F32)
                for bi in range(blocks_per_page):
                    blk = p * blocks_per_page + bi
                    chosen = jnp.where(lane // NSA_BLOCK == bi, sel[:, blk:blk + 1], chosen)
                chosen = jnp.concatenate([chosen] * NSA_HPG, axis=0)
                dist = qpos - (p * page + lane)
                return (dist >= 0) & (chosen > 0.5), slope * dist.astype(F32)

            if p < n_pages:
                pg = refs[s * n_pages + p]
                tiles.append((strided(pg, 0, g, False), strided(pg, 0, g, True), sel_mask))
            else:
                tiles.append((new_rows(ksn_ref, s, g, False), new_rows(ksn_ref, s, g, True), sel_mask))
        sel_problems.append((qs, tiles))

        tiles = []
        for p in range(n_win // page + 1):
            def win_mask(p=p, qpos=qpos, slope=slope):
                kpos = past_len - n_win + p * page + lane
                dist = qpos - kpos
                return (dist >= 0) & (dist < NSA_WINDOW) & (kpos >= 0), slope * dist.astype(F32)

            if p < n_win // page:
                row0 = s * win_rows + p * page * KV_SLOTS
                tiles.append((strided(win_ref, row0, g, False), strided(win_ref, row0, g, True), win_mask))
            else:
                tiles.append((new_rows(kwn_ref, s, g, False), new_rows(kwn_ref, s, g, True), win_mask))
        win_problems.append((qs, tiles))

    o_sel = _two_pass_attention(sel_problems)
    o_win = _two_pass_attention(win_problems)
    for u, (s, g) in enumerate(units):
        _write_gated(o_ref.at[seq_rows(s)], gate_ref.at[seq_rows(s)], g, t, prep[u][3], o_sel[u], o_win[u])


def _nsa_step(proj, ckv, cache2d, page_table_flat, win2d, b, t, n_pages, page, n_win):
    past_len = n_pages * page
    assert n_win % page == 0 and t <= page
    n_cmp = (past_len + t) // NSA_BLOCK
    n_blk = -(-(past_len + t) // NSA_BLOCK)
    assert n_cmp * NSA_BLOCK == past_len and n_blk <= LANES
    n_seq = STEP_SEQS if b % STEP_SEQS == 0 else 1
    page_spec = lambda s, p: pl.BlockSpec(
        (page * KV_SLOTS, NSA_HD), lambda i, pt: (pt[(i * n_seq + s) * n_pages + p], 0))
    grid_spec = pltpu.PrefetchScalarGridSpec(
        num_scalar_prefetch=1,
        grid=(b // n_seq,),
        in_specs=[page_spec(s, p) for s in range(n_seq) for p in range(n_pages)] + [
            pl.BlockSpec((n_seq * t, NSA_HEADS * NSA_HD), lambda i, pt: (i, C_QA // 1024)),
            pl.BlockSpec((n_seq * t, LANES), lambda i, pt: (i, C_GN // LANES)),
            pl.BlockSpec((n_seq, 2, NSA_GROUPS, n_cmp, NSA_HD), lambda i, pt: (i, 0, 0, 0, 0)),
            pl.BlockSpec((n_seq * t, KV_COLS), lambda i, pt: (i, C_KVS // KV_COLS)),
            pl.BlockSpec((n_seq * t, KV_COLS), lambda i, pt: (i, C_KVW // KV_COLS)),
            pl.BlockSpec((n_seq * n_win * KV_SLOTS, NSA_HD), lambda i, pt: (i, 0))],
        out_specs=pl.BlockSpec((n_seq * t, NSA_HEADS * NSA_HD), lambda i, pt: (i, 0)),
    )
    return pl.pallas_call(
        functools.partial(_nsa_step_kernel, n_seq=n_seq, t=t, n_pages=n_pages, page=page, past_len=past_len,
                          n_win=n_win, n_cmp=n_cmp, n_blk=n_blk),
        grid_spec=grid_spec,
        out_shape=jax.ShapeDtypeStruct((b * t, NSA_HEADS * NSA_HD), F32),
        compiler_params=_cparams("parallel"),
        name="nsa_step",
    )(page_table_flat, *([cache2d] * (n_seq * n_pages)), proj, proj, ckv, proj, proj, win2d)


HG_ROWS = 128


HG_SEQS = 2


def _hgrn_kernel(q_ref, f_ref, i_ref, og_ref, lb_ref, nw_ref, s0_ref, o_ref, sfin_ref, st_ref, oacc_ref,
                 *, n_seq, **kw):
    for s in range(n_seq):
        one = pl.ds(s, 1)
        _hgrn_one(q_ref.at[s], f_ref.at[s], i_ref.at[s], og_ref.at[s], lb_ref, nw_ref, s0_ref.at[one],
                  o_ref.at[s], sfin_ref.at[one], st_ref.at[s], oacc_ref.at[s], **kw)


def _hgrn_one(q_ref, f_ref, i_ref, og_ref, lb_ref, nw_ref, s0_ref, o_ref, sfin_ref,
              st_ref, oacc_ref, *, rows_in, n_tblk, has_state):
    tb = pl.program_id(1)

    @pl.when(tb == 0)
    def _():
        for h in range(HG_HEADS):
            if has_state:
                st_ref[h] = s0_ref[0, h].T
            else:
                st_ref[h] = jnp.zeros((HG_DV, HG_DK), F32)

    pr = HG_CHUNK if rows_in <= HG_CHUNK else HG_ROWS
    assert rows_in <= pr

    def padded(ref):
        x = ref[...]
        return x if rows_in == pr else _pad_rows(x, pr)

    def key_rows(x):
        return x if pr == HG_ROWS else _pad_rows(x, HG_ROWS)

    q, f, v = padded(q_ref), padded(f_ref), padded(i_ref)
    lb = lb_ref[...]
    row = lax.broadcasted_iota(jnp.int32, (pr, 1), 0)
    live = row < rows_in
    forget = lb + (1.0 - lb) * jax.nn.sigmoid(f)
    k = jnp.where(live, (1.0 - lb) * jax.nn.sigmoid(-f), 0.0)
    gl = jnp.where(live, jnp.log(forget), 0.0)
    rc = row % HG_CHUNK
    cum, suf = gl, gl
    s = 1
    while s < HG_CHUNK:
        cum = cum + jnp.where(rc >= s, pltpu.roll(cum, s, axis=0), 0.0)
        suf = suf + jnp.where(rc < HG_CHUNK - s, pltpu.roll(suf, pr - s, axis=0), 0.0)
        s *= 2
    ki = k * jnp.exp(-cum)
    qd = {HG_CHUNK: q * jnp.exp(cum)}
    ke = {HG_CHUNK: k * jnp.exp(suf - gl)}
    ci = lax.broadcasted_iota(jnp.int32, (pr, HG_ROWS), 0)
    cj = lax.broadcasted_iota(jnp.int32, (pr, HG_ROWS), 1)
    masks = {HG_CHUNK: (ci // HG_CHUNK == cj // HG_CHUNK) & (ci >= cj)}
    w = HG_CHUNK
    while w < pr:
        tot = cum + suf - gl
        odd = (row // w) % 2 == 1
        cum = cum + jnp.where(odd, pltpu.roll(tot, w, axis=0), 0.0)
        suf = suf + jnp.where(odd, 0.0, pltpu.roll(tot, pr - w, axis=0))
        masks[2 * w] = ((ci // w) % 2 == 1) & (cj // w == ci // w - 1)
        w *= 2
        qd[w] = q * jnp.exp(cum)
        ke[w] = k * jnp.exp(suf - gl)
    decay = jnp.exp((cum + suf - gl)[0:1, :])
    head_cols = [slice(h * HG_DK, (h + 1) * HG_DK) for h in range(HG_HEADS)]
    atts = []
    for hs in head_cols:
        att = jnp.where(masks[HG_CHUNK], _mm_nt(qd[HG_CHUNK][:, hs], key_rows(ki[:, hs])), 0.0)
        w = HG_CHUNK
        while w < pr:
            att = att + jnp.where(masks[2 * w], _mm_nt(qd[w][:, hs], key_rows(ke[w][:, hs])), 0.0)
            w *= 2
        atts.append(att)
    for h, hs in enumerate(head_cols):
        oacc_ref[0:pr, hs] = _mm(atts[h], key_rows(v[:, hs])) + _mm_nt(qd[pr][:, hs], st_ref[h])
    for h, hs in enumerate(head_cols):
        st_ref[h] = decay[:, hs] * st_ref[h] + _mm(key_rows(v[:, hs]).T, key_rows(ke[pr][:, hs]))
    og = og_ref[...]
    nw = nw_ref[...]
    for h in range(HG_HEADS):
        hs = slice(h * HG_DV, (h + 1) * HG_DV)
        x = oacc_ref[0:rows_in, hs]
        ms = jnp.mean(x * x, axis=-1, keepdims=True)
        y = x * lax.rsqrt(ms + EPS) * nw
        o_ref[:, hs] = (y * jax.nn.silu(og[:, hs])).astype(o_ref.dtype)

    @pl.when(tb == n_tblk - 1)
    def _():
        for h in range(HG_HEADS):
            sfin_ref[0, h] = st_ref[h].T


def _hgrn(proj, lb, norm_w, s0, b, t):
    rows_in = min(t, HG_ROWS)
    n_tblk = t // rows_in
    assert rows_in * n_tblk == t and rows_in % 8 == 0
    has_state = s0 is not None
    n_seq = HG_SEQS if b % HG_SEQS == 0 else 1
    if s0 is None:
        s0 = jnp.zeros((n_seq, HG_HEADS, HG_DK, HG_DV), F32)
    width = HG_HEADS * HG_DK
    proj3 = proj.reshape(b, t, proj.shape[1])
    col = lambda c: pl.BlockSpec((n_seq, rows_in, width), lambda i, j: (i, j, c // width))
    state = (n_seq, HG_HEADS, HG_DK, HG_DV)
    o, s_fin = pl.pallas_call(
        functools.partial(_hgrn_kernel, n_seq=n_seq, rows_in=rows_in, n_tblk=n_tblk, has_state=has_state),
        grid=(b // n_seq, n_tblk),
        in_specs=[col(C_QB), col(C_FB), col(C_IB), col(C_OG),
                  pl.BlockSpec((1, width), lambda i, j: (0, 0)),
                  pl.BlockSpec((1, HG_DV), lambda i, j: (0, 0)),
                  pl.BlockSpec(state, (lambda i, j: (i, 0, 0, 0)) if has_state else (lambda i, j: (0, 0, 0, 0)))],
        out_specs=[pl.BlockSpec((n_seq, rows_in, width), lambda i, j: (i, j, 0)),
                   pl.BlockSpec(state, lambda i, j: (i, 0, 0, 0))],
        out_shape=[jax.ShapeDtypeStruct((b, t, width), BF16),
                   jax.ShapeDtypeStruct((b, HG_HEADS, HG_DK, HG_DV), F32)],
        scratch_shapes=[pltpu.VMEM((n_seq, HG_HEADS, HG_DV, HG_DK), F32), pltpu.VMEM((n_seq, HG_ROWS, width), F32)],
        compiler_params=_cparams("parallel", "arbitrary"),
        name="hgrn2",
    )(proj3, proj3, proj3, proj3, lb.reshape(1, width), norm_w.reshape(1, HG_DV), s0)
    return o.reshape(b * t, width), s_fin


def _decoder_layer(x, past, lb, params):
    (w_in, w_a, w_b, w_o, pe, w1, w2, hg_norm, ln1, ln2, ln3, ln4, w_up, cw, cb, w_dn) = params
    b, t, d = x.shape
    m = b * t
    x2 = x.reshape(m, d)
    tm = min(MM_ROWS, m)
    proj = _in_projection(x2, ln1, *w_in, tm)
    kv_shape = (b, t, 2, NSA_GROUPS, NSA_HD)
    kvc, kvs, kvw = (a.reshape(kv_shape) for a in _kv_relayout(proj, min(CONV_ROWS, m)))
    if past is None:
        ckv = _compress_seq(proj, b, t, pe, w1, w2)
        o_a = _nsa_seq(proj, ckv, b, t)
        new_win = kvw[:, t - min(NSA_WINDOW, t):]
        s0, conv_buf = None, None
    else:
        cache_c, cache_s, page_table_flat, n_pages, page, win_buf, s0, conv_buf = past
        n_win = win_buf.shape[1]
        ckv = _compress_paged(cache_c, page_table_flat, b, n_pages, page, pe, w1, w2)
        o_a = _nsa_step(proj, ckv, cache_s, page_table_flat, win_buf.reshape(-1, NSA_HD), b, t, n_pages, page, n_win)
        new_win = jnp.concatenate([win_buf, kvw], axis=1)[:, t:]
    o_hg, s_fin = _hgrn(proj, lb, hg_norm, s0, b, t)
    mixed = _gated_merge(o_a.astype(BF16), o_hg, w_a, w_b, proj, tm, 512)
    x1 = _matmul_norm_res(mixed, w_o, x2, ln2, min(OUT_ROWS, m), w_o.shape[0])
    u = _norm_matmul(x1, ln3, w_up, tm, UP_TN)
    if past is None:
        y = _conv_glu_down(u, cw, cb, w_dn, x1, ln4, t, min(CONV_ROWS, t), 512)
        new_conv = u.reshape(b, t, 2 * D_FF)[:, t - (CONV_W - 1):]
    else:
        act = _conv_glu_step(u.reshape(b, t, 2 * D_FF), conv_buf, cw, cb, min(64, b), 512).reshape(m, D_FF)
        new_conv = jnp.concatenate([conv_buf, u.reshape(b, t, 2 * D_FF)], axis=1)[:, t:] if t < CONV_W - 1 else \
            u.reshape(b, t, 2 * D_FF)[:, t - (CONV_W - 1):]
        y = _matmul_norm_res(act, w_dn, x1, ln4, tm, 512)
    return y.reshape(b, t, d), (kvc, kvs, new_win, s_fin, new_conv)


def _split_w_in(w):
    o_rest = NSA_HEADS * NSA_HD + 3 * KV_COLS + 3 * NSA_HEADS
    edge = jnp.pad(w[:, :o_rest].astype(BF16), ((0, 0), (0, EDGE_COLS - o_rest)))
    return edge, w[:, o_rest:].astype(BF16)


def kernel(x_prompt, x_sample, cache_cmp_kv, cache_sel_kv, state_win_kv, state_hgrn, state_conv, page_table,
           w_in, w_branch_a, w_branch_b, w_out, cmp_pe, cmp_w1, cmp_w2, hg_lb_raw, hg_norm_w, ln_mix_pre,
           ln_mix_post, ln_ffn_pre, ln_ffn_post, w_up, conv_w, conv_b, w_down):
    depth = w_in.shape[0]
    dec_b, n_pages = page_table.shape
    page = cache_cmp_kv.shape[2]
    lb_all = jnp.cumsum(jax.nn.softmax(hg_lb_raw.astype(F32), axis=0), axis=0)
    pt_flat = page_table.reshape(-1).astype(jnp.int32)
    y_p, y_s = x_prompt, x_sample
    new_p, new_s = [], []
    for l in range(depth):
        params = (_split_w_in(w_in[l]), w_branch_a[l].astype(BF16), w_branch_b[l].astype(BF16),
                  w_out[l].astype(BF16), cmp_pe[l].transpose(1, 0, 2)[:, :, None, :],
                  cmp_w1[l].astype(BF16).transpose(1, 0, 2, 3).reshape(
                      2, NSA_BLOCK // CMP_STACK, CMP_STACK * NSA_HD, NSA_HD),
                  cmp_w2[l].astype(BF16), hg_norm_w[l], ln_mix_pre[l], ln_mix_post[l], ln_ffn_pre[l],
                  ln_ffn_post[l], w_up[l].astype(BF16), conv_w[l], conv_b[l], w_down[l].astype(BF16))
        y_p, st_p = _decoder_layer(y_p, None, lb_all[l], params)
        past = (cache_cmp_kv[l].reshape(-1, NSA_HD), cache_sel_kv[l].reshape(-1, NSA_HD),
                pt_flat, n_pages, page, state_win_kv[l], state_hgrn[l], state_conv[l])
        y_s, st_s = _decoder_layer(y_s, past, lb_all[l], params)
        new_p.append(st_p)
        new_s.append(st_s)

    def stack(group, i):
        return jnp.stack([st[i] for st in group], axis=0)

    return (y_p, y_s, stack(new_p, 0), stack(new_s, 0), stack(new_p, 1), stack(new_s, 1), stack(new_p, 2),
            stack(new_s, 2), stack(new_p, 3), stack(new_s, 3), stack(new_p, 4), stack(new_s, 4))
```
